```python
import math
import functools
import jax
import jax.numpy as jnp
from jax import lax
import numpy as np

D_MODEL = 1024
BATCH = 32
SEQ = 2048
DEPTH = 1
DEC_BATCH = 128
DEC_SEQ = 4
PAST_LEN = 8192
PAGE_SIZE = 128

NSA_HEADS = 8
NSA_KV_HEADS = 2
NSA_GROUP = NSA_HEADS // NSA_KV_HEADS
HEAD_DIM = 64
CMP_BLOCK = 32
CMP_STRIDE = 16
CMP_HIDDEN = 2 * HEAD_DIM
SEL_BLOCK = 64
N_SEL = 16
WINDOW = 512
Q_BLOCK = 128
ROPE_THETA = 10000.0
HGRN_HEADS = 4
HGRN_DK = 128
HGRN_DV = 128
HGRN_CHUNK = 64
N_GROUPS = 4
EXPERTS_PER_GROUP = 8
N_EXPERTS = N_GROUPS * EXPERTS_PER_GROUP
TOP_K_EXPERTS = 2
D_EXPERT = 512
MOE_BLOCK = 128
NSA_WIDTH = NSA_HEADS * HEAD_DIM
KV_WIDTH = 2 * NSA_KV_HEADS * HEAD_DIM
HGRN_KW = HGRN_HEADS * HGRN_DK
HGRN_VW = HGRN_HEADS * HGRN_DV
IN_SPLITS = (NSA_WIDTH, KV_WIDTH, KV_WIDTH, KV_WIDTH, 3 * NSA_HEADS, HGRN_KW, HGRN_KW, HGRN_VW, HGRN_VW, D_MODEL, D_MODEL)
IN_WIDTH = sum(IN_SPLITS)
NORM_EPS = 1e-6
NEG = -1e9
BIG = 1e9

kernel_name = "nsa_hgrn2_hmoe_hybrid_step"


def rmsnorm(x, g):
    xf = x.astype(jnp.float32)
    y = xf * lax.rsqrt(jnp.mean(xf * xf, axis=-1, keepdims=True) + NORM_EPS)
    return (y * g.astype(jnp.float32)).astype(x.dtype)


def rope(x, pos):
    half = x.shape[-1] // 2
    inv = ROPE_THETA ** (-jnp.arange(half, dtype=jnp.float32) / half)
    ang = pos.astype(jnp.float32)[:, None] * inv[None, :]
    cos = jnp.cos(ang)[:, None, :]
    sin = jnp.sin(ang)[:, None, :]
    xf = x.astype(jnp.float32)
    x1, x2 = xf[..., :half], xf[..., half:]
    return jnp.concatenate([x1 * cos - x2 * sin, x2 * cos + x1 * sin], axis=-1).astype(x.dtype)


def masked_softmax(s, ok):
    s = s.astype(jnp.float32)
    m = jnp.max(jnp.where(ok, s, NEG), axis=-1, keepdims=True)
    e = jnp.exp(jnp.where(ok, s - m, NEG))
    return e / jnp.maximum(jnp.sum(e, axis=-1, keepdims=True), 1e-30)


def shared_attend(q, k, v, ok):
    s = jnp.einsum('tkgd,lkd->kgtl', q, k) * HEAD_DIM ** -0.5
    p = masked_softmax(s, ok)
    return jnp.einsum('kgtl,lkd->tkgd', p.astype(v.dtype), v), p


def gathered_attend(q, kg, vg, ok):
    s = jnp.einsum('tkgd,ktjd->kgtj', q, kg) * HEAD_DIM ** -0.5
    p = masked_softmax(s, ok[:, None])
    return jnp.einsum('kgtj,ktjd->tkgd', p.astype(vg.dtype), vg)


def window_mask(tq, kpos):
    kp = kpos[None, :]
    tt = tq[:, None]
    return (kp >= 0) & (kp <= tt) & (kp > tt - WINDOW)


def compress_kv(kv, pe, w1, b1, w2, b2):
    n_ch = kv.shape[-4] // CMP_STRIDE
    ch = kv[..., : n_ch * CMP_STRIDE, :, :, :].reshape(kv.shape[:-4] + (n_ch, CMP_STRIDE, 2, NSA_KV_HEADS, HEAD_DIM))
    w1r = w1.reshape(2, 2, CMP_STRIDE, HEAD_DIM, CMP_HIDDEN)
    h_a = jnp.einsum('...nsckd,csdh->...nckh', ch, w1r[:, 0])
    h_b = jnp.einsum('...nsckd,csdh->...nckh', ch, w1r[:, 1])
    c0 = jnp.einsum('crd,crdh->ch', pe, w1.reshape(2, CMP_BLOCK, HEAD_DIM, CMP_HIDDEN)) + b1
    hid = jax.nn.gelu(h_a[..., :-1, :, :, :] + h_b[..., 1:, :, :, :] + c0[:, None, :])
    return jnp.einsum('...nckh,chd->...nckd', hid, w2) + b2[:, None, :]


def cmp_branch(qg, kv_rows, tpos, cmp_params):
    ckv = compress_kv(kv_rows, *cmp_params)
    n_cmp = ckv.shape[1]
    c_end = jnp.arange(n_cmp) * CMP_STRIDE + CMP_BLOCK - 1
    ok = c_end[None, :] <= tpos[:, None]
    return jax.vmap(shared_attend, in_axes=(0, 0, 0, None))(qg, ckv[:, :, 0], ckv[:, :, 1], ok)


def cover_matrix(n_cmp, n_sel):
    start = jnp.arange(n_cmp)[:, None] * CMP_STRIDE
    j = jnp.arange(n_sel)[None, :]
    return ((start < (j + 1) * SEL_BLOCK) & (start + CMP_BLOCK > j * SEL_BLOCK)).astype(jnp.float32)


def select_blocks(p_cmp, tpos, n_sel):
    imp = jnp.einsum('...kgtn,nj->...ktj', p_cmp, cover_matrix(p_cmp.shape[-1], n_sel))
    j = jnp.arange(n_sel)[None, :]
    cur = (tpos // SEL_BLOCK)[:, None]
    valid = j * SEL_BLOCK <= tpos[:, None]
    forced = (j == 0) | (j == cur) | (j == cur - 1)
    score = jnp.where(valid, jnp.where(forced, BIG, imp), NEG)
    vals, idx = lax.top_k(score, min(N_SEL, n_sel))
    return idx, vals > 0.5 * NEG


def sel_attend(q, g, idx, blk_ok, tq):
    kvh, t_len, n_top = idx.shape
    kpos = idx[..., None] * SEL_BLOCK + jnp.arange(SEL_BLOCK)
    ok = blk_ok[..., None] & (kpos <= tq[None, :, None, None])
    j_len = n_top * SEL_BLOCK
    kg = g[..., 0, :].reshape(kvh, t_len, j_len, HEAD_DIM)
    vg = g[..., 1, :].reshape(kvh, t_len, j_len, HEAD_DIM)
    return gathered_attend(q, kg, vg, ok.reshape(kvh, t_len, j_len))


def nsa_prompt(q, qr, kvc, kvs, kvw, cmp_params):
    b_sz, s_len = q.shape[:2]
    tpos = jnp.arange(s_len)
    qg = q.reshape(b_sz, s_len, NSA_KV_HEADS, NSA_GROUP, HEAD_DIM)
    qrg = qr.reshape(b_sz, s_len, NSA_KV_HEADS, NSA_GROUP, HEAD_DIM)
    o_cmp, p_cmp = cmp_branch(qg, kvc, tpos, cmp_params)
    n_sel = s_len // SEL_BLOCK
    idx, blk_ok = select_blocks(p_cmp, tpos, n_sel)
    sblk = kvs.reshape(b_sz, n_sel, SEL_BLOCK, 2, NSA_KV_HEADS, HEAD_DIM)
    wpad = jnp.pad(kvw, ((0, 0), (WINDOW, 0), (0, 0), (0, 0), (0, 0)))
    hh = jnp.arange(NSA_KV_HEADS)[:, None, None]
    n_qb = s_len // Q_BLOCK

    def item(args):
        b, qb = args
        t0 = qb * Q_BLOCK
        tq = t0 + jnp.arange(Q_BLOCK)
        q_blk = lax.dynamic_slice_in_dim(qrg[b], t0, Q_BLOCK, axis=0)
        ib = lax.dynamic_slice_in_dim(idx[b], t0, Q_BLOCK, axis=1)
        okb = lax.dynamic_slice_in_dim(blk_ok[b], t0, Q_BLOCK, axis=1)
        g = sblk[b][ib, :, :, hh, :]
        o_s = sel_attend(q_blk, g, ib, okb, tq)
        wk = lax.dynamic_slice_in_dim(wpad[b], t0, Q_BLOCK + WINDOW, axis=0)
        wpos = t0 - WINDOW + jnp.arange(Q_BLOCK + WINDOW)
        o_w, _ = shared_attend(q_blk, wk[:, 0], wk[:, 1], window_mask(tq, wpos))
        return o_s, o_w

    bids = jnp.repeat(jnp.arange(b_sz), n_qb)
    qids = jnp.tile(jnp.arange(n_qb), b_sz)
    o_slc, o_win = lax.map(item, (bids, qids))
    shp = (b_sz, s_len, NSA_HEADS, HEAD_DIM)
    new_win = kvw[:, s_len - min(WINDOW, s_len):]
    return o_cmp.reshape(shp), o_slc.reshape(shp), o_win.reshape(shp), (kvc, kvs, new_win)


def nsa_sample(q, qr, kvc, kvs, kvw, cache_cmp, cache_slc, cache_win, page_table, cmp_params):
    b_sz, t_len = q.shape[:2]
    past = page_table.shape[1] * PAGE_SIZE
    total = past + t_len
    tpos = past + jnp.arange(t_len)
    qg = q.reshape(b_sz, t_len, NSA_KV_HEADS, NSA_GROUP, HEAD_DIM)
    qrg = qr.reshape(b_sz, t_len, NSA_KV_HEADS, NSA_GROUP, HEAD_DIM)
    past_cmp = cache_cmp[page_table].reshape(b_sz, past, 2, NSA_KV_HEADS, HEAD_DIM)
    full_cmp = jnp.concatenate([past_cmp, kvc.astype(past_cmp.dtype)], axis=1)
    o_cmp, p_cmp = cmp_branch(qg, full_cmp, tpos, cmp_params)
    n_sel = -(-total // SEL_BLOCK)
    idx, blk_ok = select_blocks(p_cmp, tpos, n_sel)
    n_past_blk = past // SEL_BLOCK
    n_new_blk = n_sel - n_past_blk
    blk_per_page = PAGE_SIZE // SEL_BLOCK
    pool_blk = cache_slc.reshape(-1, SEL_BLOCK, 2, NSA_KV_HEADS, HEAD_DIM)
    new_blk = jnp.pad(kvs, ((0, 0), (0, n_new_blk * SEL_BLOCK - t_len), (0, 0), (0, 0), (0, 0)))
    new_blk = new_blk.reshape(b_sz, n_new_blk, SEL_BLOCK, 2, NSA_KV_HEADS, HEAD_DIM)
    win_rows = jnp.concatenate([cache_win, kvw.astype(cache_win.dtype)], axis=1)
    wpos = past - cache_win.shape[1] + jnp.arange(win_rows.shape[1])
    wok = window_mask(tpos, wpos)
    hh = jnp.arange(NSA_KV_HEADS)[:, None, None]

    def item(b):
        ib = idx[b]
        jc = jnp.minimum(ib, n_past_blk - 1)
        phys = page_table[b][jc // blk_per_page] * blk_per_page + jc % blk_per_page
        g_past = pool_blk[phys, :, :, hh, :]
        g_new = new_blk[b][jnp.clip(ib - n_past_blk, 0, n_new_blk - 1), :, :, hh, :]
        g = jnp.where((ib < n_past_blk)[..., None, None, None], g_past, g_new.astype(g_past.dtype))
        o_s = sel_attend(qrg[b], g, ib, blk_ok[b], tpos)
        o_w, _ = shared_attend(qrg[b], win_rows[b][:, 0], win_rows[b][:, 1], wok)
        return o_s, o_w

    o_slc, o_win = lax.map(item, jnp.arange(b_sz))
    shp = (b_sz, t_len, NSA_HEADS, HEAD_DIM)
    new_win = win_rows[:, win_rows.shape[1] - min(WINDOW, total):]
    return o_cmp.reshape(shp), o_slc.reshape(shp), o_win.reshape(shp), (kvc, kvs, new_win)


def gla_scan(q, k, v, logf, s0, chunk):
    b_sz, t_len, n_h, _ = q.shape
    n_c = t_len // chunk

    def to_chunks(a):
        return a.reshape(b_sz, n_c, chunk, n_h, a.shape[-1]).transpose(1, 0, 3, 2, 4)

    tri = jnp.tril(jnp.ones((chunk, chunk), bool))[:, :, None]

    def step(s, xs):
        qc, kc, vc, fc = xs
        cum = jnp.cumsum(fc, axis=2)
        diff = cum[:, :, :, None, :] - cum[:, :, None, :, :]
        decay = jnp.exp(jnp.where(tri, diff, NEG))
        att = jnp.einsum('bhtk,bhsk,bhtsk->bhts', qc, kc, decay)
        o = jnp.einsum('bhts,bhsv->bhtv', att, vc) + jnp.einsum('bhtk,bhkv->bhtv', qc * jnp.exp(cum), s)
        last = cum[:, :, -1:, :]
        s = jnp.exp(last[:, :, 0, :, None]) * s + jnp.einsum('bhsk,bhsv->bhkv', kc * jnp.exp(last - cum), vc)
        return s, o

    s_fin, o = lax.scan(step, s0, (to_chunks(q), to_chunks(k), to_chunks(v), to_chunks(logf)))
    return o.transpose(1, 0, 3, 2, 4).reshape(b_sz, t_len, n_h, v.shape[-1]), s_fin


def hgrn2(hq, hf, hi, hg, lb, gnorm, s0):
    b_sz, t_len, _ = hq.shape
    shp_k = (b_sz, t_len, HGRN_HEADS, HGRN_DK)
    shp_v = (b_sz, t_len, HGRN_HEADS, HGRN_DV)
    q = (jax.nn.silu(hq.astype(jnp.float32)) * HGRN_DK ** -0.5).reshape(shp_k)
    f = (lb + (1.0 - lb) * jax.nn.sigmoid(hf.astype(jnp.float32))).reshape(shp_k)
    v = hi.astype(jnp.float32).reshape(shp_v)
    o, s_new = gla_scan(q, 1.0 - f, v, jnp.log(f), s0.astype(jnp.float32), math.gcd(t_len, HGRN_CHUNK))
    o = rmsnorm(o, gnorm) * jax.nn.silu(hg.astype(jnp.float32)).reshape(shp_v)
    return o.reshape(b_sz, t_len, HGRN_VW).astype(hq.dtype), s_new


def swiglu(x, w_gate, w_up, w_down):
    return (jax.nn.silu(x @ w_gate) * (x @ w_up)) @ w_down


def moe_dispatch(h, eid, gate, w_gate, w_up, w_down):
    n, d = h.shape
    k = eid.shape[1]
    a = n * k
    ne = w_gate.shape[0]
    flat = eid.reshape(a)
    order = jnp.argsort(flat)
    sorted_e = flat[order]
    counts = jnp.zeros((ne,), jnp.int32).at[flat].add(1)
    padded = (counts + MOE_BLOCK - 1) // MOE_BLOCK * MOE_BLOCK
    pad_end = jnp.cumsum(padded)
    pad_start = pad_end - padded
    start = jnp.cumsum(counts) - counts
    dest = pad_start[sorted_e] + jnp.arange(a, dtype=jnp.int32) - start[sorted_e]
    n_blocks = -(-(a + ne * (MOE_BLOCK - 1)) // MOE_BLOCK)
    rows_total = n_blocks * MOE_BLOCK
    src = jnp.full((rows_total,), n, jnp.int32).at[dest].set((order // k).astype(jnp.int32))
    h_pad = jnp.concatenate([h, jnp.zeros((1, d), h.dtype)], axis=0)
    xb = h_pad[src].reshape(n_blocks, MOE_BLOCK, d)
    blk_e = jnp.minimum(jnp.searchsorted(pad_end, jnp.arange(n_blocks, dtype=jnp.int32) * MOE_BLOCK, side='right'), ne - 1)

    def run(args):
        x_rows, e = args
        return swiglu(x_rows, w_gate[e], w_up[e], w_down[e])

    yb = lax.map(run, (xb, blk_e)).reshape(rows_total, d)
    y = jnp.zeros((a, d), yb.dtype).at[order].set(yb[dest]).reshape(n, k, d)
    return jnp.einsum('nkd,nk->nd', y, gate.astype(y.dtype))


def hier_moe(h, w_rg, b_rg, w_re, b_re, w_gate, w_up, w_down):
    lg = jnp.dot(h, w_rg).astype(jnp.float32) + b_rg.astype(jnp.float32)
    pg_top, g_top = lax.top_k(jax.nn.softmax(lg, axis=-1), 1)
    le_all = jnp.einsum('nd,gde->nge', h, w_re).astype(jnp.float32) + b_re.astype(jnp.float32)
    le = jnp.take_along_axis(le_all, g_top[:, :, None], axis=1)[:, 0]
    lv, e_top = lax.top_k(le, TOP_K_EXPERTS)
    gate = jax.nn.softmax(lv, axis=-1) * pg_top
    eid = g_top * EXPERTS_PER_GROUP + e_top
    return moe_dispatch(h, eid, gate, w_gate, w_up, w_down)


def trunk_layer(x, pos, nsa_fn, s0, norm1, w_in, lb, gnorm, w_proj_a, w_proj_b, w_out, norm2,
                w_rg, b_rg, w_re, b_re, w_gate, w_up, w_down):
    b_sz, t_len, _ = x.shape
    h = rmsnorm(x, norm1)
    z = jnp.einsum('btd,de->bte', h, w_in)
    q, k_c, k_s, k_w, g_nsa, hq, hf, hi, hg, m_a, m_b = jnp.split(z, np.cumsum(IN_SPLITS)[:-1].tolist(), axis=-1)
    q = q.reshape(b_sz, t_len, NSA_HEADS, HEAD_DIM)
    qr = rope(q, pos)
    kv_shape = (b_sz, t_len, 2, NSA_KV_HEADS, HEAD_DIM)
    kvc = k_c.reshape(kv_shape)
    kvs = k_s.reshape(kv_shape)
    kvs = jnp.stack([rope(kvs[:, :, 0], pos), kvs[:, :, 1]], axis=2)
    kvw = k_w.reshape(kv_shape)
    kvw = jnp.stack([rope(kvw[:, :, 0], pos), kvw[:, :, 1]], axis=2)
    o_cmp, o_slc, o_win, nsa_state = nsa_fn(q, qr, kvc, kvs, kvw)
    gate = jax.nn.sigmoid(g_nsa.astype(jnp.float32)).reshape(b_sz, t_len, NSA_HEADS, 3, 1).astype(x.dtype)
    o_a = (gate[..., 0, :] * o_cmp + gate[..., 1, :] * o_slc + gate[..., 2, :] * o_win).reshape(b_sz, t_len, NSA_WIDTH)
    o_b, s_new = hgrn2(hq, hf, hi, hg, lb, gnorm, s0)
    merged = jax.nn.sigmoid(m_a) * (o_a @ w_proj_a) + jax.nn.sigmoid(m_b) * (o_b @ w_proj_b)
    x = x + merged @ w_out
    h2 = rmsnorm(x, norm2).reshape(b_sz * t_len, D_MODEL)
    x = x + hier_moe(h2, w_rg, b_rg, w_re, b_re, w_gate, w_up, w_down).reshape(b_sz, t_len, D_MODEL)
    return x, nsa_state, s_new


def setup_inputs(seed: int = 0) -> dict:
    key = jax.random.key(seed)
    ks = jax.random.split(key, 28)
    n_pages = PAST_LEN // PAGE_SIZE
    n_phys = DEC_BATCH * n_pages * 5 // 4
    win_buf = min(WINDOW, PAST_LEN)
    kv_row = (2, NSA_KV_HEADS, HEAD_DIM)

    def nrm(k, shape, scale):
        return jax.random.normal(k, shape, jnp.float32) * scale

    def gain(k, shape):
        return 1.0 + nrm(k, shape, 0.02)

    page_table = jax.random.permutation(ks[6], n_phys)[: DEC_BATCH * n_pages].reshape(DEC_BATCH, n_pages).astype(jnp.int32)
    return {
        "x_prompt": nrm(ks[0], (BATCH, SEQ, D_MODEL), 1.0),
        "x_sample": nrm(ks[1], (DEC_BATCH, DEC_SEQ, D_MODEL), 1.0),
        "cache_cmp_kv": nrm(ks[2], (DEPTH, n_phys, PAGE_SIZE) + kv_row, 1.0),
        "cache_slc_kv": nrm(ks[3], (DEPTH, n_phys, PAGE_SIZE) + kv_row, 1.0),
        "cache_win_kv": nrm(ks[4], (DEPTH, DEC_BATCH, win_buf) + kv_row, 1.0),
        "state_hgrn": nrm(ks[5], (DEPTH, DEC_BATCH, HGRN_HEADS, HGRN_DK, HGRN_DV), 0.5),
        "page_table": page_table,
        "norm1": gain(ks[7], (DEPTH, D_MODEL)),
        "w_in": nrm(ks[8], (DEPTH, D_MODEL, IN_WIDTH), D_MODEL ** -0.5),
        "cmp_pe": nrm(ks[9], (DEPTH, 2, CMP_BLOCK, HEAD_DIM), 0.1),
        "cmp_w1": nrm(ks[10], (DEPTH, 2, CMP_BLOCK * HEAD_DIM, CMP_HIDDEN), (CMP_BLOCK * HEAD_DIM) ** -0.5),
        "cmp_b1": nrm(ks[11], (DEPTH, 2, CMP_HIDDEN), 0.01),
        "cmp_w2": nrm(ks[12], (DEPTH, 2, CMP_HIDDEN, HEAD_DIM), CMP_HIDDEN ** -0.5),
        "cmp_b2": nrm(ks[13], (DEPTH, 2, HEAD_DIM), 0.01),
        "hgrn_lb_logits": nrm(ks[14], (DEPTH + 1, HGRN_KW), 0.5),
        "hgrn_gnorm": gain(ks[15], (DEPTH, HGRN_DV)),
        "w_proj_a": nrm(ks[16], (DEPTH, NSA_WIDTH, D_MODEL), NSA_WIDTH ** -0.5),
        "w_proj_b": nrm(ks[17], (DEPTH, HGRN_VW, D_MODEL), HGRN_VW ** -0.5),
        "w_out": nrm(ks[18], (DEPTH, D_MODEL, D_MODEL), D_MODEL ** -0.5),
        "norm2": gain(ks[19], (DEPTH, D_MODEL)),
        "w_router_group": nrm(ks[20], (DEPTH, D_MODEL, N_GROUPS), D_MODEL ** -0.5),
        "b_router_group": nrm(ks[21], (DEPTH, N_GROUPS), 0.01),
        "w_router_expert": nrm(ks[22], (DEPTH, N_GROUPS, D_MODEL, EXPERTS_PER_GROUP), D_MODEL ** -0.5),
        "b_router_expert": nrm(ks[23], (DEPTH, N_GROUPS, EXPERTS_PER_GROUP), 0.01),
        "w_exp_gate": nrm(ks[24], (DEPTH, N_EXPERTS, D_MODEL, D_EXPERT), D_MODEL ** -0.5),
        "w_exp_up": nrm(ks[25], (DEPTH, N_EXPERTS, D_MODEL, D_EXPERT), D_MODEL ** -0.5),
        "w_exp_down": nrm(ks[26], (DEPTH, N_EXPERTS, D_EXPERT, D_MODEL), D_EXPERT ** -0.5),
        "final_norm": gain(ks[27], (D_MODEL,)),
    }


def _stack_field(rows, i):
    return jnp.stack([r[i] for r in rows], axis=0)


def reference(x_prompt, x_sample, cache_cmp_kv, cache_slc_kv, cache_win_kv, state_hgrn, page_table,
              norm1, w_in, cmp_pe, cmp_w1, cmp_b1, cmp_w2, cmp_b2, hgrn_lb_logits, hgrn_gnorm,
              w_proj_a, w_proj_b, w_out, norm2, w_router_group, b_router_group, w_router_expert,
              b_router_expert, w_exp_gate, w_exp_up, w_exp_down, final_norm):
    b_sz, s_len = x_prompt.shape[:2]
    past = page_table.shape[1] * PAGE_SIZE
    pos_p = jnp.arange(s_len)
    pos_s = past + jnp.arange(x_sample.shape[1])
    lb_all = jnp.cumsum(jax.nn.softmax(hgrn_lb_logits.astype(jnp.float32), axis=0), axis=0)
    xp, xs = x_prompt, x_sample
    rows_p, rows_s = [], []
    for l in range(DEPTH):
        cmp_params = (cmp_pe[l], cmp_w1[l], cmp_b1[l], cmp_w2[l], cmp_b2[l])
        lw = (norm1[l], w_in[l], lb_all[l], hgrn_gnorm[l], w_proj_a[l], w_proj_b[l], w_out[l], norm2[l],
              w_router_group[l], b_router_group[l], w_router_expert[l], b_router_expert[l],
              w_exp_gate[l], w_exp_up[l], w_exp_down[l])
        nsa_p = functools.partial(nsa_prompt, cmp_params=cmp_params)
        nsa_s = functools.partial(nsa_sample, cache_cmp=cache_cmp_kv[l], cache_slc=cache_slc_kv[l],
                                  cache_win=cache_win_kv[l], page_table=page_table, cmp_params=cmp_params)
        s0 = jnp.zeros((b_sz, HGRN_HEADS, HGRN_DK, HGRN_DV), jnp.float32)
        xp, (pc, ps, pw), ph = trunk_layer(xp, pos_p, nsa_p, s0, *lw)
        xs, (sc, ss, sw), sh = trunk_layer(xs, pos_s, nsa_s, state_hgrn[l], *lw)
        rows_p.append((pc, ps, pw, ph))
        rows_s.append((sc, ss, sw, sh))
    y_prompt = rmsnorm(xp, final_norm)
    y_sample = rmsnorm(xs, final_norm)
    return (y_prompt, y_sample,
            _stack_field(rows_p, 0), _stack_field(rows_s, 0),
            _stack_field(rows_p, 1), _stack_field(rows_s, 1),
            _stack_field(rows_p, 2), _stack_field(rows_s, 2),
            _stack_field(rows_p, 3), _stack_field(rows_s, 3))
```

```python
import functools
import math

import numpy as np
import jax
import jax.numpy as jnp
from jax import lax
from jax.experimental import pallas as pl
from jax.experimental.pallas import tpu as pltpu

D_MODEL = 1024
PAGE_SIZE = 128
NSA_HEADS = 8
NSA_KV_HEADS = 2
NSA_GROUP = NSA_HEADS // NSA_KV_HEADS
HEAD_DIM = 64
CMP_BLOCK = 32
CMP_STRIDE = 16
CMP_HIDDEN = 2 * HEAD_DIM
SEL_BLOCK = 64
N_SEL = 16
WINDOW = 512
ROPE_THETA = 10000.0
HGRN_HEADS = 4
HGRN_DK = 128
HGRN_DV = 128
N_GROUPS = 4
EXPERTS_PER_GROUP = 8
N_EXPERTS = N_GROUPS * EXPERTS_PER_GROUP
TOP_K_EXPERTS = 2
D_EXPERT = 512
NSA_WIDTH = NSA_HEADS * HEAD_DIM
KV_WIDTH = 2 * NSA_KV_HEADS * HEAD_DIM
HGRN_KW = HGRN_HEADS * HGRN_DK
HGRN_VW = HGRN_HEADS * HGRN_DV
IN_SPLITS = (NSA_WIDTH, KV_WIDTH, KV_WIDTH, KV_WIDTH, 3 * NSA_HEADS, HGRN_KW, HGRN_KW, HGRN_VW, HGRN_VW,
             D_MODEL, D_MODEL)
NORM_EPS = 1e-6
NEG = -1e9
BIG = 1e9

LANES = 128
VMEM_LIMIT = 56 * 1024 * 1024
TM_PROJ = 512
TQ = 128
TK_SLC = 256
HG_CHUNK = 64
HG_TS = 256
TM_POST = 256
TR_ROUTE = 512
MOE_BM = 256
T_DISP = 256

F32 = jnp.float32
BF16 = jnp.bfloat16
HIGHEST = lax.Precision.HIGHEST


def _dot(a, b, precision=None):
    return jnp.dot(a, b, preferred_element_type=F32, precision=precision)


def _dot_nt(a, b, precision=None):
    return lax.dot_general(a, b, (((1,), (1,)), ((), ())), preferred_element_type=F32, precision=precision)


def _tile(n, target, align=8):
    if n <= target:
        return n
    t = target - target % align
    while n % t:
        t -= align
    return t


def _cparams(sem):
    return pltpu.CompilerParams(dimension_semantics=sem, vmem_limit_bytes=VMEM_LIMIT)


def _masked_softmax(s, ok):
    m = jnp.max(jnp.where(ok, s, NEG), axis=-1, keepdims=True)
    e = jnp.exp(jnp.where(ok, s - m, NEG))
    return e / jnp.maximum(jnp.sum(e, axis=-1, keepdims=True), 1e-30)


def _rms(x, g):
    return x * lax.rsqrt(jnp.mean(x * x, axis=-1, keepdims=True) + NORM_EPS) * g


_A_Q, _A_QR, _A_KC, _A_KS, _A_KSR, _A_KW, _A_KWR, _A_G, _A_END = 0, 512, 1024, 1280, 1536, 1664, 1920, 2048, 2176


def _attn_proj_kernel(x_ref, g_ref, w_ref, cos_ref, sin_ref,
                      qu_ref, qr_ref, kvc_ref, kvs_ref, kvw_ref, kvsb_ref, kvwb_ref, gate_ref):
    h = _rms(x_ref[...], g_ref[...]).astype(BF16)
    z = _dot(h, w_ref[...])
    cos = cos_ref[...]
    sin = sin_ref[...]
    q = z[:, _A_Q:_A_QR]
    qu_ref[...] = q.astype(BF16)
    qr_ref[...] = (q * cos + z[:, _A_QR:_A_KC] * sin).astype(BF16)
    kvc_ref[...] = z[:, _A_KC:_A_KS]
    ck = cos[:, :LANES]
    sk = sin[:, :LANES]
    ks = z[:, _A_KS:_A_KS + 128] * ck + z[:, _A_KSR:_A_KW] * sk
    vs = z[:, _A_KS + 128:_A_KSR]
    kvs_ref[:, :128] = ks
    kvs_ref[:, 128:] = vs
    kvsb_ref[:, :128] = ks.astype(BF16)
    kvsb_ref[:, 128:] = vs.astype(BF16)
    kw = z[:, _A_KW:_A_KW + 128] * ck + z[:, _A_KWR:_A_G] * sk
    vw = z[:, _A_KW + 128:_A_KWR]
    kvw_ref[:, :128] = kw
    kvw_ref[:, 128:] = vw
    kvwb_ref[:, :128] = kw.astype(BF16)
    kvwb_ref[:, 128:] = vw.astype(BF16)
    gate_ref[...] = jax.nn.sigmoid(z[:, _A_G:_A_END])


def _rot_cols(w):
    d, n = w.shape
    w4 = w.reshape(d, n // HEAD_DIM, 2, HEAD_DIM // 2)
    return jnp.stack([-w4[:, :, 1], w4[:, :, 0]], axis=2).reshape(d, n)


def _pair_perm():
    idx = []
    for g in range(NSA_GROUP):
        for kvh in range(NSA_KV_HEADS):
            h = kvh * NSA_GROUP + g
            idx.extend(range(h * HEAD_DIM, (h + 1) * HEAD_DIM))
    return np.asarray(idx, np.int32)


def _attn_weight(w_in):
    offs = np.cumsum((0,) + IN_SPLITS)
    scale = HEAD_DIM ** -0.5
    wq = w_in[:, offs[0]:offs[1]][:, _pair_perm()] * scale
    wkc = w_in[:, offs[1]:offs[2]]
    wks = w_in[:, offs[2]:offs[3]]
    wkw = w_in[:, offs[3]:offs[4]]
    wg = w_in[:, offs[4]:offs[5]]
    wg = jnp.pad(wg, ((0, 0), (0, LANES - wg.shape[1])))
    half = KV_WIDTH // 2
    w = jnp.concatenate([wq, _rot_cols(wq), wkc, wks, _rot_cols(wks[:, :half]), wkw, _rot_cols(wkw[:, :half]), wg],
                        axis=1)
    return w.astype(BF16)


def _rope_tables(pos):
    half = HEAD_DIM // 2
    inv = ROPE_THETA ** (-jnp.arange(half, dtype=F32) / half)
    ang = pos.astype(F32)[:, None] * inv[None, :]
    cos = jnp.tile(jnp.cos(ang), (1, 2 * NSA_HEADS))
    sin = jnp.tile(jnp.sin(ang), (1, 2 * NSA_HEADS))
    return cos, sin


def _attn_proj(x2, g, w_att, cos, sin):
    n = x2.shape[0]
    tm = _tile(n, TM_PROJ)
    n_tab = cos.shape[0] // tm
    row = lambda i: (i, 0)
    tab = lambda i: (i % n_tab, 0)
    const = lambda i: (0, 0)
    outs = [
        jax.ShapeDtypeStruct((n, NSA_WIDTH), BF16), jax.ShapeDtypeStruct((n, NSA_WIDTH), BF16),
        jax.ShapeDtypeStruct((n, KV_WIDTH), F32), jax.ShapeDtypeStruct((n, KV_WIDTH), F32),
        jax.ShapeDtypeStruct((n, KV_WIDTH), F32), jax.ShapeDtypeStruct((n, KV_WIDTH), BF16),
        jax.ShapeDtypeStruct((n, KV_WIDTH), BF16), jax.ShapeDtypeStruct((n, LANES), F32),
    ]
    return pl.pallas_call(
        _attn_proj_kernel,
        grid=(n // tm,),
        in_specs=[pl.BlockSpec((tm, D_MODEL), row), pl.BlockSpec((1, D_MODEL), const),
                  pl.BlockSpec((D_MODEL, _A_END), const),
                  pl.BlockSpec((tm, NSA_WIDTH), tab), pl.BlockSpec((tm, NSA_WIDTH), tab)],
        out_specs=[pl.BlockSpec((tm, o.shape[1]), row) for o in outs],
        out_shape=outs,
        compiler_params=_cparams(("parallel",)),
    )(x2, g, w_att, cos, sin)


def _norm_proj_kernel(x_ref, g_ref, w_ref, o_ref):
    h = _rms(x_ref[...], g_ref[...]).astype(BF16)
    o_ref[...] = _dot(h, w_ref[...])


def _norm_proj(x2, g, w):
    n = x2.shape[0]
    tm = _tile(n, TM_PROJ)
    width = w.shape[1]
    return pl.pallas_call(
        _norm_proj_kernel,
        grid=(n // tm,),
        in_specs=[pl.BlockSpec((tm, D_MODEL), lambda i: (i, 0)), pl.BlockSpec((1, D_MODEL), lambda i: (0, 0)),
                  pl.BlockSpec((D_MODEL, width), lambda i: (0, 0))],
        out_specs=pl.BlockSpec((tm, width), lambda i: (i, 0)),
        out_shape=jax.ShapeDtypeStruct((n, width), F32),
        compiler_params=_cparams(("parallel",)),
    )(x2, g, w)


def _compress_weights(cmp_pe, cmp_w1, cmp_b1, cmp_w2, cmp_b2):
    w1r = cmp_w1.reshape(2, 2, CMP_STRIDE, HEAD_DIM, CMP_HIDDEN)
    eye = jnp.eye(NSA_KV_HEADS, dtype=F32)
    w1bd = jnp.einsum('casdh,kj->cskdajh', w1r, eye).reshape(2, CMP_STRIDE * NSA_KV_HEADS * HEAD_DIM,
                                                             2 * NSA_KV_HEADS * CMP_HIDDEN)
    w2bd = jnp.einsum('chd,kj->ckhjd', cmp_w2, eye).reshape(2, NSA_KV_HEADS * CMP_HIDDEN, NSA_KV_HEADS * HEAD_DIM)
    pe8 = jnp.broadcast_to(cmp_pe.reshape(2, 1, CMP_BLOCK * HEAD_DIM), (2, 8, CMP_BLOCK * HEAD_DIM))
    b1t = jnp.tile(cmp_b1[:, None, :], (1, 1, NSA_KV_HEADS))
    b2t = jnp.tile(cmp_b2[:, None, :], (1, 1, NSA_KV_HEADS))
    return w1bd.astype(BF16), w2bd.astype(BF16), pe8, cmp_w1, b1t, b2t


def _compress_rows(rows_refs, n_ch, a_ref, w1bd_ref, w2bd_ref, pe8_ref, w1_ref, b1_ref, b2_ref, out_ref):
    for c in range(2):
        for s in range(CMP_STRIDE):
            a_ref[:, s * LANES:(s + 1) * LANES] = rows_refs[c][pl.ds(s, n_ch, stride=CMP_STRIDE), :].astype(BF16)
        hcat = _dot(a_ref[...], w1bd_ref[c])
        c0 = _dot(pe8_ref[c], w1_ref[c], precision=HIGHEST)[0:1, :]
        c0 = jnp.concatenate([c0, c0], axis=1) + b1_ref[c]
        h_b = pltpu.roll(hcat[:, 2 * LANES:], n_ch - 1, 0)
        hid = jax.nn.gelu(hcat[:, :2 * LANES] + h_b + c0)
        out_ref[:, c * LANES:(c + 1) * LANES] = _dot(hid.astype(BF16), w2bd_ref[c]) + b2_ref[c]


def _compress_prompt_kernel(k_ref, v_ref, w1bd_ref, w2bd_ref, pe8_ref, w1_ref, b1_ref, b2_ref, out_ref, a_ref, *, n_ch):
    _compress_rows((k_ref, v_ref), n_ch, a_ref, w1bd_ref, w2bd_ref, pe8_ref, w1_ref, b1_ref, b2_ref, out_ref)


def _cw_specs():
    z3 = lambda *a: (0, 0, 0)
    return [pl.BlockSpec((2, CMP_STRIDE * LANES, 4 * LANES), z3), pl.BlockSpec((2, 2 * LANES, LANES), z3),
            pl.BlockSpec((2, 8, CMP_BLOCK * HEAD_DIM), z3), pl.BlockSpec((2, CMP_BLOCK * HEAD_DIM, CMP_HIDDEN), z3),
            pl.BlockSpec((2, 1, 2 * LANES), z3), pl.BlockSpec((2, 1, LANES), z3)]


def _compress_prompt(kvc, b_sz, s_len, cw):
    n_ch = s_len // CMP_STRIDE
    return pl.pallas_call(
        functools.partial(_compress_prompt_kernel, n_ch=n_ch),
        grid=(b_sz,),
        in_specs=[pl.BlockSpec((s_len, LANES), lambda b: (b, 0)), pl.BlockSpec((s_len, LANES), lambda b: (b, 1))]
        + _cw_specs(),
        out_specs=pl.BlockSpec((n_ch, KV_WIDTH), lambda b: (b, 0)),
        out_shape=jax.ShapeDtypeStruct((b_sz * n_ch, KV_WIDTH), F32),
        scratch_shapes=[pltpu.VMEM((n_ch, CMP_STRIDE * LANES), BF16)],
        compiler_params=_cparams(("parallel",)),
    )(kvc, kvc, *cw)


def _cover_t(n_cmp_pad, n_cmp, n_sel, n_sel_pad):
    i = np.arange(n_cmp_pad)[None, :]
    j = np.arange(n_sel_pad)[:, None]
    start = i * CMP_STRIDE
    m = (start < (j + 1) * SEL_BLOCK) & (start + CMP_BLOCK > j * SEL_BLOCK) & (i < n_cmp) & (j < n_sel)
    return jnp.asarray(m.astype(np.float32))


def _select_t(imp_t, tp, n_sel):
    j = lax.broadcasted_iota(jnp.int32, imp_t.shape, 0)
    cur = tp // SEL_BLOCK
    valid = j * SEL_BLOCK <= tp
    forced = (j == 0) | (j == cur) | (j == cur - 1)
    score = jnp.where(valid, jnp.where(forced, BIG, imp_t), NEG)
    cnt = jnp.zeros(imp_t.shape, F32)
    for jp in range(n_sel):
        row = score[jp:jp + 1, :]
        cnt = cnt + jnp.where(j > jp, jnp.where(row >= score, 1.0, 0.0), jnp.where(row > score, 1.0, 0.0))
    return jnp.where((cnt < min(N_SEL, n_sel)) & valid, 1.0, 0.0)


def _online_step(s, ok, v, m, l, acc):
    m_new = jnp.maximum(m, jnp.max(jnp.where(ok, s, NEG), axis=-1, keepdims=True))
    e = jnp.where(ok, jnp.exp(s - m_new), 0.0)
    alpha = jnp.exp(m - m_new)
    l = alpha * l + jnp.sum(e, axis=-1, keepdims=True)
    acc = alpha * acc + _dot(e.astype(BF16), v)
    return m_new, l, acc


def _stack_heads(q, keep):
    parts = []
    for g in range(NSA_GROUP):
        qg = q[:, g * LANES:(g + 1) * LANES]
        parts += [qg, qg]
    stacked = jnp.concatenate(parts, axis=0)
    return jnp.where(keep, stacked, jnp.zeros_like(stacked))


def _nsa_prompt_kernel(qu_ref, qr_ref, ckv_ref, ks_ref, vs_ref, kw_ref, vw_ref, gate_ref, covt_ref, e_ref, o_ref,
                       *, n_sel):
    qi = pl.program_id(1)
    t0 = qi * TQ
    r = 2 * NSA_GROUP * TQ
    row = lax.broadcasted_iota(jnp.int32, (r, LANES), 0)
    lane = lax.broadcasted_iota(jnp.int32, (r, LANES), 1)
    keep = (((row // TQ) % 2) == 0) == (lane < HEAD_DIM)
    tpos = t0 + (lax.broadcasted_iota(jnp.int32, (r, 1), 0) % TQ)
    qu = _stack_heads(qu_ref[...], keep)
    qr = _stack_heads(qr_ref[...], keep)

    ckv = ckv_ref[...]
    n_cmp_pad = ckv.shape[0]
    s = _dot_nt(qu, ckv[:, :LANES].astype(BF16))
    c_end = lax.broadcasted_iota(jnp.int32, (1, n_cmp_pad), 1) * CMP_STRIDE + (CMP_BLOCK - 1)
    p = _masked_softmax(s, c_end <= tpos)
    o_cmp = _dot(p.astype(BF16), ckv[:, LANES:].astype(BF16))

    psum = []
    for kvh in range(NSA_KV_HEADS):
        acc = None
        for g in range(NSA_GROUP):
            blk = g * NSA_KV_HEADS + kvh
            pg = p[blk * TQ:(blk + 1) * TQ]
            acc = pg if acc is None else acc + pg
        psum.append(acc)
    psum = jnp.concatenate(psum, axis=0)
    imp_t = _dot_nt(covt_ref[...], psum, precision=HIGHEST)
    tp = t0 + (lax.broadcasted_iota(jnp.int32, (n_sel, NSA_KV_HEADS * TQ), 1) % TQ)
    sel_t = _select_t(imp_t[:n_sel], tp, n_sel)
    sel_t = jnp.concatenate([sel_t, jnp.zeros((LANES - n_sel, NSA_KV_HEADS * TQ), F32)], axis=0)
    sel = sel_t.T.astype(BF16)
    sel = jnp.concatenate([sel] * NSA_GROUP, axis=0)

    init = (jnp.full((r, 1), NEG, F32), jnp.zeros((r, 1), F32), jnp.zeros((r, LANES), F32))

    def slc_body(kt, carry):
        k0 = pl.multiple_of(kt * TK_SLC, TK_SLC)
        s = _dot_nt(qr, ks_ref[pl.ds(k0, TK_SLC), :])
        hit = _dot(sel, e_ref[:, pl.ds(k0, TK_SLC)])
        kpos = k0 + lax.broadcasted_iota(jnp.int32, (1, TK_SLC), 1)
        ok = (hit > 0.5) & (kpos <= tpos)
        return _online_step(s, ok, vs_ref[pl.ds(k0, TK_SLC), :], *carry)

    m, l, acc = lax.fori_loop(0, (t0 + TQ + TK_SLC - 1) // TK_SLC, slc_body, init)
    o_slc = acc / jnp.maximum(l, 1e-30)

    def win_body(kt, carry):
        k0 = pl.multiple_of(kt * TQ, TQ)
        s = _dot_nt(qr, kw_ref[pl.ds(k0, TQ), :])
        kpos = k0 + lax.broadcasted_iota(jnp.int32, (1, TQ), 1)
        ok = (kpos <= tpos) & (kpos > tpos - WINDOW)
        return _online_step(s, ok, vw_ref[pl.ds(k0, TQ), :], *carry)

    m, l, acc = lax.fori_loop(jnp.maximum(qi - WINDOW // TQ, 0), qi + 1, win_body, init)
    o_win = acc / jnp.maximum(l, 1e-30)

    gt = gate_ref[...]
    lane_t = lax.broadcasted_iota(jnp.int32, (TQ, LANES), 1)
    for g in range(NSA_GROUP):
        out = jnp.zeros((TQ, LANES), F32)
        for kvh in range(NSA_KV_HEADS):
            h = kvh * NSA_GROUP + g
            blk = g * NSA_KV_HEADS + kvh
            sl = slice(blk * TQ, (blk + 1) * TQ)
            o = (gt[:, 3 * h:3 * h + 1] * o_cmp[sl] + gt[:, 3 * h + 1:3 * h + 2] * o_slc[sl]
                 + gt[:, 3 * h + 2:3 * h + 3] * o_win[sl])
            out = out + jnp.where((lane_t < HEAD_DIM) == (kvh == 0), o, 0.0)
        o_ref[:, g * LANES:(g + 1) * LANES] = out.astype(BF16)


def _block_expand(n_sel_pad, n_keys):
    j = np.arange(n_sel_pad)[:, None]
    k = np.arange(n_keys)[None, :]
    return jnp.asarray((k // SEL_BLOCK == j).astype(np.float32)).astype(BF16)


def _nsa_prompt(qu, qr, ckv, kvs_b, kvw_b, gates, b_sz, s_len):
    n_cmp_pad = s_len // CMP_STRIDE
    n_sel = s_len // SEL_BLOCK
    assert n_sel <= LANES and n_cmp_pad % LANES == 0
    covt = _cover_t(n_cmp_pad, n_cmp_pad - 1, n_sel, LANES)
    e_mat = _block_expand(LANES, s_len)
    nq = s_len // TQ
    tile = lambda b, q: (b * nq + q, 0)
    const = lambda b, q: (0, 0)
    return pl.pallas_call(
        functools.partial(_nsa_prompt_kernel, n_sel=n_sel),
        grid=(b_sz, nq),
        in_specs=[pl.BlockSpec((TQ, NSA_WIDTH), tile), pl.BlockSpec((TQ, NSA_WIDTH), tile),
                  pl.BlockSpec((n_cmp_pad, KV_WIDTH), lambda b, q: (b, 0)),
                  pl.BlockSpec((s_len, LANES), lambda b, q: (b, 0)), pl.BlockSpec((s_len, LANES), lambda b, q: (b, 1)),
                  pl.BlockSpec((s_len, LANES), lambda b, q: (b, 0)), pl.BlockSpec((s_len, LANES), lambda b, q: (b, 1)),
                  pl.BlockSpec((TQ, LANES), tile),
                  pl.BlockSpec((LANES, n_cmp_pad), const), pl.BlockSpec((LANES, s_len), const)],
        out_specs=pl.BlockSpec((TQ, NSA_WIDTH), tile),
        out_shape=jax.ShapeDtypeStruct((b_sz * s_len, NSA_WIDTH), BF16),
        compiler_params=_cparams(("parallel", "arbitrary")),
    )(qu, qr, ckv, kvs_b, kvs_b, kvw_b, kvw_b, gates, covt, e_mat)


def _hgrn_consts(c):
    n_lvl = int(math.log2(c))
    mall = np.zeros(((1 + n_lvl) * c, c), np.float32)
    masks = np.zeros((n_lvl, c, c), np.float32)
    r = np.arange(c)
    mall[:c] = (r[None, :] <= r[:, None])
    for li in range(n_lvl):
        n = c >> (li + 1)
        up_start = (r // (2 * n)) * 2 * n + n
        upper = r >= up_start
        u = r[None, :]
        m_up = upper[:, None] & (u >= up_start[:, None]) & (u <= r[:, None])
        m_lo = (~upper)[:, None] & (u > r[:, None]) & (u < up_start[:, None])
        mall[(1 + li) * c:(2 + li) * c] = m_up | m_lo
        masks[li] = upper[:, None] & (~upper)[None, :] & ((r // (2 * n))[:, None] == (r // (2 * n))[None, :])
    return jnp.asarray(mall).astype(BF16), jnp.asarray(masks)


def _split3(x):
    a = x.astype(BF16)
    r1 = x - a.astype(F32)
    b = r1.astype(BF16)
    c = (r1 - b.astype(F32)).astype(BF16)
    return a, b, c


def _lower_bound(lbl):
    e = jnp.exp(lbl - jnp.max(lbl, axis=0, keepdims=True))
    return e[0:1] / jnp.sum(e, axis=0, keepdims=True)


def _hgrn_prompt_kernel(hq_ref, hf_ref, hi_ref, hg_ref, lbl_ref, gn_ref, mall_ref, mask_ref, o_ref, s_ref, st_ref,
                        *, c, n_lvl, ts):
    si = pl.program_id(2)

    @pl.when(si == 0)
    def _():
        st_ref[...] = jnp.zeros_like(st_ref)

    lb = _lower_bound(lbl_ref[...])
    gn = gn_ref[...]
    rowi = lax.broadcasted_iota(jnp.int32, (c, 1), 0)

    def chunk(ci, carry):
        r0 = pl.multiple_of(ci * c, c)
        q = jax.nn.silu(hq_ref[pl.ds(r0, c), :]) * HGRN_DK ** -0.5
        f = lb + (1.0 - lb) * jax.nn.sigmoid(hf_ref[pl.ds(r0, c), :])
        k = 1.0 - f
        v = hi_ref[pl.ds(r0, c), :]
        a, b, d = _split3(jnp.log(f))
        sums = _dot(mall_ref[...], jnp.concatenate([a, b, d], axis=1))
        sums = (sums[:, :LANES] + sums[:, LANES:2 * LANES]) + sums[:, 2 * LANES:]
        cum = sums[:c]
        att = jnp.zeros((c, c), F32)
        for li in range(n_lvl):
            n = c >> (li + 1)
            e = jnp.exp(sums[(1 + li) * c:(2 + li) * c])
            zz = (jnp.where((rowi & n) != 0, q, k) * e).astype(BF16)
            att = att + _dot_nt(zz, zz) * mask_ref[li]
        vb = v.astype(BF16)
        diag = jnp.sum(q * k, axis=-1, keepdims=True)
        st = st_ref[...]
        o = _dot(att.astype(BF16), vb) + diag * v + _dot_nt((q * jnp.exp(cum)).astype(BF16), st.astype(BF16))
        last = cum[c - 1:c, :]
        kd = (k * jnp.exp(last - cum)).astype(BF16)
        st_ref[...] = st * jnp.exp(last) + _dot(v.T.astype(BF16), kd)
        y = o * lax.rsqrt(jnp.mean(o * o, axis=-1, keepdims=True) + NORM_EPS) * gn
        o_ref[pl.ds(r0, c), :] = (y * jax.nn.silu(hg_ref[pl.ds(r0, c), :])).astype(BF16)
        return carry

    lax.fori_loop(0, ts // c, chunk, 0)

    @pl.when(si == pl.num_programs(2) - 1)
    def _():
        s_ref[0, 0] = st_ref[...].T


def _hgrn_prompt(zh, lb_logits, gnorm, b_sz, s_len):
    c = HG_CHUNK
    ts = min(HG_TS, s_len)
    ns = s_len // ts
    n_lvl = int(math.log2(c))
    mall, masks = _hgrn_consts(c)
    col = lambda k: (lambda b, h, s: (b * ns + s, h + HGRN_HEADS * k))
    return pl.pallas_call(
        functools.partial(_hgrn_prompt_kernel, c=c, n_lvl=n_lvl, ts=ts),
        grid=(b_sz, HGRN_HEADS, ns),
        in_specs=[pl.BlockSpec((ts, LANES), col(0)), pl.BlockSpec((ts, LANES), col(1)),
                  pl.BlockSpec((ts, LANES), col(2)), pl.BlockSpec((ts, LANES), col(3)),
                  pl.BlockSpec((lb_logits.shape[0], LANES), lambda b, h, s: (0, h)),
                  pl.BlockSpec((1, LANES), lambda b, h, s: (0, 0)),
                  pl.BlockSpec(mall.shape, lambda b, h, s: (0, 0)),
                  pl.BlockSpec(masks.shape, lambda b, h, s: (0, 0, 0))],
        out_specs=[pl.BlockSpec((ts, LANES), lambda b, h, s: (b * ns + s, h)),
                   pl.BlockSpec((1, 1, HGRN_DK, HGRN_DV), lambda b, h, s: (b, h, 0, 0))],
        out_shape=[jax.ShapeDtypeStruct((b_sz * s_len, HGRN_VW), BF16),
                   jax.ShapeDtypeStruct((b_sz, HGRN_HEADS, HGRN_DK, HGRN_DV), F32)],
        scratch_shapes=[pltpu.VMEM((HGRN_DV, HGRN_DK), F32)],
        compiler_params=_cparams(("parallel", "parallel", "arbitrary")),
    )(zh, zh, zh, zh, lb_logits, gnorm, mall, masks)


_R_EXP0 = 8


def _post_kernel(x_ref, oa_ref, ob_ref, ma_ref, mb_ref, wpa_ref, wpb_ref, wo_ref, g2_ref, wrt_ref, br_ref,
                 x1_ref, h2_ref, lg_ref):
    pa = _dot(oa_ref[...], wpa_ref[...])
    pb = _dot(ob_ref[...], wpb_ref[...])
    merged = jax.nn.sigmoid(ma_ref[...]) * pa + jax.nn.sigmoid(mb_ref[...]) * pb
    x1 = x_ref[...] + _dot(merged.astype(BF16), wo_ref[...])
    h2 = _rms(x1, g2_ref[...])
    x1_ref[...] = x1
    h2_ref[...] = h2
    lg_ref[...] = _dot_nt(wrt_ref[...], h2, precision=HIGHEST) + br_ref[...]


def _router_weights(w_rg, b_rg, w_re, b_re):
    wrt = jnp.zeros((LANES, D_MODEL), F32)
    wrt = wrt.at[:N_GROUPS].set(w_rg.T)
    wrt = wrt.at[_R_EXP0:_R_EXP0 + N_EXPERTS].set(jnp.transpose(w_re, (0, 2, 1)).reshape(N_EXPERTS, D_MODEL))
    br = jnp.zeros((LANES, 1), F32)
    br = br.at[:N_GROUPS, 0].set(b_rg)
    br = br.at[_R_EXP0:_R_EXP0 + N_EXPERTS, 0].set(b_re.reshape(N_EXPERTS))
    return wrt, br


def _post_mixer(x2, o_a, o_b, zm, wpa, wpb, wo, g2, wrt, br):
    n = x2.shape[0]
    tm = _tile(n, TM_POST)
    row = lambda i: (i, 0)
    const = lambda i: (0, 0)
    return pl.pallas_call(
        _post_kernel,
        grid=(n // tm,),
        in_specs=[pl.BlockSpec((tm, D_MODEL), row), pl.BlockSpec((tm, NSA_WIDTH), row),
                  pl.BlockSpec((tm, HGRN_VW), row),
                  pl.BlockSpec((tm, D_MODEL), lambda i: (i, 0)), pl.BlockSpec((tm, D_MODEL), lambda i: (i, 1)),
                  pl.BlockSpec((NSA_WIDTH, D_MODEL), const), pl.BlockSpec((HGRN_VW, D_MODEL), const),
                  pl.BlockSpec((D_MODEL, D_MODEL), const), pl.BlockSpec((1, D_MODEL), const),
                  pl.BlockSpec((LANES, D_MODEL), const), pl.BlockSpec((LANES, 1), const)],
        out_specs=[pl.BlockSpec((tm, D_MODEL), row), pl.BlockSpec((tm, D_MODEL), row),
                   pl.BlockSpec((LANES, tm), lambda i: (0, i))],
        out_shape=[jax.ShapeDtypeStruct((n, D_MODEL), F32), jax.ShapeDtypeStruct((n, D_MODEL), F32),
                   jax.ShapeDtypeStruct((LANES, n), F32)],
        compiler_params=_cparams(("parallel",)),
    )(x2, o_a, o_b, zm, zm, wpa, wpb, wo, g2, wrt, br)


def _lowest_argmax(vals, top):
    idx = jnp.full(top.shape, len(vals) - 1, jnp.int32)
    for i in range(len(vals) - 2, -1, -1):
        idx = jnp.where(vals[i] == top, i, idx)
    return idx


def _route_kernel(lg_ref, u_ref, eid_ref, gate_ref, rank_ref, cnt_ref, carry_ref):
    @pl.when(pl.program_id(0) == 0)
    def _():
        carry_ref[...] = jnp.zeros_like(carry_ref)

    lg = lg_ref[...]
    tr = lg.shape[1]
    grp = [lg[g:g + 1] for g in range(N_GROUPS)]
    mx = functools.reduce(jnp.maximum, grp)
    ex = [jnp.exp(v - mx) for v in grp]
    den = functools.reduce(lambda a, b: a + b, ex)
    pr = [e / den for e in ex]
    pg = functools.reduce(jnp.maximum, pr)
    gtop = _lowest_argmax(pr, pg)
    le = []
    for j in range(EXPERTS_PER_GROUP):
        v = lg[_R_EXP0 + j:_R_EXP0 + j + 1]
        for g in range(1, N_GROUPS):
            r = _R_EXP0 + g * EXPERTS_PER_GROUP + j
            v = jnp.where(gtop == g, lg[r:r + 1], v)
        le.append(v)
    m1 = functools.reduce(jnp.maximum, le)
    i1 = _lowest_argmax(le, m1)
    le2 = [jnp.where(i1 == j, -jnp.inf, le[j]) for j in range(EXPERTS_PER_GROUP)]
    m2 = functools.reduce(jnp.maximum, le2)
    i2 = _lowest_argmax(le2, m2)
    e2 = jnp.exp(m2 - m1)
    den2 = 1.0 + e2
    gate1 = (1.0 / den2) * pg
    gate2 = (e2 / den2) * pg
    eid1 = gtop * EXPERTS_PER_GROUP + i1
    eid2 = gtop * EXPERTS_PER_GROUP + i2

    eio = lax.broadcasted_iota(jnp.int32, (N_EXPERTS, tr), 0)
    oh1 = jnp.where(eio == eid1, 1.0, 0.0)
    oh2 = jnp.where(eio == eid2, 1.0, 0.0)
    both = (oh1 + oh2).astype(BF16)
    carry = carry_ref[...]
    before = _dot(both, u_ref[...]) + jnp.concatenate([carry] * (tr // LANES), axis=1)
    rank1 = jnp.sum(oh1 * before, axis=0, keepdims=True).astype(jnp.int32)
    rank2 = jnp.sum(oh2 * before, axis=0, keepdims=True).astype(jnp.int32)
    carry = carry + _dot(both, jnp.ones((tr, LANES), BF16))
    carry_ref[...] = carry
    cnt_ref[...] = carry
    zi = jnp.zeros((6, tr), jnp.int32)
    eid_ref[...] = jnp.concatenate([eid1, eid2, zi], axis=0)
    rank_ref[...] = jnp.concatenate([rank1, rank2, zi], axis=0)
    gate_ref[...] = jnp.concatenate([gate1, gate2, jnp.zeros((6, tr), F32)], axis=0)


def _route(lg):
    n = lg.shape[1]
    tr = _tile(n, TR_ROUTE, LANES)
    u = jnp.asarray(np.triu(np.ones((tr, tr), np.float32), 1)).astype(BF16)
    col = lambda i: (0, i)
    return pl.pallas_call(
        _route_kernel,
        grid=(n // tr,),
        in_specs=[pl.BlockSpec((LANES, tr), col), pl.BlockSpec((tr, tr), lambda i: (0, 0))],
        out_specs=[pl.BlockSpec((8, tr), col), pl.BlockSpec((8, tr), col), pl.BlockSpec((8, tr), col),
                   pl.BlockSpec((N_EXPERTS, LANES), lambda i: (0, 0))],
        out_shape=[jax.ShapeDtypeStruct((8, n), jnp.int32), jax.ShapeDtypeStruct((8, n), F32),
                   jax.ShapeDtypeStruct((8, n), jnp.int32), jax.ShapeDtypeStruct((N_EXPERTS, LANES), F32)],
        scratch_shapes=[pltpu.VMEM((N_EXPERTS, LANES), F32)],
        compiler_params=_cparams(("arbitrary",)),
    )(lg, u)


def _dispatch_kernel(eid_ref, rank_ref, ps_ref, h_ref, xb_in_ref, xb_ref, sem):
    del xb_in_ref
    t_n = h_ref.shape[0]

    def row_copy(t, k):
        d = ps_ref[eid_ref[k, t]] + rank_ref[k, t]
        return pltpu.make_async_copy(h_ref.at[pl.ds(t, 1), :], xb_ref.at[pl.ds(d, 1), :], sem)

    def issue(t, c):
        for k in range(TOP_K_EXPERTS):
            row_copy(t, k).start()
        return c

    def wait(t, c):
        for k in range(TOP_K_EXPERTS):
            row_copy(t, k).wait()
        return c

    lax.fori_loop(0, t_n, issue, 0)
    lax.fori_loop(0, t_n, wait, 0)


def _smem_cols(t):
    return pl.BlockSpec((8, t), lambda i: (0, i), memory_space=pltpu.SMEM)


def _dispatch(eid, rank, pad_start, h2, rows_total):
    n = h2.shape[0]
    t = _tile(n, T_DISP, LANES)
    xb0 = jnp.zeros((rows_total, D_MODEL), F32)
    return pl.pallas_call(
        _dispatch_kernel,
        grid=(n // t,),
        in_specs=[_smem_cols(t), _smem_cols(t), pl.BlockSpec(memory_space=pltpu.SMEM),
                  pl.BlockSpec((t, D_MODEL), lambda i: (i, 0)), pl.BlockSpec(memory_space=pl.ANY)],
        out_specs=pl.BlockSpec(memory_space=pl.ANY),
        out_shape=jax.ShapeDtypeStruct((rows_total, D_MODEL), F32),
        scratch_shapes=[pltpu.SemaphoreType.DMA(())],
        input_output_aliases={4: 0},
        compiler_params=_cparams(("arbitrary",)),
    )(eid, rank, pad_start, h2, xb0)


def _expert_kernel(blk_e_ref, x_ref, wg_ref, wu_ref, wd_ref, y_ref):
    del blk_e_ref
    x = x_ref[...].astype(BF16)
    hid = jax.nn.silu(_dot(x, wg_ref[0])) * _dot(x, wu_ref[0])
    y_ref[...] = _dot(hid.astype(BF16), wd_ref[0])


def _experts(blk_e, xb, wg, wu, wd):
    rows_total = xb.shape[0]
    wsel = lambda i, e: (e[i], 0, 0)
    grid_spec = pltpu.PrefetchScalarGridSpec(
        num_scalar_prefetch=1,
        grid=(rows_total // MOE_BM,),
        in_specs=[pl.BlockSpec((MOE_BM, D_MODEL), lambda i, e: (i, 0)),
                  pl.BlockSpec((1, D_MODEL, D_EXPERT), wsel), pl.BlockSpec((1, D_MODEL, D_EXPERT), wsel),
                  pl.BlockSpec((1, D_EXPERT, D_MODEL), wsel)],
        out_specs=pl.BlockSpec((MOE_BM, D_MODEL), lambda i, e: (i, 0)),
    )
    return pl.pallas_call(
        _expert_kernel,
        grid_spec=grid_spec,
        out_shape=jax.ShapeDtypeStruct((rows_total, D_MODEL), F32),
        compiler_params=_cparams(("arbitrary",)),
    )(blk_e, xb, wg, wu, wd)


def _combine_kernel(eid_ref, rank_ref, ps_ref, x1_ref, gate_ref, fn_ref, yb_ref, y_ref, buf_ref, sem):
    t_n = x1_ref.shape[0]

    def row_copy(t, k):
        d = ps_ref[eid_ref[k, t]] + rank_ref[k, t]
        return pltpu.make_async_copy(yb_ref.at[pl.ds(d, 1), :], buf_ref.at[k, pl.ds(t, 1), :], sem)

    def issue(t, c):
        for k in range(TOP_K_EXPERTS):
            row_copy(t, k).start()
        return c

    def wait(t, c):
        for k in range(TOP_K_EXPERTS):
            row_copy(t, k).wait()
        return c

    lax.fori_loop(0, t_n, issue, 0)
    lax.fori_loop(0, t_n, wait, 0)
    g = gate_ref[...]
    out = x1_ref[...] + g[:, 0:1] * buf_ref[0] + g[:, 1:2] * buf_ref[1]
    y_ref[...] = _rms(out, fn_ref[...])


def _combine(eid, rank, pad_start, x1, gate_t, fnorm, yb):
    n = x1.shape[0]
    t = _tile(n, T_DISP, LANES)
    return pl.pallas_call(
        _combine_kernel,
        grid=(n // t,),
        in_specs=[_smem_cols(t), _smem_cols(t), pl.BlockSpec(memory_space=pltpu.SMEM),
                  pl.BlockSpec((t, D_MODEL), lambda i: (i, 0)), pl.BlockSpec((t, 8), lambda i: (i, 0)),
                  pl.BlockSpec((1, D_MODEL), lambda i: (0, 0)), pl.BlockSpec(memory_space=pl.ANY)],
        out_specs=pl.BlockSpec((t, D_MODEL), lambda i: (i, 0)),
        out_shape=jax.ShapeDtypeStruct((n, D_MODEL), F32),
        scratch_shapes=[pltpu.VMEM((TOP_K_EXPERTS, t, D_MODEL), F32), pltpu.SemaphoreType.DMA(())],
        compiler_params=_cparams(("arbitrary",)),
    )(eid, rank, pad_start, x1, gate_t, fnorm, yb)


def _moe_and_final(x1, h2, lg, wg, wu, wd, fnorm):
    n = x1.shape[0]
    eid, gate, rank, cnt = _route(lg)
    counts = cnt[:, 0].astype(jnp.int32)
    padded = (counts + MOE_BM - 1) // MOE_BM * MOE_BM
    pad_end = jnp.cumsum(padded)
    pad_start = (pad_end - padded).astype(jnp.int32)
    n_blocks = -(-(n * TOP_K_EXPERTS + N_EXPERTS * (MOE_BM - 1)) // MOE_BM)
    blk_e = jnp.minimum(jnp.searchsorted(pad_end, jnp.arange(n_blocks, dtype=jnp.int32) * MOE_BM, side='right'),
                        N_EXPERTS - 1).astype(jnp.int32)
    xb = _dispatch(eid, rank, pad_start, h2, n_blocks * MOE_BM)
    yb = _experts(blk_e, xb, wg, wu, wd)
    return _combine(eid, rank, pad_start, x1, gate.T, fnorm, yb)


def _gather_pages(pt_ref, b, pool_ref, buf_ref, sem, n_pages):
    def page_copy(p):
        return pltpu.make_async_copy(pool_ref.at[pt_ref[b, p]], buf_ref.at[pl.ds(p * PAGE_SIZE, PAGE_SIZE), :], sem)

    def issue(p, c):
        page_copy(p).start()
        return c

    def wait(p, c):
        page_copy(p).wait()
        return c

    lax.fori_loop(0, n_pages, issue, 0)
    return lambda: lax.fori_loop(0, n_pages, wait, 0)


def _compress_sample_kernel(pt_ref, pool_ref, w1bd_ref, w2bd_ref, pe8_ref, w1_ref, b1_ref, b2_ref, out_ref,
                            kbuf_ref, vbuf_ref, a_ref, sem, *, n_pages, n_ch):
    b = pl.program_id(0)

    def half_copy(p, c, buf_ref):
        return pltpu.make_async_copy(pool_ref.at[pt_ref[b, p], :, pl.ds(c * LANES, LANES)],
                                     buf_ref.at[pl.ds(p * PAGE_SIZE, PAGE_SIZE), :], sem)

    def issue(p, carry):
        half_copy(p, 0, kbuf_ref).start()
        half_copy(p, 1, vbuf_ref).start()
        return carry

    def wait(p, carry):
        half_copy(p, 0, kbuf_ref).wait()
        half_copy(p, 1, vbuf_ref).wait()
        return carry

    lax.fori_loop(0, n_pages, issue, 0)
    lax.fori_loop(0, n_pages, wait, 0)
    _compress_rows((kbuf_ref, vbuf_ref), n_ch, a_ref, w1bd_ref, w2bd_ref, pe8_ref, w1_ref, b1_ref, b2_ref, out_ref)


def _compress_sample(page_table, pool, cw):
    b_sz, n_pages = page_table.shape
    past = n_pages * PAGE_SIZE
    n_ch = past // CMP_STRIDE
    z3 = lambda b, pt: (0, 0, 0)
    specs = [pl.BlockSpec((2, CMP_STRIDE * LANES, 4 * LANES), z3), pl.BlockSpec((2, 2 * LANES, LANES), z3),
             pl.BlockSpec((2, 8, CMP_BLOCK * HEAD_DIM), z3), pl.BlockSpec((2, CMP_BLOCK * HEAD_DIM, CMP_HIDDEN), z3),
             pl.BlockSpec((2, 1, 2 * LANES), z3), pl.BlockSpec((2, 1, LANES), z3)]
    grid_spec = pltpu.PrefetchScalarGridSpec(
        num_scalar_prefetch=1,
        grid=(b_sz,),
        in_specs=[pl.BlockSpec(memory_space=pl.ANY)] + specs,
        out_specs=pl.BlockSpec((n_ch, KV_WIDTH), lambda b, pt: (b, 0)),
        scratch_shapes=[pltpu.VMEM((past, LANES), F32), pltpu.VMEM((past, LANES), F32),
                        pltpu.VMEM((n_ch, CMP_STRIDE * LANES), BF16), pltpu.SemaphoreType.DMA(())],
    )
    return pl.pallas_call(
        functools.partial(_compress_sample_kernel, n_pages=n_pages, n_ch=n_ch),
        grid_spec=grid_spec,
        out_shape=jax.ShapeDtypeStruct((b_sz * n_ch, KV_WIDTH), F32),
        compiler_params=_cparams(("arbitrary",)),
    )(page_table, pool, *cw)


def _cmp_sample_kernel(qu_ref, ckv_ref, covt_ref, ocmp_ref, selt_ref, *, past, t_len, n_sel):
    r = qu_ref.shape[1]
    qu = qu_ref[0]
    ckv = ckv_ref[...]
    n_cmp_pad = ckv.shape[0]
    tpos = past + (lax.broadcasted_iota(jnp.int32, (r, 1), 0) % t_len)
    s = _dot_nt(qu, ckv[:, :LANES].astype(BF16))
    c_end = lax.broadcasted_iota(jnp.int32, (1, n_cmp_pad), 1) * CMP_STRIDE + (CMP_BLOCK - 1)
    p = _masked_softmax(s, c_end <= tpos)
    ocmp_ref[0] = _dot(p.astype(BF16), ckv[:, LANES:].astype(BF16))
    psum = []
    for kvh in range(NSA_KV_HEADS):
        acc = None
        for g in range(NSA_GROUP):
            blk = g * NSA_KV_HEADS + kvh
            pg = p[blk * t_len:(blk + 1) * t_len]
            acc = pg if acc is None else acc + pg
        psum.append(acc)
    psum = jnp.concatenate(psum, axis=0)
    imp_t = _dot_nt(covt_ref[...], psum, precision=HIGHEST)
    tp = past + (lax.broadcasted_iota(jnp.int32, imp_t.shape, 1) % t_len)
    selt_ref[0] = _select_t(imp_t, tp, n_sel)


def _cmp_sample(qu32, ckv, b_sz, past, t_len, n_sel, n_sel_pad):
    n_cmp_pad = past // CMP_STRIDE
    covt = _cover_t(n_cmp_pad, n_cmp_pad - 1, n_sel, n_sel_pad)
    r = qu32.shape[1]
    return pl.pallas_call(
        functools.partial(_cmp_sample_kernel, past=past, t_len=t_len, n_sel=n_sel),
        grid=(b_sz,),
        in_specs=[pl.BlockSpec((1, r, LANES), lambda b: (b, 0, 0)),
                  pl.BlockSpec((n_cmp_pad, KV_WIDTH), lambda b: (b, 0)),
                  pl.BlockSpec((n_sel_pad, n_cmp_pad), lambda b: (0, 0))],
        out_specs=[pl.BlockSpec((1, r, LANES), lambda b: (b, 0, 0)),
                   pl.BlockSpec((1, n_sel_pad, NSA_KV_HEADS * t_len), lambda b: (b, 0, 0))],
        out_shape=[jax.ShapeDtypeStruct((b_sz, r, LANES), F32),
                   jax.ShapeDtypeStruct((b_sz, n_sel_pad, NSA_KV_HEADS * t_len), F32)],
        compiler_params=_cparams(("parallel",)),
    )(qu32, ckv, covt)


TK_SAMPLE = 1024


def _slc_win_sample_kernel(pt_ref, qr_ref, sel_ref, e_ref, pool_ref, new_s_ref, win_ref, new_w_ref, ocmp_ref, gate_ref,
                           o_ref, buf_ref, sem, *, n_pages, past, t_len, win_buf):
    wait_pages = _gather_pages(pt_ref, pl.program_id(0), pool_ref, buf_ref, sem, n_pages)
    qr = qr_ref[0]
    sel = sel_ref[0]
    r = qr.shape[0]
    tpos = past + (lax.broadcasted_iota(jnp.int32, (r, 1), 0) % t_len)
    init = (jnp.full((r, 1), NEG, F32), jnp.zeros((r, 1), F32), jnp.zeros((r, LANES), F32))
    new_i = lax.broadcasted_iota(jnp.int32, (1, LANES), 1)
    new_ok = ((past + new_i) <= tpos) & (new_i < t_len)

    carry = init
    for k0 in range(0, win_buf, LANES):
        rows = win_ref[k0:k0 + LANES, :]
        wpos = past - win_buf + k0 + lax.broadcasted_iota(jnp.int32, (1, LANES), 1)
        ok = (wpos >= 0) & (wpos <= tpos) & (wpos > tpos - WINDOW)
        carry = _online_step(_dot_nt(qr, rows[:, :LANES].astype(BF16)), ok, rows[:, LANES:].astype(BF16), *carry)
    rows = new_w_ref[0]
    ok = new_ok & ((past + new_i) > tpos - WINDOW)
    m, l, acc = _online_step(_dot_nt(qr, rows[:, :LANES].astype(BF16)), ok, rows[:, LANES:].astype(BF16), *carry)
    o_win = acc / jnp.maximum(l, 1e-30)

    wait_pages()

    def slc_body(kt, carry):
        k0 = pl.multiple_of(kt * TK_SAMPLE, TK_SAMPLE)
        rows = buf_ref[pl.ds(k0, TK_SAMPLE), :]
        hit = _dot(sel, e_ref[:, pl.ds(k0, TK_SAMPLE)])
        return _online_step(_dot_nt(qr, rows[:, :LANES].astype(BF16)), hit > 0.5, rows[:, LANES:].astype(BF16),
                            *carry)

    carry = lax.fori_loop(0, past // TK_SAMPLE, slc_body, init)
    rows = new_s_ref[0]
    n_past_blk = past // SEL_BLOCK
    ok = new_ok & (sel[:, n_past_blk:n_past_blk + 1] > 0.5)
    m, l, acc = _online_step(_dot_nt(qr, rows[:, :LANES].astype(BF16)), ok, rows[:, LANES:].astype(BF16), *carry)
    o_slc = acc / jnp.maximum(l, 1e-30)

    g = gate_ref[0]
    o = g[:, 0:1] * ocmp_ref[0] + g[:, 1:2] * o_slc + g[:, 2:3] * o_win
    row = lax.broadcasted_iota(jnp.int32, (r, LANES), 0)
    lane = lax.broadcasted_iota(jnp.int32, (r, LANES), 1)
    o_ref[0] = jnp.where((((row // t_len) % 2) == 0) == (lane < HEAD_DIM), o, 0.0)


def _slc_win_sample(page_table, qr32, sel32, pool, new_s, cache_win, new_w, ocmp32, gate32, past, t_len, n_sel_pad):
    b_sz, n_pages = page_table.shape
    win_buf = cache_win.shape[0] // b_sz
    r = qr32.shape[1]
    e_mat = _block_expand(n_sel_pad, past)
    b3 = lambda b, pt: (b, 0, 0)
    grid_spec = pltpu.PrefetchScalarGridSpec(
        num_scalar_prefetch=1,
        grid=(b_sz,),
        in_specs=[pl.BlockSpec((1, r, LANES), b3), pl.BlockSpec((1, r, n_sel_pad), b3),
                  pl.BlockSpec((n_sel_pad, past), lambda b, pt: (0, 0)),
                  pl.BlockSpec(memory_space=pl.ANY),
                  pl.BlockSpec((1, LANES, KV_WIDTH), b3),
                  pl.BlockSpec((win_buf, KV_WIDTH), lambda b, pt: (b, 0)),
                  pl.BlockSpec((1, LANES, KV_WIDTH), b3),
                  pl.BlockSpec((1, r, LANES), b3), pl.BlockSpec((1, r, 3), b3)],
        out_specs=pl.BlockSpec((1, r, LANES), b3),
        scratch_shapes=[pltpu.VMEM((past, KV_WIDTH), F32), pltpu.SemaphoreType.DMA(())],
    )
    return pl.pallas_call(
        functools.partial(_slc_win_sample_kernel, n_pages=n_pages, past=past, t_len=t_len, win_buf=win_buf),
        grid_spec=grid_spec,
        out_shape=jax.ShapeDtypeStruct((b_sz, r, LANES), F32),
        compiler_params=_cparams(("arbitrary",)),
    )(page_table, qr32, sel32, e_mat, pool, new_s, cache_win, new_w, ocmp32, gate32)


def _rows32(q, b_sz, t_len):
    q5 = q.reshape(b_sz, t_len, NSA_GROUP, NSA_KV_HEADS, HEAD_DIM).transpose(0, 2, 3, 1, 4)
    eye = jnp.eye(NSA_KV_HEADS, dtype=q.dtype)
    return jnp.einsum('bgktd,kj->bgktjd', q5, eye).reshape(b_sz, NSA_HEADS * t_len, LANES)


def _from_rows32(o32, b_sz, t_len):
    o6 = o32.reshape(b_sz, NSA_GROUP, NSA_KV_HEADS, t_len, NSA_KV_HEADS, HEAD_DIM)
    o5 = jnp.stack([o6[:, :, k, :, k] for k in range(NSA_KV_HEADS)], axis=2)
    return o5.transpose(0, 3, 1, 2, 4).reshape(b_sz * t_len, NSA_WIDTH)


def _nsa_sample(qu, qr, kvs_new, kvw_new, gates, cache_cmp, cache_slc, cache_win, page_table, cw, t_len):
    b_sz, n_pages = page_table.shape
    past = n_pages * PAGE_SIZE
    total = past + t_len
    assert t_len < CMP_STRIDE and t_len <= SEL_BLOCK and past % TK_SAMPLE == 0
    n_sel = -(-total // SEL_BLOCK)
    n_sel_pad = -(-n_sel // 8) * 8
    pool_c = cache_cmp.reshape(-1, PAGE_SIZE, KV_WIDTH)
    pool_s = cache_slc.reshape(-1, PAGE_SIZE, KV_WIDTH)
    ckv = _compress_sample(page_table, pool_c, cw)
    ocmp32, sel_t = _cmp_sample(_rows32(qu, b_sz, t_len), ckv, b_sz, past, t_len, n_sel, n_sel_pad)
    sel = sel_t.transpose(0, 2, 1).reshape(b_sz, 1, NSA_KV_HEADS * t_len, n_sel_pad)
    sel32 = jnp.broadcast_to(sel, (b_sz, NSA_GROUP, NSA_KV_HEADS * t_len, n_sel_pad))
    sel32 = sel32.reshape(b_sz, NSA_HEADS * t_len, n_sel_pad).astype(BF16)
    pad_rows = lambda a: jnp.pad(a.reshape(b_sz, t_len, KV_WIDTH), ((0, 0), (0, LANES - t_len), (0, 0)))
    g4 = gates[:, :3 * NSA_HEADS].reshape(b_sz, t_len, NSA_KV_HEADS, NSA_GROUP, 3)
    gate32 = g4.transpose(0, 3, 2, 1, 4).reshape(b_sz, NSA_HEADS * t_len, 3)
    o32 = _slc_win_sample(page_table, _rows32(qr, b_sz, t_len), sel32, pool_s, pad_rows(kvs_new),
                          cache_win.reshape(b_sz * cache_win.shape[1], KV_WIDTH), pad_rows(kvw_new), ocmp32, gate32,
                          past, t_len, n_sel_pad)
    return _from_rows32(o32, b_sz, t_len).astype(BF16)


def _hgrn_sample_kernel(hq_ref, hf_ref, vt_ref, hgt_ref, s0_ref, lbl_ref, gnt_ref, ot_ref, s_ref, *, t_len):
    lb = _lower_bound(lbl_ref[...])
    q = jax.nn.silu(hq_ref[0]) * HGRN_DK ** -0.5
    f = lb + (1.0 - lb) * jax.nn.sigmoid(hf_ref[0])
    k = 1.0 - f
    vt = vt_ref[0, 0]
    st = s0_ref[0, 0].T
    cols = []
    for t in range(t_len):
        st = st * f[t:t + 1] + vt[:, t:t + 1] * k[t:t + 1]
        cols.append(jnp.sum(st * q[t:t + 1], axis=1, keepdims=True))
    ot = jnp.concatenate(cols, axis=1)
    y = ot * lax.rsqrt(jnp.mean(ot * ot, axis=0, keepdims=True) + NORM_EPS) * gnt_ref[...]
    ot_ref[0, 0] = y * jax.nn.silu(hgt_ref[0, 0])
    s_ref[0, 0] = st.T


def _hgrn_sample(zh, s0, lb_logits, gnorm, b_sz, t_len):
    z3 = zh.reshape(b_sz, t_len, 4 * HGRN_KW)
    to_t = lambda a: a.reshape(b_sz, t_len, HGRN_HEADS, HGRN_DV).transpose(0, 2, 3, 1)
    vt = to_t(z3[:, :, 2 * HGRN_KW:3 * HGRN_KW])
    hgt = to_t(z3[:, :, 3 * HGRN_KW:])
    col = lambda k: (lambda b, h: (b, 0, h + HGRN_HEADS * k))
    b4 = lambda b, h: (b, h, 0, 0)
    ot, s_new = pl.pallas_call(
        functools.partial(_hgrn_sample_kernel, t_len=t_len),
        grid=(b_sz, HGRN_HEADS),
        in_specs=[pl.BlockSpec((1, t_len, LANES), col(0)), pl.BlockSpec((1, t_len, LANES), col(1)),
                  pl.BlockSpec((1, 1, HGRN_DV, t_len), b4), pl.BlockSpec((1, 1, HGRN_DV, t_len), b4),
                  pl.BlockSpec((1, 1, HGRN_DK, HGRN_DV), b4),
                  pl.BlockSpec((lb_logits.shape[0], LANES), lambda b, h: (0, h)),
                  pl.BlockSpec((HGRN_DV, 1), lambda b, h: (0, 0))],
        out_specs=[pl.BlockSpec((1, 1, HGRN_DV, t_len), b4), pl.BlockSpec((1, 1, HGRN_DK, HGRN_DV), b4)],
        out_shape=[jax.ShapeDtypeStruct((b_sz, HGRN_HEADS, HGRN_DV, t_len), F32),
                   jax.ShapeDtypeStruct((b_sz, HGRN_HEADS, HGRN_DK, HGRN_DV), F32)],
        compiler_params=_cparams(("parallel", "parallel")),
    )(z3, z3, vt, hgt, s0, lb_logits, gnorm.reshape(HGRN_DV, 1))
    o_b = ot.transpose(0, 3, 1, 2).reshape(b_sz * t_len, HGRN_VW)
    return o_b.astype(BF16), s_new


def kernel(x_prompt, x_sample, cache_cmp_kv, cache_slc_kv, cache_win_kv, state_hgrn, page_table, norm1, w_in, cmp_pe,
           cmp_w1, cmp_b1, cmp_w2, cmp_b2, hgrn_lb_logits, hgrn_gnorm, w_proj_a, w_proj_b, w_out, norm2,
           w_router_group, b_router_group, w_router_expert, b_router_expert, w_exp_gate, w_exp_up, w_exp_down,
           final_norm):
    assert w_in.shape[0] == 1, "one layer"
    b_sz, s_len, _ = x_prompt.shape
    d_sz, t_len, _ = x_sample.shape
    past = page_table.shape[1] * PAGE_SIZE
    n_p = b_sz * s_len
    n_s = d_sz * t_len
    kv_row = (2, NSA_KV_HEADS, HEAD_DIM)

    offs = np.cumsum((0,) + IN_SPLITS)
    w_att = _attn_weight(w_in[0])
    w_h = w_in[0][:, offs[5]:offs[9]].astype(BF16)
    w_m = w_in[0][:, offs[9]:offs[11]].astype(BF16)
    g1 = norm1[0][None]
    cw = _compress_weights(cmp_pe[0], cmp_w1[0], cmp_b1[0], cmp_w2[0], cmp_b2[0])
    perm = _pair_perm()
    wpa = w_proj_a[0][perm].astype(BF16)
    wpb = w_proj_b[0].astype(BF16)
    wo = w_out[0].astype(BF16)
    wrt, br = _router_weights(w_router_group[0], b_router_group[0], w_router_expert[0], b_router_expert[0])
    gn = hgrn_gnorm[0][None]

    xp = x_prompt.reshape(n_p, D_MODEL)
    cos, sin = _rope_tables(jnp.arange(s_len))
    qu, qr, kvc_p, kvs_p, kvw_p, kvs_b, kvw_b, gates = _attn_proj(xp, g1, w_att, cos, sin)
    zh = _norm_proj(xp, g1, w_h)
    zm = _norm_proj(xp, g1, w_m)
    ckv = _compress_prompt(kvc_p, b_sz, s_len, cw)
    o_a = _nsa_prompt(qu, qr, ckv, kvs_b, kvw_b, gates, b_sz, s_len)
    o_b, hg_p = _hgrn_prompt(zh, hgrn_lb_logits, gn, b_sz, s_len)
    x1_p, h2_p, lg_p = _post_mixer(xp, o_a, o_b, zm, wpa, wpb, wo, norm2[0][None], wrt, br)

    xs = x_sample.reshape(n_s, D_MODEL)
    cos, sin = _rope_tables(past + jnp.tile(jnp.arange(t_len), d_sz))
    qu, qr, kvc_s, kvs_s, kvw_s, _, _, gates = _attn_proj(xs, g1, w_att, cos, sin)
    zh = _norm_proj(xs, g1, w_h)
    zm = _norm_proj(xs, g1, w_m)
    o_a = _nsa_sample(qu, qr, kvs_s, kvw_s, gates, cache_cmp_kv[0], cache_slc_kv[0], cache_win_kv[0], page_table, cw,
                      t_len)
    o_b, hg_s = _hgrn_sample(zh, state_hgrn[0], hgrn_lb_logits, hgrn_gnorm[0], d_sz, t_len)
    x1_s, h2_s, lg_s = _post_mixer(xs, o_a, o_b, zm, wpa, wpb, wo, norm2[0][None], wrt, br)

    y = _moe_and_final(jnp.concatenate([x1_p, x1_s], axis=0), jnp.concatenate([h2_p, h2_s], axis=0),
                       jnp.concatenate([lg_p, lg_s], axis=1), w_exp_gate[0].astype(BF16), w_exp_up[0].astype(BF16),
                       w_exp_down[0].astype(BF16), final_norm[None])
    y_p = y[:n_p].reshape(b_sz, s_len, D_MODEL)
    y_s = y[n_p:].reshape(d_sz, t_len, D_MODEL)

    win_p = kvw_p.reshape((b_sz, s_len) + kv_row)[:, s_len - min(WINDOW, s_len):]
    win_rows = jnp.concatenate([cache_win_kv[0], kvw_s.reshape((d_sz, t_len) + kv_row)], axis=1)
    win_s = win_rows[:, win_rows.shape[1] - min(WINDOW, past + t_len):]
    return (y_p, y_s,
            kvc_p.reshape((1, b_sz, s_len) + kv_row), kvc_s.reshape((1, d_sz, t_len) + kv_row),
            kvs_p.reshape((1, b_sz, s_len) + kv_row), kvs_s.reshape((1, d_sz, t_len) + kv_row),
            win_p[None], win_s[None], hg_p[None], hg_s[None])
```

```python
import functools
import math

import numpy as np
import jax
import jax.numpy as jnp
from jax import lax
from jax.experimental import pallas as pl
from jax.experimental.pallas import tpu as pltpu

D_MODEL = 1024
PAGE_SIZE = 128
NSA_HEADS = 8
NSA_KV_HEADS = 2
NSA_GROUP = NSA_HEADS // NSA_KV_HEADS
HEAD_DIM = 64
CMP_BLOCK = 32
CMP_STRIDE = 16
CMP_HIDDEN = 2 * HEAD_DIM
SEL_BLOCK = 64
N_SEL = 16
WINDOW = 512
ROPE_THETA = 10000.0
HGRN_HEADS = 4
HGRN_DK = 128
HGRN_DV = 128
N_GROUPS = 4
EXPERTS_PER_GROUP = 8
N_EXPERTS = N_GROUPS * EXPERTS_PER_GROUP
TOP_K_EXPERTS = 2
D_EXPERT = 512
NSA_WIDTH = NSA_HEADS * HEAD_DIM
KV_WIDTH = 2 * NSA_KV_HEADS * HEAD_DIM
HGRN_KW = HGRN_HEADS * HGRN_DK
HGRN_VW = HGRN_HEADS * HGRN_DV
IN_SPLITS = (NSA_WIDTH, KV_WIDTH, KV_WIDTH, KV_WIDTH, 3 * NSA_HEADS, HGRN_KW, HGRN_KW, HGRN_VW, HGRN_VW,
             D_MODEL, D_MODEL)
NORM_EPS = 1e-6
NEG = -1e9
BIG = 1e9

LANES = 128
VMEM_LIMIT = 56 * 1024 * 1024
TM_PROJ = 512
TQ = 128
TK_SLC = 256
HG_CHUNK = 64
HG_TS = 256
TM_POST = 256
TR_ROUTE = 512
MOE_BM = 256
T_DISP = 256

F32 = jnp.float32
BF16 = jnp.bfloat16
HIGHEST = lax.Precision.HIGHEST


def _dot(a, b, precision=None):
    return jnp.dot(a, b, preferred_element_type=F32, precision=precision)


def _dot_nt(a, b, precision=None):
    return lax.dot_general(a, b, (((1,), (1,)), ((), ())), preferred_element_type=F32, precision=precision)


def _tile(n, target, align=8):
    if n <= target:
        return n
    t = target - target % align
    while n % t:
        t -= align
    return t


def _pallas(body, **kw):
    fn = getattr(body, "func", body)
    return pl.pallas_call(body, name=fn.__name__.strip("_").removesuffix("_kernel"), **kw)


def _cparams(sem):
    return pltpu.CompilerParams(dimension_semantics=sem, vmem_limit_bytes=VMEM_LIMIT)


def _masked_softmax(s, ok):
    m = jnp.max(jnp.where(ok, s, NEG), axis=-1, keepdims=True)
    e = jnp.exp2(jnp.where(ok, s - m, NEG))
    return e / jnp.maximum(jnp.sum(e, axis=-1, keepdims=True), 1e-30)


def _rms(x, g):
    return x * lax.rsqrt(jnp.mean(x * x, axis=-1, keepdims=True) + NORM_EPS) * g


_A_Q, _A_QR, _A_KC, _A_KS, _A_KSR, _A_KW, _A_KWR, _A_G, _A_END = 0, 512, 1024, 1280, 1536, 1664, 1920, 2048, 2176


def _attn_proj_kernel(x_ref, g_ref, w_ref, cos_ref, sin_ref,
                      qu_ref, qr_ref, kvc_ref, kvs_ref, kvw_ref, kvsb_ref, kvwb_ref, gate_ref):
    h = _rms(x_ref[...], g_ref[...]).astype(BF16)
    z = _dot(h, w_ref[...])
    cos = cos_ref[...]
    sin = sin_ref[...]
    q = z[:, _A_Q:_A_QR]
    qu_ref[...] = q.astype(BF16)
    qr_ref[...] = (q * cos + z[:, _A_QR:_A_KC] * sin).astype(BF16)
    kvc_ref[...] = z[:, _A_KC:_A_KS]
    ck = cos[:, :LANES]
    sk = sin[:, :LANES]
    ks = z[:, _A_KS:_A_KS + 128] * ck + z[:, _A_KSR:_A_KW] * sk
    vs = z[:, _A_KS + 128:_A_KSR]
    kvs_ref[:, :128] = ks
    kvs_ref[:, 128:] = vs
    kvsb_ref[:, :128] = ks.astype(BF16)
    kvsb_ref[:, 128:] = vs.astype(BF16)
    kw = z[:, _A_KW:_A_KW + 128] * ck + z[:, _A_KWR:_A_G] * sk
    vw = z[:, _A_KW + 128:_A_KWR]
    kvw_ref[:, :128] = kw
    kvw_ref[:, 128:] = vw
    kvwb_ref[:, :128] = kw.astype(BF16)
    kvwb_ref[:, 128:] = vw.astype(BF16)
    gate_ref[...] = jax.nn.sigmoid(z[:, _A_G:_A_END])


def _rot_cols(w):
    d, n = w.shape
    w4 = w.reshape(d, n // HEAD_DIM, 2, HEAD_DIM // 2)
    return jnp.stack([-w4[:, :, 1], w4[:, :, 0]], axis=2).reshape(d, n)


def _pair_perm():
    idx = []
    for g in range(NSA_GROUP):
        for kvh in range(NSA_KV_HEADS):
            h = kvh * NSA_GROUP + g
            idx.extend(range(h * HEAD_DIM, (h + 1) * HEAD_DIM))
    return np.asarray(idx, np.int32)


def _attn_weight(w_in):
    offs = np.cumsum((0,) + IN_SPLITS)
    scale = HEAD_DIM ** -0.5 * math.log2(math.e)
    wq = w_in[:, offs[0]:offs[1]][:, _pair_perm()] * scale
    wkc = w_in[:, offs[1]:offs[2]]
    wks = w_in[:, offs[2]:offs[3]]
    wkw = w_in[:, offs[3]:offs[4]]
    wg = w_in[:, offs[4]:offs[5]]
    wg = jnp.pad(wg, ((0, 0), (0, LANES - wg.shape[1])))
    half = KV_WIDTH // 2
    w = jnp.concatenate([wq, _rot_cols(wq), wkc, wks, _rot_cols(wks[:, :half]), wkw, _rot_cols(wkw[:, :half]), wg],
                        axis=1)
    return w.astype(BF16)


def _rope_tables(pos):
    half = HEAD_DIM // 2
    inv = ROPE_THETA ** (-jnp.arange(half, dtype=F32) / half)
    ang = pos.astype(F32)[:, None] * inv[None, :]
    cos = jnp.tile(jnp.cos(ang), (1, 2 * NSA_HEADS))
    sin = jnp.tile(jnp.sin(ang), (1, 2 * NSA_HEADS))
    return cos, sin


def _attn_proj(x2, g, w_att, cos, sin):
    n = x2.shape[0]
    tm = _tile(n, TM_PROJ)
    n_tab = cos.shape[0] // tm
    row = lambda i: (i, 0)
    tab = lambda i: (i % n_tab, 0)
    const = lambda i: (0, 0)
    outs = [
        jax.ShapeDtypeStruct((n, NSA_WIDTH), BF16), jax.ShapeDtypeStruct((n, NSA_WIDTH), BF16),
        jax.ShapeDtypeStruct((n, KV_WIDTH), F32), jax.ShapeDtypeStruct((n, KV_WIDTH), F32),
        jax.ShapeDtypeStruct((n, KV_WIDTH), F32), jax.ShapeDtypeStruct((n, KV_WIDTH), BF16),
        jax.ShapeDtypeStruct((n, KV_WIDTH), BF16), jax.ShapeDtypeStruct((n, LANES), F32),
    ]
    return _pallas(
        _attn_proj_kernel,
        grid=(n // tm,),
        in_specs=[pl.BlockSpec((tm, D_MODEL), row), pl.BlockSpec((1, D_MODEL), const),
                  pl.BlockSpec((D_MODEL, _A_END), const),
                  pl.BlockSpec((tm, NSA_WIDTH), tab), pl.BlockSpec((tm, NSA_WIDTH), tab)],
        out_specs=[pl.BlockSpec((tm, o.shape[1]), row) for o in outs],
        out_shape=outs,
        compiler_params=_cparams(("parallel",)),
    )(x2, g, w_att, cos, sin)


def _norm_proj_kernel(x_ref, g_ref, w_ref, o_ref):
    h = _rms(x_ref[...], g_ref[...]).astype(BF16)
    o_ref[...] = _dot(h, w_ref[...])


def _norm_proj(x2, g, w):
    n = x2.shape[0]
    tm = _tile(n, TM_PROJ)
    width = w.shape[1]
    return _pallas(
        _norm_proj_kernel,
        grid=(n // tm,),
        in_specs=[pl.BlockSpec((tm, D_MODEL), lambda i: (i, 0)), pl.BlockSpec((1, D_MODEL), lambda i: (0, 0)),
                  pl.BlockSpec((D_MODEL, width), lambda i: (0, 0))],
        out_specs=pl.BlockSpec((tm, width), lambda i: (i, 0)),
        out_shape=jax.ShapeDtypeStruct((n, width), F32),
        compiler_params=_cparams(("parallel",)),
    )(x2, g, w)


def _compress_weights(cmp_pe, cmp_w1, cmp_b1, cmp_w2, cmp_b2):
    w1r = cmp_w1.reshape(2, 2, CMP_STRIDE, HEAD_DIM, CMP_HIDDEN)
    eye = jnp.eye(NSA_KV_HEADS, dtype=F32)
    w1bd = jnp.einsum('casdh,kj->cskdajh', w1r, eye).reshape(2, CMP_STRIDE * NSA_KV_HEADS * HEAD_DIM,
                                                             2 * NSA_KV_HEADS * CMP_HIDDEN)
    w2bd = jnp.einsum('chd,kj->ckhjd', cmp_w2, eye).reshape(2, NSA_KV_HEADS * CMP_HIDDEN, NSA_KV_HEADS * HEAD_DIM)
    pe8 = jnp.broadcast_to(cmp_pe.reshape(2, 1, CMP_BLOCK * HEAD_DIM), (2, 8, CMP_BLOCK * HEAD_DIM))
    b1t = jnp.tile(cmp_b1[:, None, :], (1, 1, NSA_KV_HEADS))
    b2t = jnp.tile(cmp_b2[:, None, :], (1, 1, NSA_KV_HEADS))
    return w1bd.astype(BF16), w2bd.astype(BF16), pe8, cmp_w1, b1t, b2t


def _compress_rows(rows_refs, n_ch, a_ref, w1bd_ref, w2bd_ref, pe8_ref, w1_ref, b1_ref, b2_ref, out_ref):
    for c in range(2):
        for s in range(CMP_STRIDE):
            a_ref[:, s * LANES:(s + 1) * LANES] = rows_refs[c][pl.ds(s, n_ch, stride=CMP_STRIDE), :].astype(BF16)
        hcat = _dot(a_ref[...], w1bd_ref[c])
        c0 = _dot(pe8_ref[c], w1_ref[c], precision=HIGHEST)[0:1, :]
        c0 = jnp.concatenate([c0, c0], axis=1) + b1_ref[c]
        h_b = pltpu.roll(hcat[:, 2 * LANES:], n_ch - 1, 0)
        hid = jax.nn.gelu(hcat[:, :2 * LANES] + h_b + c0)
        out_ref[:, c * LANES:(c + 1) * LANES] = _dot(hid.astype(BF16), w2bd_ref[c]) + b2_ref[c]


def _compress_prompt_kernel(k_ref, v_ref, w1bd_ref, w2bd_ref, pe8_ref, w1_ref, b1_ref, b2_ref, out_ref, a_ref, *, n_ch):
    _compress_rows((k_ref, v_ref), n_ch, a_ref, w1bd_ref, w2bd_ref, pe8_ref, w1_ref, b1_ref, b2_ref, out_ref)


def _cw_specs():
    z3 = lambda *a: (0, 0, 0)
    return [pl.BlockSpec((2, CMP_STRIDE * LANES, 4 * LANES), z3), pl.BlockSpec((2, 2 * LANES, LANES), z3),
            pl.BlockSpec((2, 8, CMP_BLOCK * HEAD_DIM), z3), pl.BlockSpec((2, CMP_BLOCK * HEAD_DIM, CMP_HIDDEN), z3),
            pl.BlockSpec((2, 1, 2 * LANES), z3), pl.BlockSpec((2, 1, LANES), z3)]


def _compress_prompt(kvc, b_sz, s_len, cw):
    n_ch = s_len // CMP_STRIDE
    return _pallas(
        functools.partial(_compress_prompt_kernel, n_ch=n_ch),
        grid=(b_sz,),
        in_specs=[pl.BlockSpec((s_len, LANES), lambda b: (b, 0)), pl.BlockSpec((s_len, LANES), lambda b: (b, 1))]
        + _cw_specs(),
        out_specs=pl.BlockSpec((n_ch, KV_WIDTH), lambda b: (b, 0)),
        out_shape=jax.ShapeDtypeStruct((b_sz * n_ch, KV_WIDTH), F32),
        scratch_shapes=[pltpu.VMEM((n_ch, CMP_STRIDE * LANES), BF16)],
        compiler_params=_cparams(("parallel",)),
    )(kvc, kvc, *cw)


def _cover_t(n_cmp_pad, n_cmp, n_sel, n_sel_pad):
    i = np.arange(n_cmp_pad)[None, :]
    j = np.arange(n_sel_pad)[:, None]
    start = i * CMP_STRIDE
    m = (start < (j + 1) * SEL_BLOCK) & (start + CMP_BLOCK > j * SEL_BLOCK) & (i < n_cmp) & (j < n_sel)
    return jnp.asarray(m.astype(np.float32))


def _select_t(imp_t, tp, n_sel):
    j = lax.broadcasted_iota(jnp.int32, imp_t.shape, 0)
    cur = tp // SEL_BLOCK
    valid = j * SEL_BLOCK <= tp
    forced = (j == 0) | (j == cur) | (j == cur - 1)
    score = jnp.where(valid, jnp.where(forced, BIG, imp_t), NEG)
    cnt = jnp.zeros(imp_t.shape, F32)
    for jp in range(n_sel):
        row = score[jp:jp + 1, :]
        cnt = cnt + jnp.where(j > jp, jnp.where(row >= score, 1.0, 0.0), jnp.where(row > score, 1.0, 0.0))
    return jnp.where((cnt < min(N_SEL, n_sel)) & valid, 1.0, 0.0)


def _online_step(s, ok, v, m, l, acc, v_t=False):
    m_new = jnp.maximum(m, jnp.max(jnp.where(ok, s, NEG), axis=-1, keepdims=True))
    e = jnp.where(ok, jnp.exp2(s - m_new), 0.0)
    alpha = jnp.exp2(m - m_new)
    l = alpha * l + jnp.sum(e, axis=-1, keepdims=True)
    acc = alpha * acc + (_dot_nt if v_t else _dot)(e.astype(BF16), v)
    return m_new, l, acc


TK_WIN = 256


def _nsa_prompt_kernel(qu_ref, qr_ref, ckv_ref, ks_ref, vs_ref, kw_ref, vw_ref, gate_ref, covt_ref, eb_ref, o_ref,
                       qu_s, qa_s, m_s, l_s, acc_s, oslc_s, *, n_sel):
    qi = pl.program_id(1)
    t0 = qi * TQ
    r = NSA_HEADS * TQ
    lane = lax.broadcasted_iota(jnp.int32, (TQ, LANES), 1)
    own = [lane < HEAD_DIM, lane >= HEAD_DIM]
    rpos = lax.broadcasted_iota(jnp.int32, (TQ, 1), 0)
    for g in range(NSA_GROUP):
        qug = qu_ref[:, g * LANES:(g + 1) * LANES]
        qrg = qr_ref[:, g * LANES:(g + 1) * LANES]
        for kvh in range(NSA_KV_HEADS):
            rows = pl.ds((g * NSA_KV_HEADS + kvh) * TQ, TQ)
            qu_s[rows, :] = jnp.where(own[kvh], qug, jnp.zeros_like(qug))
            qa_s[rows, :LANES] = jnp.where(own[kvh], qrg, jnp.zeros_like(qrg))

    ckv = ckv_ref[...]
    n_cmp_pad = ckv.shape[0]
    c_ok = (lax.broadcasted_iota(jnp.int32, (1, n_cmp_pad), 1) * CMP_STRIDE + (CMP_BLOCK - 1)) <= (t0 + rpos)
    s = _dot_nt(qu_s[...], ckv[:, :LANES].astype(BF16)).reshape(NSA_HEADS, TQ, n_cmp_pad)
    p = _masked_softmax(s, c_ok[None])
    o_cmp = _dot(p.reshape(r, n_cmp_pad).astype(BF16), ckv[:, LANES:].astype(BF16))
    p4 = p.reshape(NSA_GROUP, NSA_KV_HEADS * TQ, n_cmp_pad)
    psum = (p4[0] + p4[1]) + (p4[2] + p4[3])
    imp_t = _dot_nt(covt_ref[...], psum, precision=HIGHEST)
    tp = t0 + (lax.broadcasted_iota(jnp.int32, (n_sel, NSA_KV_HEADS * TQ), 1) % TQ)
    sel_t = _select_t(imp_t[:n_sel], tp, n_sel)
    unsel_t = jnp.concatenate([1.0 - sel_t, jnp.zeros((LANES - n_sel, NSA_KV_HEADS * TQ), F32)], axis=0)
    unsel = unsel_t.T.astype(BF16)
    for g in range(NSA_GROUP):
        qa_s[pl.ds(g * NSA_KV_HEADS * TQ, NSA_KV_HEADS * TQ), LANES:] = unsel

    ones = jnp.ones((TK_SLC, LANES), BF16)

    def reset():
        m_s[...] = jnp.full(m_s.shape, NEG, F32)
        l_s[...] = jnp.zeros(l_s.shape, F32)
        acc_s[...] = jnp.zeros(acc_s.shape, F32)

    def attend(s, v_t, mask):
        tk = s.shape[1]
        if mask is not None:
            s = jnp.where(mask[None], s.reshape(NSA_HEADS, TQ, tk), NEG).reshape(r, tk)
        m_old = m_s[...]
        m_new = jnp.maximum(m_old, jnp.broadcast_to(jnp.max(s, axis=-1, keepdims=True), (r, LANES)))
        e = jnp.concatenate([jnp.exp2(s[:, c:c + LANES] - m_new) for c in range(0, tk, LANES)], axis=1)
        pv = _dot(e.astype(BF16), jnp.concatenate([v_t, ones[:tk]], axis=1))
        alpha = jnp.exp2(m_old - m_new)
        acc_s[...] = alpha * acc_s[...] + pv[:, :LANES]
        l_s[...] = alpha * l_s[...] + pv[:, LANES:]
        m_s[...] = m_new

    def slc_tile(kt, causal):
        rows = pl.ds(pl.multiple_of(kt * TK_SLC, TK_SLC), TK_SLC)
        s = _dot_nt(qa_s[...], jnp.concatenate([ks_ref[rows, :], eb_ref[rows, :]], axis=1))
        mask = None
        if causal:
            mask = (kt * TK_SLC + lax.broadcasted_iota(jnp.int32, (1, TK_SLC), 1)) <= (t0 + rpos)
        attend(s, vs_ref[rows, :], mask)

    reset()
    n_full = t0 // TK_SLC

    def slc_body(kt, c):
        slc_tile(kt, False)
        return c

    lax.fori_loop(0, n_full, slc_body, 0)
    slc_tile(n_full, True)
    oslc_s[...] = acc_s[...] / jnp.maximum(l_s[...], 1e-30)

    reset()
    qr_all = qa_s[:, :LANES]
    for off, tk in ((0, TK_WIN), (TK_WIN, WINDOW - TK_WIN), (WINDOW, TQ)):
        rows = pl.ds(pl.multiple_of(t0 + off, TQ), tk)
        kpos = t0 + off - WINDOW + lax.broadcasted_iota(jnp.int32, (1, tk), 1)
        mask = (kpos >= 0) & (kpos <= t0 + rpos) & (kpos > t0 + rpos - WINDOW)
        attend(_dot_nt(qr_all, kw_ref[rows, :]), vw_ref[rows, :], mask)
    o_win = acc_s[...] / jnp.maximum(l_s[...], 1e-30)

    gt = gate_ref[...]
    o_slc = oslc_s[...]
    for g in range(NSA_GROUP):
        out = jnp.zeros((TQ, LANES), F32)
        for kvh in range(NSA_KV_HEADS):
            h = kvh * NSA_GROUP + g
            sl = slice((g * NSA_KV_HEADS + kvh) * TQ, (g * NSA_KV_HEADS + kvh + 1) * TQ)
            o = (gt[:, 3 * h:3 * h + 1] * o_cmp[sl] + gt[:, 3 * h + 1:3 * h + 2] * o_slc[sl]
                 + gt[:, 3 * h + 2:3 * h + 3] * o_win[sl])
            out = out + jnp.where(own[kvh], o, 0.0)
        o_ref[:, g * LANES:(g + 1) * LANES] = out.astype(BF16)


def _block_neg(n_keys):
    k = np.arange(n_keys)[:, None]
    j = np.arange(LANES)[None, :]
    return jnp.asarray(np.where(k // SEL_BLOCK == j, NEG, 0.0).astype(np.float32)).astype(BF16)


def _nsa_prompt(qu, qr, ckv, kvs_b, kvw_b, gates, b_sz, s_len):
    n_cmp_pad = s_len // CMP_STRIDE
    n_sel = s_len // SEL_BLOCK
    assert n_sel <= LANES and n_cmp_pad % LANES == 0 and WINDOW > TK_WIN and s_len % TK_SLC == 0
    covt = _cover_t(n_cmp_pad, n_cmp_pad - 1, n_sel, LANES)
    eb = _block_neg(s_len)
    kvw_pad = jnp.pad(kvw_b.reshape(b_sz, s_len, KV_WIDTH), ((0, 0), (WINDOW, 0), (0, 0)))
    kvw_pad = kvw_pad.reshape(b_sz * (s_len + WINDOW), KV_WIDTH)
    nq = s_len // TQ
    r = NSA_HEADS * TQ
    tile = lambda b, q: (b * nq + q, 0)
    const = lambda b, q: (0, 0)
    return _pallas(
        functools.partial(_nsa_prompt_kernel, n_sel=n_sel),
        grid=(b_sz, nq),
        in_specs=[pl.BlockSpec((TQ, NSA_WIDTH), tile), pl.BlockSpec((TQ, NSA_WIDTH), tile),
                  pl.BlockSpec((n_cmp_pad, KV_WIDTH), lambda b, q: (b, 0)),
                  pl.BlockSpec((s_len, LANES), lambda b, q: (b, 0)), pl.BlockSpec((s_len, LANES), lambda b, q: (b, 1)),
                  pl.BlockSpec((s_len + WINDOW, LANES), lambda b, q: (b, 0)),
                  pl.BlockSpec((s_len + WINDOW, LANES), lambda b, q: (b, 1)),
                  pl.BlockSpec((TQ, LANES), tile),
                  pl.BlockSpec((LANES, n_cmp_pad), const), pl.BlockSpec((s_len, LANES), const)],
        out_specs=pl.BlockSpec((TQ, NSA_WIDTH), tile),
        out_shape=jax.ShapeDtypeStruct((b_sz * s_len, NSA_WIDTH), BF16),
        scratch_shapes=[pltpu.VMEM((r, LANES), BF16), pltpu.VMEM((r, 2 * LANES), BF16),
                        pltpu.VMEM((r, LANES), F32), pltpu.VMEM((r, LANES), F32), pltpu.VMEM((r, LANES), F32),
                        pltpu.VMEM((r, LANES), F32)],
        compiler_params=_cparams(("parallel", "arbitrary")),
    )(qu, qr, ckv, kvs_b, kvs_b, kvw_pad, kvw_pad, gates, covt, eb)


def _block_expand(n_sel_pad, n_keys):
    j = np.arange(n_sel_pad)[:, None]
    k = np.arange(n_keys)[None, :]
    return jnp.asarray((k // SEL_BLOCK == j).astype(np.float32)).astype(BF16)


def _hgrn_consts(c):
    n_lvl = int(math.log2(c))
    hc = HGRN_HEADS * c
    t = np.arange(hc) % c
    head = np.arange(hc) // c
    same_head = head[:, None] == head[None, :]
    u = t[None, :]
    mall = np.zeros(((1 + n_lvl) * hc, hc), np.float32)
    masks = np.zeros((n_lvl, hc, hc), np.float32)
    mall[:hc] = same_head & (u <= t[:, None])
    for li in range(n_lvl):
        n = c >> (li + 1)
        blk = t // (2 * n)
        up_start = blk * 2 * n + n
        upper = t >= up_start
        m_up = upper[:, None] & (u >= up_start[:, None]) & (u <= t[:, None])
        m_lo = (~upper)[:, None] & (u > t[:, None]) & (u < up_start[:, None])
        mall[(1 + li) * hc:(2 + li) * hc] = same_head & (m_up | m_lo)
        masks[li] = same_head & upper[:, None] & (~upper)[None, :] & (blk[:, None] == blk[None, :])
    return jnp.asarray(mall).astype(BF16), jnp.asarray(masks)


def _split2(x):
    a = x.astype(BF16)
    return a, (x - a.astype(F32)).astype(BF16)


def _lower_bound(lbl):
    e = jnp.exp(lbl - jnp.max(lbl, axis=0, keepdims=True))
    return e[0:1] / jnp.sum(e, axis=0, keepdims=True)


def _hgrn_prompt_kernel(hq_ref, hf_ref, hi_ref, hg_ref, lbl_ref, gn_ref, mall_ref, mask_ref, o_ref, s_ref, st_ref,
                        *, c, n_lvl, ts):
    si = pl.program_id(1)

    @pl.when(si == 0)
    def _():
        st_ref[...] = jnp.zeros_like(st_ref)

    hc = HGRN_HEADS * c
    lb_all = _lower_bound(lbl_ref[...])
    lb = jnp.concatenate([jnp.broadcast_to(lb_all[:, h * LANES:(h + 1) * LANES], (c, LANES))
                          for h in range(HGRN_HEADS)], axis=0)
    gn = gn_ref[...]
    rowi = lax.broadcasted_iota(jnp.int32, (hc, 1), 0)

    def chunk(ci, carry):
        r0 = pl.multiple_of(ci * c, c)

        def stacked(ref):
            return jnp.concatenate([ref[pl.ds(r0, c), h * LANES:(h + 1) * LANES] for h in range(HGRN_HEADS)], axis=0)

        q = jax.nn.silu(stacked(hq_ref)) * HGRN_DK ** -0.5
        f = lb + (1.0 - lb) * jax.nn.sigmoid(stacked(hf_ref))
        k = 1.0 - f
        v = stacked(hi_ref)
        a, b = _split2(jnp.log(f))
        sums = _dot(mall_ref[...], jnp.concatenate([a, b], axis=1))
        sums = sums[:, :LANES] + sums[:, LANES:]
        cum = sums[:hc]
        att = jnp.zeros((hc, hc), F32)
        for li in range(n_lvl):
            n = c >> (li + 1)
            e = jnp.exp(sums[(1 + li) * hc:(2 + li) * hc])
            zz = (jnp.where((rowi & n) != 0, q, k) * e).astype(BF16)
            att = att + _dot_nt(zz, zz) * mask_ref[li]
        o = _dot(att.astype(BF16), v.astype(BF16)) + jnp.sum(q * k, axis=-1, keepdims=True) * v
        qe = (q * jnp.exp(cum)).astype(BF16)
        o_heads = []
        for h in range(HGRN_HEADS):
            rows = slice(h * c, (h + 1) * c)
            st = st_ref[h]
            o_heads.append(o[rows] + _dot_nt(qe[rows], st.astype(BF16)))
            last = cum[(h + 1) * c - 1:(h + 1) * c, :]
            kd = (k[rows] * jnp.exp(last - cum[rows])).astype(BF16)
            st_ref[h] = st * jnp.exp(last) + _dot(v[rows].T.astype(BF16), kd)
        o = jnp.concatenate(o_heads, axis=0)
        y = o * lax.rsqrt(jnp.mean(o * o, axis=-1, keepdims=True) + NORM_EPS) * gn * jax.nn.silu(stacked(hg_ref))
        for h in range(HGRN_HEADS):
            o_ref[pl.ds(r0, c), h * LANES:(h + 1) * LANES] = y[h * c:(h + 1) * c].astype(BF16)
        return carry

    lax.fori_loop(0, ts // c, chunk, 0)

    @pl.when(si == pl.num_programs(1) - 1)
    def _():
        for h in range(HGRN_HEADS):
            s_ref[0, h] = st_ref[h].T


def _hgrn_prompt(zh, lb_logits, gnorm, b_sz, s_len):
    c = HG_CHUNK
    ts = min(HG_TS, s_len)
    ns = s_len // ts
    n_lvl = int(math.log2(c))
    mall, masks = _hgrn_consts(c)
    col = lambda k: (lambda b, s: (b * ns + s, k))
    return _pallas(
        functools.partial(_hgrn_prompt_kernel, c=c, n_lvl=n_lvl, ts=ts),
        grid=(b_sz, ns),
        in_specs=[pl.BlockSpec((ts, HGRN_KW), col(0)), pl.BlockSpec((ts, HGRN_KW), col(1)),
                  pl.BlockSpec((ts, HGRN_VW), col(2)), pl.BlockSpec((ts, HGRN_VW), col(3)),
                  pl.BlockSpec((lb_logits.shape[0], HGRN_KW), lambda b, s: (0, 0)),
                  pl.BlockSpec((1, LANES), lambda b, s: (0, 0)),
                  pl.BlockSpec(mall.shape, lambda b, s: (0, 0)),
                  pl.BlockSpec(masks.shape, lambda b, s: (0, 0, 0))],
        out_specs=[pl.BlockSpec((ts, HGRN_VW), lambda b, s: (b * ns + s, 0)),
                   pl.BlockSpec((1, HGRN_HEADS, HGRN_DK, HGRN_DV), lambda b, s: (b, 0, 0, 0))],
        out_shape=[jax.ShapeDtypeStruct((b_sz * s_len, HGRN_VW), BF16),
                   jax.ShapeDtypeStruct((b_sz, HGRN_HEADS, HGRN_DK, HGRN_DV), F32)],
        scratch_shapes=[pltpu.VMEM((HGRN_HEADS, HGRN_DV, HGRN_DK), F32)],
        compiler_params=_cparams(("parallel", "arbitrary")),
    )(zh, zh, zh, zh, lb_logits, gnorm, mall, masks)


_R_EXP0 = 8


def _post_kernel(x_ref, oa_ref, ob_ref, ma_ref, mb_ref, wpa_ref, wpb_ref, wo_ref, g2_ref, wr_ref, br_ref,
                 x1_ref, h2_ref, lg_ref):
    pa = _dot(oa_ref[...], wpa_ref[...])
    pb = _dot(ob_ref[...], wpb_ref[...])
    merged = jax.nn.sigmoid(ma_ref[...]) * pa + jax.nn.sigmoid(mb_ref[...]) * pb
    x1 = x_ref[...] + _dot(merged.astype(BF16), wo_ref[...])
    h2 = _rms(x1, g2_ref[...])
    x1_ref[...] = x1
    h2_ref[...] = h2
    h_hi = h2.astype(BF16)
    h_lo = (h2 - h_hi.astype(F32)).astype(BF16)
    lg = _dot(jnp.concatenate([h_hi, h_hi, h_lo], axis=1), wr_ref[...]) + br_ref[...]
    lg_ref[...] = lg.T


def _router_weights(w_rg, b_rg, w_re, b_re):
    wr = jnp.zeros((D_MODEL, LANES), F32)
    wr = wr.at[:, :N_GROUPS].set(w_rg)
    wr = wr.at[:, _R_EXP0:_R_EXP0 + N_EXPERTS].set(jnp.transpose(w_re, (1, 0, 2)).reshape(D_MODEL, N_EXPERTS))
    w_hi = wr.astype(BF16)
    w_lo = (wr - w_hi.astype(F32)).astype(BF16)
    br = jnp.zeros((1, LANES), F32)
    br = br.at[0, :N_GROUPS].set(b_rg)
    br = br.at[0, _R_EXP0:_R_EXP0 + N_EXPERTS].set(b_re.reshape(N_EXPERTS))
    return jnp.concatenate([w_hi, w_lo, w_hi], axis=0), br


def _post_mixer(x2, o_a, o_b, zm, wpa, wpb, wo, g2, wrt, br):
    n = x2.shape[0]
    tm = _tile(n, TM_POST)
    row = lambda i: (i, 0)
    const = lambda i: (0, 0)
    return _pallas(
        _post_kernel,
        grid=(n // tm,),
        in_specs=[pl.BlockSpec((tm, D_MODEL), row), pl.BlockSpec((tm, NSA_WIDTH), row),
                  pl.BlockSpec((tm, HGRN_VW), row),
                  pl.BlockSpec((tm, D_MODEL), lambda i: (i, 0)), pl.BlockSpec((tm, D_MODEL), lambda i: (i, 1)),
                  pl.BlockSpec((NSA_WIDTH, D_MODEL), const), pl.BlockSpec((HGRN_VW, D_MODEL), const),
                  pl.BlockSpec((D_MODEL, D_MODEL), const), pl.BlockSpec((1, D_MODEL), const),
                  pl.BlockSpec((3 * D_MODEL, LANES), const), pl.BlockSpec((1, LANES), const)],
        out_specs=[pl.BlockSpec((tm, D_MODEL), row), pl.BlockSpec((tm, D_MODEL), row),
                   pl.BlockSpec((LANES, tm), lambda i: (0, i))],
        out_shape=[jax.ShapeDtypeStruct((n, D_MODEL), F32), jax.ShapeDtypeStruct((n, D_MODEL), F32),
                   jax.ShapeDtypeStruct((LANES, n), F32)],
        compiler_params=_cparams(("parallel",)),
    )(x2, o_a, o_b, zm, zm, wpa, wpb, wo, g2, wrt, br)


def _lowest_argmax(vals, top):
    idx = jnp.full(top.shape, len(vals) - 1, jnp.int32)
    for i in range(len(vals) - 2, -1, -1):
        idx = jnp.where(vals[i] == top, i, idx)
    return idx


def _route_kernel(lg_ref, u_ref, eid_ref, gate_ref, rank_ref, cnt_ref, carry_ref):
    @pl.when(pl.program_id(0) == 0)
    def _():
        carry_ref[...] = jnp.zeros_like(carry_ref)

    lg = lg_ref[...]
    tr = lg.shape[1]
    grp = [lg[g:g + 1] for g in range(N_GROUPS)]
    mx = functools.reduce(jnp.maximum, grp)
    ex = [jnp.exp(v - mx) for v in grp]
    den = functools.reduce(lambda a, b: a + b, ex)
    pr = [e / den for e in ex]
    pg = functools.reduce(jnp.maximum, pr)
    gtop = _lowest_argmax(pr, pg)
    le = []
    for j in range(EXPERTS_PER_GROUP):
        v = lg[_R_EXP0 + j:_R_EXP0 + j + 1]
        for g in range(1, N_GROUPS):
            r = _R_EXP0 + g * EXPERTS_PER_GROUP + j
            v = jnp.where(gtop == g, lg[r:r + 1], v)
        le.append(v)
    m1 = functools.reduce(jnp.maximum, le)
    i1 = _lowest_argmax(le, m1)
    le2 = [jnp.where(i1 == j, -jnp.inf, le[j]) for j in range(EXPERTS_PER_GROUP)]
    m2 = functools.reduce(jnp.maximum, le2)
    i2 = _lowest_argmax(le2, m2)
    e2 = jnp.exp(m2 - m1)
    den2 = 1.0 + e2
    gate1 = (1.0 / den2) * pg
    gate2 = (e2 / den2) * pg
    eid1 = gtop * EXPERTS_PER_GROUP + i1
    eid2 = gtop * EXPERTS_PER_GROUP + i2

    eio = lax.broadcasted_iota(jnp.int32, (N_EXPERTS, tr), 0)
    oh1 = jnp.where(eio == eid1, 1.0, 0.0)
    oh2 = jnp.where(eio == eid2, 1.0, 0.0)
    both = (oh1 + oh2).astype(BF16)
    carry = carry_ref[...]
    before = _dot(both, u_ref[...]) + jnp.concatenate([carry] * (tr // LANES), axis=1)
    rank1 = jnp.sum(oh1 * before, axis=0, keepdims=True).astype(jnp.int32)
    rank2 = jnp.sum(oh2 * before, axis=0, keepdims=True).astype(jnp.int32)
    carry = carry + _dot(both, jnp.ones((tr, LANES), BF16))
    carry_ref[...] = carry
    cnt_ref[...] = carry
    zi = jnp.zeros((6, tr), jnp.int32)
    eid_ref[...] = jnp.concatenate([eid1, eid2, zi], axis=0)
    rank_ref[...] = jnp.concatenate([rank1, rank2, zi], axis=0)
    gate_ref[...] = jnp.concatenate([gate1, gate2, jnp.zeros((6, tr), F32)], axis=0)


def _route(lg):
    n = lg.shape[1]
    tr = _tile(n, TR_ROUTE, LANES)
    u = jnp.asarray(np.triu(np.ones((tr, tr), np.float32), 1)).astype(BF16)
    col = lambda i: (0, i)
    return _pallas(
        _route_kernel,
        grid=(n // tr,),
        in_specs=[pl.BlockSpec((LANES, tr), col), pl.BlockSpec((tr, tr), lambda i: (0, 0))],
        out_specs=[pl.BlockSpec((8, tr), col), pl.BlockSpec((8, tr), col), pl.BlockSpec((8, tr), col),
                   pl.BlockSpec((N_EXPERTS, LANES), lambda i: (0, 0))],
        out_shape=[jax.ShapeDtypeStruct((8, n), jnp.int32), jax.ShapeDtypeStruct((8, n), F32),
                   jax.ShapeDtypeStruct((8, n), jnp.int32), jax.ShapeDtypeStruct((N_EXPERTS, LANES), F32)],
        scratch_shapes=[pltpu.VMEM((N_EXPERTS, LANES), F32)],
        compiler_params=_cparams(("arbitrary",)),
    )(lg, u)


def _dest_kernel(ps_ref, eid_ref, rank_ref, dest_ref):
    eid = eid_ref[...]
    dest = rank_ref[...]
    for e in range(N_EXPERTS):
        dest = dest + jnp.where(eid == e, ps_ref[e], 0)
    dest_ref[...] = dest


def _dest_rows(pad_start, eid, rank):
    n = eid.shape[1]
    tr = _tile(n, 4096, LANES)
    col = lambda i: (0, i)
    return _pallas(
        _dest_kernel,
        grid=(n // tr,),
        in_specs=[pl.BlockSpec(memory_space=pltpu.SMEM), pl.BlockSpec((8, tr), col), pl.BlockSpec((8, tr), col)],
        out_specs=pl.BlockSpec((8, tr), col),
        out_shape=jax.ShapeDtypeStruct((8, n), jnp.int32),
        compiler_params=_cparams(("parallel",)),
    )(pad_start, eid, rank)


DMA_UNROLL = 8


def _row_loops(t_n, row_copy):
    def issue(t, c):
        for k in range(TOP_K_EXPERTS):
            row_copy(t, k).start()
        return c

    def wait(t, c):
        for k in range(TOP_K_EXPERTS):
            row_copy(t, k).wait()
        return c

    lax.fori_loop(0, t_n, issue, 0, unroll=DMA_UNROLL)
    lax.fori_loop(0, t_n, wait, 0, unroll=DMA_UNROLL)


def _dispatch_kernel(dest_ref, h_ref, xb_in_ref, xb_ref, sem):
    del xb_in_ref

    def row_copy(t, k):
        return pltpu.make_async_copy(h_ref.at[pl.ds(t, 1), :], xb_ref.at[pl.ds(dest_ref[k, t], 1), :], sem)

    _row_loops(h_ref.shape[0], row_copy)


def _smem_cols(t):
    return pl.BlockSpec((8, t), lambda i: (0, i), memory_space=pltpu.SMEM)


def _dispatch(dest, h2, rows_total):
    n = h2.shape[0]
    t = _tile(n, T_DISP, LANES)
    xb0 = jnp.zeros((rows_total, D_MODEL), F32)
    return _pallas(
        _dispatch_kernel,
        grid=(n // t,),
        in_specs=[_smem_cols(t), pl.BlockSpec((t, D_MODEL), lambda i: (i, 0)), pl.BlockSpec(memory_space=pl.ANY)],
        out_specs=pl.BlockSpec(memory_space=pl.ANY),
        out_shape=jax.ShapeDtypeStruct((rows_total, D_MODEL), F32),
        scratch_shapes=[pltpu.SemaphoreType.DMA(())],
        input_output_aliases={2: 0},
        compiler_params=_cparams(("arbitrary",)),
    )(dest, h2, xb0)


def _expert_kernel(blk_e_ref, x_ref, wg_ref, wu_ref, wd_ref, y_ref):
    del blk_e_ref
    x = x_ref[...].astype(BF16)
    hid = jax.nn.silu(_dot(x, wg_ref[0])) * _dot(x, wu_ref[0])
    y_ref[...] = _dot(hid.astype(BF16), wd_ref[0])


def _experts(blk_e, xb, wg, wu, wd):
    rows_total = xb.shape[0]
    wsel = lambda i, e: (e[i], 0, 0)
    grid_spec = pltpu.PrefetchScalarGridSpec(
        num_scalar_prefetch=1,
        grid=(rows_total // MOE_BM,),
        in_specs=[pl.BlockSpec((MOE_BM, D_MODEL), lambda i, e: (i, 0)),
                  pl.BlockSpec((1, D_MODEL, D_EXPERT), wsel), pl.BlockSpec((1, D_MODEL, D_EXPERT), wsel),
                  pl.BlockSpec((1, D_EXPERT, D_MODEL), wsel)],
        out_specs=pl.BlockSpec((MOE_BM, D_MODEL), lambda i, e: (i, 0)),
    )
    return _pallas(
        _expert_kernel,
        grid_spec=grid_spec,
        out_shape=jax.ShapeDtypeStruct((rows_total, D_MODEL), F32),
        compiler_params=_cparams(("arbitrary",)),
    )(blk_e, xb, wg, wu, wd)


def _combine_kernel(dest_ref, x1_ref, gate_ref, fn_ref, yb_ref, y_ref, buf_ref, sem):
    def row_copy(t, k):
        return pltpu.make_async_copy(yb_ref.at[pl.ds(dest_ref[k, t], 1), :], buf_ref.at[k, pl.ds(t, 1), :], sem)

    _row_loops(x1_ref.shape[0], row_copy)
    g = gate_ref[...]
    out = x1_ref[...] + g[:, 0:1] * buf_ref[0] + g[:, 1:2] * buf_ref[1]
    y_ref[...] = _rms(out, fn_ref[...])


def _combine(dest, x1, gate_t, fnorm, yb):
    n = x1.shape[0]
    t = _tile(n, T_DISP, LANES)
    return _pallas(
        _combine_kernel,
        grid=(n // t,),
        in_specs=[_smem_cols(t),
                  pl.BlockSpec((t, D_MODEL), lambda i: (i, 0)), pl.BlockSpec((t, 8), lambda i: (i, 0)),
                  pl.BlockSpec((1, D_MODEL), lambda i: (0, 0)), pl.BlockSpec(memory_space=pl.ANY)],
        out_specs=pl.BlockSpec((t, D_MODEL), lambda i: (i, 0)),
        out_shape=jax.ShapeDtypeStruct((n, D_MODEL), F32),
        scratch_shapes=[pltpu.VMEM((TOP_K_EXPERTS, t, D_MODEL), F32), pltpu.SemaphoreType.DMA(())],
        compiler_params=_cparams(("arbitrary",)),
    )(dest, x1, gate_t, fnorm, yb)


def _moe_and_final(x1, h2, lg, wg, wu, wd, fnorm):
    n = x1.shape[0]
    eid, gate, rank, cnt = _route(lg)
    counts = cnt[:, 0].astype(jnp.int32)
    padded = (counts + MOE_BM - 1) // MOE_BM * MOE_BM
    pad_end = jnp.cumsum(padded)
    pad_start = (pad_end - padded).astype(jnp.int32)
    n_blocks = -(-(n * TOP_K_EXPERTS + N_EXPERTS * (MOE_BM - 1)) // MOE_BM)
    blk_start = jnp.arange(n_blocks, dtype=jnp.int32) * MOE_BM
    blk_e = jnp.minimum(jnp.sum(pad_end[None, :] <= blk_start[:, None], axis=1), N_EXPERTS - 1).astype(jnp.int32)
    dest = _dest_rows(pad_start, eid, rank)
    xb = _dispatch(dest, h2, n_blocks * MOE_BM)
    yb = _experts(blk_e, xb, wg, wu, wd)
    return _combine(dest, x1, gate.T, fnorm, yb)


def _page_stream(pt_ref, pool_ref, buf_ref, sem, n_pages):
    b = pl.program_id(0)
    slot = b % 2

    def page_copy(bb, sl, p):
        col = pl.ds(pl.multiple_of(p * PAGE_SIZE, PAGE_SIZE), PAGE_SIZE)
        return pltpu.make_async_copy(pool_ref.at[pt_ref[bb, p]], buf_ref.at[sl, :, col], sem.at[sl])

    def issue(bb, sl):
        def body(p, c):
            page_copy(bb, sl, p).start()
            return c
        lax.fori_loop(0, n_pages, body, 0)

    def wait():
        def body(p, c):
            page_copy(b, slot, p).wait()
            return c
        lax.fori_loop(0, n_pages, body, 0)

    @pl.when(b == 0)
    def _():
        issue(0, 0)

    @pl.when(b + 1 < pl.num_programs(0))
    def _():
        issue(b + 1, 1 - slot)

    return slot, wait


TR_CHUNK = 1024


def _compress_sample_kernel(pt_ref, pool_ref, w1bd_ref, w2bd_ref, pe8_ref, w1_ref, b1_ref, b2_ref, out_ref,
                            buf_ref, kbuf_ref, vbuf_ref, a_ref, sem, *, n_pages, n_ch):
    slot, wait = _page_stream(pt_ref, pool_ref, buf_ref, sem, n_pages)
    wait()
    for c, dst in ((0, kbuf_ref), (1, vbuf_ref)):
        for j in range(0, n_pages * PAGE_SIZE, TR_CHUNK):
            dst[j:j + TR_CHUNK, :] = buf_ref[slot, c * LANES:(c + 1) * LANES, j:j + TR_CHUNK].T
    _compress_rows((kbuf_ref, vbuf_ref), n_ch, a_ref, w1bd_ref, w2bd_ref, pe8_ref, w1_ref, b1_ref, b2_ref, out_ref)


def _compress_sample(page_table, pool, cw):
    b_sz, n_pages = page_table.shape
    past = n_pages * PAGE_SIZE
    n_ch = past // CMP_STRIDE
    z3 = lambda b, pt: (0, 0, 0)
    specs = [pl.BlockSpec((2, CMP_STRIDE * LANES, 4 * LANES), z3), pl.BlockSpec((2, 2 * LANES, LANES), z3),
             pl.BlockSpec((2, 8, CMP_BLOCK * HEAD_DIM), z3), pl.BlockSpec((2, CMP_BLOCK * HEAD_DIM, CMP_HIDDEN), z3),
             pl.BlockSpec((2, 1, 2 * LANES), z3), pl.BlockSpec((2, 1, LANES), z3)]
    grid_spec = pltpu.PrefetchScalarGridSpec(
        num_scalar_prefetch=1,
        grid=(b_sz,),
        in_specs=[pl.BlockSpec(memory_space=pl.ANY)] + specs,
        out_specs=pl.BlockSpec((n_ch, KV_WIDTH), lambda b, pt: (b, 0)),
        scratch_shapes=[pltpu.VMEM((2, KV_WIDTH, past), F32), pltpu.VMEM((past, LANES), F32),
                        pltpu.VMEM((past, LANES), F32), pltpu.VMEM((n_ch, CMP_STRIDE * LANES), BF16),
                        pltpu.SemaphoreType.DMA((2,))],
    )
    return _pallas(
        functools.partial(_compress_sample_kernel, n_pages=n_pages, n_ch=n_ch),
        grid_spec=grid_spec,
        out_shape=jax.ShapeDtypeStruct((b_sz * n_ch, KV_WIDTH), F32),
        compiler_params=_cparams(("arbitrary",)),
    )(page_table, pool, *cw)


def _cmp_sample_kernel(qu_ref, ckv_ref, covt_ref, ocmp_ref, selt_ref, *, past, t_len, n_sel):
    r = qu_ref.shape[1]
    qu = qu_ref[0]
    ckv = ckv_ref[...]
    n_cmp_pad = ckv.shape[0]
    tpos = past + (lax.broadcasted_iota(jnp.int32, (r, 1), 0) % t_len)
    s = _dot_nt(qu, ckv[:, :LANES].astype(BF16))
    c_end = lax.broadcasted_iota(jnp.int32, (1, n_cmp_pad), 1) * CMP_STRIDE + (CMP_BLOCK - 1)
    p = _masked_softmax(s, c_end <= tpos)
    ocmp_ref[0] = _dot(p.astype(BF16), ckv[:, LANES:].astype(BF16))
    psum = []
    for kvh in range(NSA_KV_HEADS):
        acc = None
        for g in range(NSA_GROUP):
            blk = g * NSA_KV_HEADS + kvh
            pg = p[blk * t_len:(blk + 1) * t_len]
            acc = pg if acc is None else acc + pg
        psum.append(acc)
    psum = jnp.concatenate(psum, axis=0)
    imp_t = _dot_nt(covt_ref[...], psum, precision=HIGHEST)
    tp = past + (lax.broadcasted_iota(jnp.int32, imp_t.shape, 1) % t_len)
    selt_ref[0] = _select_t(imp_t, tp, n_sel)


def _cmp_sample(qu32, ckv, b_sz, past, t_len, n_sel, n_sel_pad):
    n_cmp_pad = past // CMP_STRIDE
    covt = _cover_t(n_cmp_pad, n_cmp_pad - 1, n_sel, n_sel_pad)
    r = qu32.shape[1]
    return _pallas(
        functools.partial(_cmp_sample_kernel, past=past, t_len=t_len, n_sel=n_sel),
        grid=(b_sz,),
        in_specs=[pl.BlockSpec((1, r, LANES), lambda b: (b, 0, 0)),
                  pl.BlockSpec((n_cmp_pad, KV_WIDTH), lambda b: (b, 0)),
                  pl.BlockSpec((n_sel_pad, n_cmp_pad), lambda b: (0, 0))],
        out_specs=[pl.BlockSpec((1, r, LANES), lambda b: (b, 0, 0)),
                   pl.BlockSpec((1, n_sel_pad, NSA_KV_HEADS * t_len), lambda b: (b, 0, 0))],
        out_shape=[jax.ShapeDtypeStruct((b_sz, r, LANES), F32),
                   jax.ShapeDtypeStruct((b_sz, n_sel_pad, NSA_KV_HEADS * t_len), F32)],
        compiler_params=_cparams(("parallel",)),
    )(qu32, ckv, covt)


TK_SAMPLE = 1024


def _slc_win_sample_kernel(pt_ref, qr_ref, sel_ref, e_ref, pool_ref, new_s_ref, win_ref, new_w_ref, ocmp_ref, gate_ref,
                           o_ref, buf_ref, sem, *, n_pages, past, t_len, win_buf):
    slot, wait_pages = _page_stream(pt_ref, pool_ref, buf_ref, sem, n_pages)
    qr = qr_ref[0]
    sel = sel_ref[0]
    r = qr.shape[0]
    tpos = past + (lax.broadcasted_iota(jnp.int32, (r, 1), 0) % t_len)
    init = (jnp.full((r, 1), NEG, F32), jnp.zeros((r, 1), F32), jnp.zeros((r, LANES), F32))
    new_i = lax.broadcasted_iota(jnp.int32, (1, LANES), 1)
    new_ok = ((past + new_i) <= tpos) & (new_i < t_len)

    wt = win_ref[0]
    wpos = past - win_buf + lax.broadcasted_iota(jnp.int32, (1, win_buf), 1)
    ok = (wpos >= 0) & (wpos <= tpos) & (wpos > tpos - WINDOW)
    carry = _online_step(_dot(qr, wt[:LANES].astype(BF16)), ok, wt[LANES:].astype(BF16), *init, v_t=True)
    rows = new_w_ref[0]
    ok = new_ok & ((past + new_i) > tpos - WINDOW)
    m, l, acc = _online_step(_dot_nt(qr, rows[:, :LANES].astype(BF16)), ok, rows[:, LANES:].astype(BF16), *carry)
    o_win = acc / jnp.maximum(l, 1e-30)

    wait_pages()

    def slc_body(kt, carry):
        col = pl.ds(pl.multiple_of(kt * TK_SAMPLE, TK_SAMPLE), TK_SAMPLE)
        hit = _dot(sel, e_ref[:, col])
        return _online_step(_dot(qr, buf_ref[slot, :LANES, col].astype(BF16)), hit > 0.5,
                            buf_ref[slot, LANES:, col].astype(BF16), *carry, v_t=True)

    carry = lax.fori_loop(0, past // TK_SAMPLE, slc_body, init)
    rows = new_s_ref[0]
    n_past_blk = past // SEL_BLOCK
    ok = new_ok & (sel[:, n_past_blk:n_past_blk + 1] > 0.5)
    m, l, acc = _online_step(_dot_nt(qr, rows[:, :LANES].astype(BF16)), ok, rows[:, LANES:].astype(BF16), *carry)
    o_slc = acc / jnp.maximum(l, 1e-30)

    g = gate_ref[0]
    o = g[:, 0:1] * ocmp_ref[0] + g[:, 1:2] * o_slc + g[:, 2:3] * o_win
    row = lax.broadcasted_iota(jnp.int32, (r, LANES), 0)
    lane = lax.broadcasted_iota(jnp.int32, (r, LANES), 1)
    o_ref[0] = jnp.where((((row // t_len) % 2) == 0) == (lane < HEAD_DIM), o, 0.0)


def _slc_win_sample(page_table, qr32, sel32, pool, new_s, win_t, new_w, ocmp32, gate32, past, t_len, n_sel_pad):
    b_sz, n_pages = page_table.shape
    win_buf = win_t.shape[2]
    r = qr32.shape[1]
    e_mat = _block_expand(n_sel_pad, past)
    b3 = lambda b, pt: (b, 0, 0)
    grid_spec = pltpu.PrefetchScalarGridSpec(
        num_scalar_prefetch=1,
        grid=(b_sz,),
        in_specs=[pl.BlockSpec((1, r, LANES), b3), pl.BlockSpec((1, r, n_sel_pad), b3),
                  pl.BlockSpec((n_sel_pad, past), lambda b, pt: (0, 0)),
                  pl.BlockSpec(memory_space=pl.ANY),
                  pl.BlockSpec((1, LANES, KV_WIDTH), b3),
                  pl.BlockSpec((1, KV_WIDTH, win_buf), b3),
                  pl.BlockSpec((1, LANES, KV_WIDTH), b3),
                  pl.BlockSpec((1, r, LANES), b3), pl.BlockSpec((1, r, 3), b3)],
        out_specs=pl.BlockSpec((1, r, LANES), b3),
        scratch_shapes=[pltpu.VMEM((2, KV_WIDTH, past), F32), pltpu.SemaphoreType.DMA((2,))],
    )
    return _pallas(
        functools.partial(_slc_win_sample_kernel, n_pages=n_pages, past=past, t_len=t_len, win_buf=win_buf),
        grid_spec=grid_spec,
        out_shape=jax.ShapeDtypeStruct((b_sz, r, LANES), F32),
        compiler_params=_cparams(("arbitrary",)),
    )(page_table, qr32, sel32, e_mat, pool, new_s, win_t, new_w, ocmp32, gate32)


def _rows32(q, b_sz, t_len):
    q5 = q.reshape(b_sz, t_len, NSA_GROUP, NSA_KV_HEADS, HEAD_DIM).transpose(0, 2, 3, 1, 4)
    eye = jnp.eye(NSA_KV_HEADS, dtype=q.dtype)
    return jnp.einsum('bgktd,kj->bgktjd', q5, eye).reshape(b_sz, NSA_HEADS * t_len, LANES)


def _from_rows32(o32, b_sz, t_len):
    o6 = o32.reshape(b_sz, NSA_GROUP, NSA_KV_HEADS, t_len, NSA_KV_HEADS, HEAD_DIM)
    o5 = jnp.stack([o6[:, :, k, :, k] for k in range(NSA_KV_HEADS)], axis=2)
    return o5.transpose(0, 3, 1, 2, 4).reshape(b_sz * t_len, NSA_WIDTH)


def _nsa_sample(qu, qr, kvs_new, kvw_new, gates, cache_cmp, cache_slc, cache_win, page_table, cw, t_len):
    b_sz, n_pages = page_table.shape
    past = n_pages * PAGE_SIZE
    total = past + t_len
    assert t_len < CMP_STRIDE and t_len <= SEL_BLOCK and past % TK_SAMPLE == 0
    n_sel = -(-total // SEL_BLOCK)
    n_sel_pad = -(-n_sel // 8) * 8
    pool_c = cache_cmp.reshape(-1, PAGE_SIZE, KV_WIDTH).transpose(0, 2, 1)
    pool_s = cache_slc.reshape(-1, PAGE_SIZE, KV_WIDTH).transpose(0, 2, 1)
    win_t = cache_win.reshape(b_sz, cache_win.shape[1], KV_WIDTH).transpose(0, 2, 1)
    ckv = _compress_sample(page_table, pool_c, cw)
    ocmp32, sel_t = _cmp_sample(_rows32(qu, b_sz, t_len), ckv, b_sz, past, t_len, n_sel, n_sel_pad)
    sel = sel_t.transpose(0, 2, 1).reshape(b_sz, 1, NSA_KV_HEADS * t_len, n_sel_pad)
    sel32 = jnp.broadcast_to(sel, (b_sz, NSA_GROUP, NSA_KV_HEADS * t_len, n_sel_pad))
    sel32 = sel32.reshape(b_sz, NSA_HEADS * t_len, n_sel_pad).astype(BF16)
    pad_rows = lambda a: jnp.pad(a.reshape(b_sz, t_len, KV_WIDTH), ((0, 0), (0, LANES - t_len), (0, 0)))
    g4 = gates[:, :3 * NSA_HEADS].reshape(b_sz, t_len, NSA_KV_HEADS, NSA_GROUP, 3)
    gate32 = g4.transpose(0, 3, 2, 1, 4).reshape(b_sz, NSA_HEADS * t_len, 3)
    o32 = _slc_win_sample(page_table, _rows32(qr, b_sz, t_len), sel32, pool_s, pad_rows(kvs_new), win_t,
                          pad_rows(kvw_new), ocmp32, gate32, past, t_len, n_sel_pad)
    return _from_rows32(o32, b_sz, t_len).astype(BF16)


def _hgrn_sample_kernel(hq_ref, hf_ref, vt_ref, hgt_ref, s0_ref, lbl_ref, gnt_ref, ot_ref, s_ref, *, t_len):
    lb = _lower_bound(lbl_ref[...])
    q = jax.nn.silu(hq_ref[0]) * HGRN_DK ** -0.5
    f = lb + (1.0 - lb) * jax.nn.sigmoid(hf_ref[0])
    k = 1.0 - f
    vt = vt_ref[0, 0]
    st = s0_ref[0, 0].T
    cols = []
    for t in range(t_len):
        st = st * f[t:t + 1] + vt[:, t:t + 1] * k[t:t + 1]
        cols.append(jnp.sum(st * q[t:t + 1], axis=1, keepdims=True))
    ot = jnp.concatenate(cols, axis=1)
    y = ot * lax.rsqrt(jnp.mean(ot * ot, axis=0, keepdims=True) + NORM_EPS) * gnt_ref[...]
    ot_ref[0, 0] = y * jax.nn.silu(hgt_ref[0, 0])
    s_ref[0, 0] = st.T


def _hgrn_sample(zh, s0, lb_logits, gnorm, b_sz, t_len):
    z3 = zh.reshape(b_sz, t_len, 4 * HGRN_KW)
    to_t = lambda a: a.reshape(b_sz, t_len, HGRN_HEADS, HGRN_DV).transpose(0, 2, 3, 1)
    vt = to_t(z3[:, :, 2 * HGRN_KW:3 * HGRN_KW])
    hgt = to_t(z3[:, :, 3 * HGRN_KW:])
    col = lambda k: (lambda b, h: (b, 0, h + HGRN_HEADS * k))
    b4 = lambda b, h: (b, h, 0, 0)
    ot, s_new = _pallas(
        functools.partial(_hgrn_sample_kernel, t_len=t_len),
        grid=(b_sz, HGRN_HEADS),
        in_specs=[pl.BlockSpec((1, t_len, LANES), col(0)), pl.BlockSpec((1, t_len, LANES), col(1)),
                  pl.BlockSpec((1, 1, HGRN_DV, t_len), b4), pl.BlockSpec((1, 1, HGRN_DV, t_len), b4),
                  pl.BlockSpec((1, 1, HGRN_DK, HGRN_DV), b4),
                  pl.BlockSpec((lb_logits.shape[0], LANES), lambda b, h: (0, h)),
                  pl.BlockSpec((HGRN_DV, 1), lambda b, h: (0, 0))],
        out_specs=[pl.BlockSpec((1, 1, HGRN_DV, t_len), b4), pl.BlockSpec((1, 1, HGRN_DK, HGRN_DV), b4)],
        out_shape=[jax.ShapeDtypeStruct((b_sz, HGRN_HEADS, HGRN_DV, t_len), F32),
                   jax.ShapeDtypeStruct((b_sz, HGRN_HEADS, HGRN_DK, HGRN_DV), F32)],
        compiler_params=_cparams(("parallel", "parallel")),
    )(z3, z3, vt, hgt, s0, lb_logits, gnorm.reshape(HGRN_DV, 1))
    o_b = ot.transpose(0, 3, 1, 2).reshape(b_sz * t_len, HGRN_VW)
    return o_b.astype(BF16), s_new


def kernel(x_prompt, x_sample, cache_cmp_kv, cache_slc_kv, cache_win_kv, state_hgrn, page_table, norm1, w_in, cmp_pe,
           cmp_w1, cmp_b1, cmp_w2, cmp_b2, hgrn_lb_logits, hgrn_gnorm, w_proj_a, w_proj_b, w_out, norm2,
           w_router_group, b_router_group, w_router_expert, b_router_expert, w_exp_gate, w_exp_up, w_exp_down,
           final_norm):
    assert w_in.shape[0] == 1, "one layer"
    b_sz, s_len, _ = x_prompt.shape
    d_sz, t_len, _ = x_sample.shape
    past = page_table.shape[1] * PAGE_SIZE
    n_p = b_sz * s_len
    n_s = d_sz * t_len
    kv_row = (2, NSA_KV_HEADS, HEAD_DIM)

    offs = np.cumsum((0,) + IN_SPLITS)
    w_att = _attn_weight(w_in[0])
    w_h = w_in[0][:, offs[5]:offs[9]].astype(BF16)
    w_m = w_in[0][:, offs[9]:offs[11]].astype(BF16)
    g1 = norm1[0][None]
    cw = _compress_weights(cmp_pe[0], cmp_w1[0], cmp_b1[0], cmp_w2[0], cmp_b2[0])
    perm = _pair_perm()
    wpa = w_proj_a[0][perm].astype(BF16)
    wpb = w_proj_b[0].astype(BF16)
    wo = w_out[0].astype(BF16)
    wrt, br = _router_weights(w_router_group[0], b_router_group[0], w_router_expert[0], b_router_expert[0])
    gn = hgrn_gnorm[0][None]

    xp = x_prompt.reshape(n_p, D_MODEL)
    cos, sin = _rope_tables(jnp.arange(s_len))
    qu, qr, kvc_p, kvs_p, kvw_p, kvs_b, kvw_b, gates = _attn_proj(xp, g1, w_att, cos, sin)
    zh = _norm_proj(xp, g1, w_h)
    zm = _norm_proj(xp, g1, w_m)
    ckv = _compress_prompt(kvc_p, b_sz, s_len, cw)
    o_a = _nsa_prompt(qu, qr, ckv, kvs_b, kvw_b, gates, b_sz, s_len)
    o_b, hg_p = _hgrn_prompt(zh, hgrn_lb_logits, gn, b_sz, s_len)
    x1_p, h2_p, lg_p = _post_mixer(xp, o_a, o_b, zm, wpa, wpb, wo, norm2[0][None], wrt, br)

    xs = x_sample.reshape(n_s, D_MODEL)
    cos, sin = _rope_tables(past + jnp.tile(jnp.arange(t_len), d_sz))
    qu, qr, kvc_s, kvs_s, kvw_s, _, _, gates = _attn_proj(xs, g1, w_att, cos, sin)
    zh = _norm_proj(xs, g1, w_h)
    zm = _norm_proj(xs, g1, w_m)
    o_a = _nsa_sample(qu, qr, kvs_s, kvw_s, gates, cache_cmp_kv[0], cache_slc_kv[0], cache_win_kv[0], page_table, cw,
                      t_len)
    o_b, hg_s = _hgrn_sample(zh, state_hgrn[0], hgrn_lb_logits, hgrn_gnorm[0], d_sz, t_len)
    x1_s, h2_s, lg_s = _post_mixer(xs, o_a, o_b, zm, wpa, wpb, wo, norm2[0][None], wrt, br)

    y = _moe_and_final(jnp.concatenate([x1_p, x1_s], axis=0), jnp.concatenate([h2_p, h2_s], axis=0),
                       jnp.concatenate([lg_p, lg_s], axis=1), w_exp_gate[0].astype(BF16), w_exp_up[0].astype(BF16),
                       w_exp_down[0].astype(BF16), final_norm[None])
    y_p = y[:n_p].reshape(b_sz, s_len, D_MODEL)
    y_s = y[n_p:].reshape(d_sz, t_len, D_MODEL)

    win_p = kvw_p.reshape((b_sz, s_len) + kv_row)[:, s_len - min(WINDOW, s_len):]
    win_rows = jnp.concatenate([cache_win_kv[0], kvw_s.reshape((d_sz, t_len) + kv_row)], axis=1)
    win_s = win_rows[:, win_rows.shape[1] - min(WINDOW, past + t_len):]
    return (y_p, y_s,
            kvc_p.reshape((1, b_sz, s_len) + kv_row), kvc_s.reshape((1, d_sz, t_len) + kv_row),
            kvs_p.reshape((1, b_sz, s_len) + kv_row), kvs_s.reshape((1, d_sz, t_len) + kv_row),
            win_p[None], win_s[None], hg_p[None], hg_s[None])
```

```python
import functools
import math

import numpy as np
import jax
import jax.numpy as jnp
from jax import lax
from jax.experimental import pallas as pl
from jax.experimental.pallas import tpu as pltpu

D_MODEL = 1024
PAGE_SIZE = 128
NSA_HEADS = 8
NSA_KV_HEADS = 2
NSA_GROUP = NSA_HEADS // NSA_KV_HEADS
HEAD_DIM = 64
CMP_BLOCK = 32
CMP_STRIDE = 16
CMP_HIDDEN = 2 * HEAD_DIM
SEL_BLOCK = 64
N_SEL = 16
WINDOW = 512
ROPE_THETA = 10000.0
HGRN_HEADS = 4
HGRN_DK = 128
HGRN_DV = 128
N_GROUPS = 4
EXPERTS_PER_GROUP = 8
N_EXPERTS = N_GROUPS * EXPERTS_PER_GROUP
TOP_K_EXPERTS = 2
D_EXPERT = 512
NSA_WIDTH = NSA_HEADS * HEAD_DIM
KV_WIDTH = 2 * NSA_KV_HEADS * HEAD_DIM
HGRN_KW = HGRN_HEADS * HGRN_DK
HGRN_VW = HGRN_HEADS * HGRN_DV
IN_SPLITS = (NSA_WIDTH, KV_WIDTH, KV_WIDTH, KV_WIDTH, 3 * NSA_HEADS, HGRN_KW, HGRN_KW, HGRN_VW, HGRN_VW,
             D_MODEL, D_MODEL)
NORM_EPS = 1e-6
NEG = -1e9
BIG = 1e9

LANES = 128
VMEM_LIMIT = 56 * 1024 * 1024
TM_PROJ = 512
TQ = 128
TK_SLC = 512
HG_CHUNK = 64
HG_TS = 256
TM_POST = 256
TR_ROUTE = 512
MOE_BM = 256
T_DISP = 256

F32 = jnp.float32
BF16 = jnp.bfloat16
HIGHEST = lax.Precision.HIGHEST


def _dot(a, b, precision=None):
    return jnp.dot(a, b, preferred_element_type=F32, precision=precision)


def _dot_nt(a, b, precision=None):
    return lax.dot_general(a, b, (((1,), (1,)), ((), ())), preferred_element_type=F32, precision=precision)


def _tile(n, target, align=8):
    if n <= target:
        return n
    t = target - target % align
    while n % t:
        t -= align
    return t


def _pallas(body, **kw):
    fn = getattr(body, "func", body)
    return pl.pallas_call(body, name=fn.__name__.strip("_").removesuffix("_kernel"), **kw)


def _cparams(sem):
    return pltpu.CompilerParams(dimension_semantics=sem, vmem_limit_bytes=VMEM_LIMIT)


def _masked_softmax(s, ok):
    m = jnp.max(jnp.where(ok, s, NEG), axis=-1, keepdims=True)
    e = jnp.exp2(jnp.where(ok, s - m, NEG))
    return e / jnp.maximum(jnp.sum(e, axis=-1, keepdims=True), 1e-30)


def _rms(x, g):
    return x * lax.rsqrt(jnp.mean(x * x, axis=-1, keepdims=True) + NORM_EPS) * g


_A_Q, _A_QR, _A_KC, _A_KS, _A_KSR, _A_KW, _A_KWR, _A_G, _A_END = 0, 512, 1024, 1280, 1536, 1664, 1920, 2048, 2176


def _attn_proj_kernel(x_ref, g_ref, w_ref, cos_ref, sin_ref,
                      qu_ref, qr_ref, kvc_ref, kvs_ref, kvw_ref, kvsb_ref, kvwb_ref, gate_ref):
    h = _rms(x_ref[...], g_ref[...]).astype(BF16)
    z = _dot(h, w_ref[...])
    cos = cos_ref[...]
    sin = sin_ref[...]
    q = z[:, _A_Q:_A_QR]
    qu_ref[...] = q.astype(BF16)
    qr_ref[...] = (q * cos + z[:, _A_QR:_A_KC] * sin).astype(BF16)
    kvc_ref[...] = z[:, _A_KC:_A_KS]
    ck = cos[:, :LANES]
    sk = sin[:, :LANES]
    ks = z[:, _A_KS:_A_KS + 128] * ck + z[:, _A_KSR:_A_KW] * sk
    vs = z[:, _A_KS + 128:_A_KSR]
    kvs_ref[:, :128] = ks
    kvs_ref[:, 128:] = vs
    kvsb_ref[:, :128] = ks.astype(BF16)
    kvsb_ref[:, 128:] = vs.astype(BF16)
    kw = z[:, _A_KW:_A_KW + 128] * ck + z[:, _A_KWR:_A_G] * sk
    vw = z[:, _A_KW + 128:_A_KWR]
    kvw_ref[:, :128] = kw
    kvw_ref[:, 128:] = vw
    kvwb_ref[:, :128] = kw.astype(BF16)
    kvwb_ref[:, 128:] = vw.astype(BF16)
    gate_ref[...] = jax.nn.sigmoid(z[:, _A_G:_A_END])


def _rot_cols(w):
    d, n = w.shape
    w4 = w.reshape(d, n // HEAD_DIM, 2, HEAD_DIM // 2)
    return jnp.stack([-w4[:, :, 1], w4[:, :, 0]], axis=2).reshape(d, n)


def _pair_perm():
    idx = []
    for g in range(NSA_GROUP):
        for kvh in range(NSA_KV_HEADS):
            h = kvh * NSA_GROUP + g
            idx.extend(range(h * HEAD_DIM, (h + 1) * HEAD_DIM))
    return np.asarray(idx, np.int32)


def _attn_weight(w_in):
    offs = np.cumsum((0,) + IN_SPLITS)
    scale = HEAD_DIM ** -0.5 * math.log2(math.e)
    wq = w_in[:, offs[0]:offs[1]][:, _pair_perm()] * scale
    wkc = w_in[:, offs[1]:offs[2]]
    wks = w_in[:, offs[2]:offs[3]]
    wkw = w_in[:, offs[3]:offs[4]]
    wg = w_in[:, offs[4]:offs[5]]
    wg = jnp.pad(wg, ((0, 0), (0, LANES - wg.shape[1])))
    half = KV_WIDTH // 2
    w = jnp.concatenate([wq, _rot_cols(wq), wkc, wks, _rot_cols(wks[:, :half]), wkw, _rot_cols(wkw[:, :half]), wg],
                        axis=1)
    return w.astype(BF16)


def _rope_tables(pos):
    half = HEAD_DIM // 2
    inv = ROPE_THETA ** (-jnp.arange(half, dtype=F32) / half)
    ang = pos.astype(F32)[:, None] * inv[None, :]
    cos = jnp.tile(jnp.cos(ang), (1, 2 * NSA_HEADS))
    sin = jnp.tile(jnp.sin(ang), (1, 2 * NSA_HEADS))
    return cos, sin


def _attn_proj(x2, g, w_att, cos, sin):
    n = x2.shape[0]
    tm = _tile(n, TM_PROJ)
    n_tab = cos.shape[0] // tm
    row = lambda i: (i, 0)
    tab = lambda i: (i % n_tab, 0)
    const = lambda i: (0, 0)
    outs = [
        jax.ShapeDtypeStruct((n, NSA_WIDTH), BF16), jax.ShapeDtypeStruct((n, NSA_WIDTH), BF16),
        jax.ShapeDtypeStruct((n, KV_WIDTH), F32), jax.ShapeDtypeStruct((n, KV_WIDTH), F32),
        jax.ShapeDtypeStruct((n, KV_WIDTH), F32), jax.ShapeDtypeStruct((n, KV_WIDTH), BF16),
        jax.ShapeDtypeStruct((n, KV_WIDTH), BF16), jax.ShapeDtypeStruct((n, LANES), F32),
    ]
    return _pallas(
        _attn_proj_kernel,
        grid=(n // tm,),
        in_specs=[pl.BlockSpec((tm, D_MODEL), row), pl.BlockSpec((1, D_MODEL), const),
                  pl.BlockSpec((D_MODEL, _A_END), const),
                  pl.BlockSpec((tm, NSA_WIDTH), tab), pl.BlockSpec((tm, NSA_WIDTH), tab)],
        out_specs=[pl.BlockSpec((tm, o.shape[1]), row) for o in outs],
        out_shape=outs,
        compiler_params=_cparams(("parallel",)),
    )(x2, g, w_att, cos, sin)


def _norm_proj_kernel(x_ref, g_ref, w_ref, o_ref):
    h = _rms(x_ref[...], g_ref[...]).astype(BF16)
    o_ref[...] = _dot(h, w_ref[...])


def _norm_proj(x2, g, w):
    n = x2.shape[0]
    tm = _tile(n, TM_PROJ)
    width = w.shape[1]
    return _pallas(
        _norm_proj_kernel,
        grid=(n // tm,),
        in_specs=[pl.BlockSpec((tm, D_MODEL), lambda i: (i, 0)), pl.BlockSpec((1, D_MODEL), lambda i: (0, 0)),
                  pl.BlockSpec((D_MODEL, width), lambda i: (0, 0))],
        out_specs=pl.BlockSpec((tm, width), lambda i: (i, 0)),
        out_shape=jax.ShapeDtypeStruct((n, width), F32),
        compiler_params=_cparams(("parallel",)),
    )(x2, g, w)


def _compress_weights(cmp_pe, cmp_w1, cmp_b1, cmp_w2, cmp_b2):
    w1r = cmp_w1.reshape(2, 2, CMP_STRIDE, HEAD_DIM, CMP_HIDDEN)
    eye = jnp.eye(NSA_KV_HEADS, dtype=F32)
    w1bd = jnp.einsum('casdh,kj->cskdajh', w1r, eye).reshape(2, CMP_STRIDE * NSA_KV_HEADS * HEAD_DIM,
                                                             2 * NSA_KV_HEADS * CMP_HIDDEN)
    w2bd = jnp.einsum('chd,kj->ckhjd', cmp_w2, eye).reshape(2, NSA_KV_HEADS * CMP_HIDDEN, NSA_KV_HEADS * HEAD_DIM)
    pe8 = jnp.broadcast_to(cmp_pe.reshape(2, 1, CMP_BLOCK * HEAD_DIM), (2, 8, CMP_BLOCK * HEAD_DIM))
    b1t = jnp.tile(cmp_b1[:, None, :], (1, 1, NSA_KV_HEADS))
    b2t = jnp.tile(cmp_b2[:, None, :], (1, 1, NSA_KV_HEADS))
    return w1bd.astype(BF16), w2bd.astype(BF16), pe8, cmp_w1, b1t, b2t


def _compress_rows(rows_refs, n_ch, a_ref, w1bd_ref, w2bd_ref, pe8_ref, w1_ref, b1_ref, b2_ref, out_ref):
    for c in range(2):
        for s in range(CMP_STRIDE):
            a_ref[:, s * LANES:(s + 1) * LANES] = rows_refs[c][pl.ds(s, n_ch, stride=CMP_STRIDE), :].astype(BF16)
        hcat = _dot(a_ref[...], w1bd_ref[c])
        c0 = _dot(pe8_ref[c], w1_ref[c], precision=HIGHEST)[0:1, :]
        c0 = jnp.concatenate([c0, c0], axis=1) + b1_ref[c]
        h_b = pltpu.roll(hcat[:, 2 * LANES:], n_ch - 1, 0)
        hid = jax.nn.gelu(hcat[:, :2 * LANES] + h_b + c0)
        out_ref[:, c * LANES:(c + 1) * LANES] = _dot(hid.astype(BF16), w2bd_ref[c]) + b2_ref[c]


def _compress_prompt_kernel(k_ref, v_ref, w1bd_ref, w2bd_ref, pe8_ref, w1_ref, b1_ref, b2_ref, out_ref, a_ref, *, n_ch):
    _compress_rows((k_ref, v_ref), n_ch, a_ref, w1bd_ref, w2bd_ref, pe8_ref, w1_ref, b1_ref, b2_ref, out_ref)


def _cw_specs():
    z3 = lambda *a: (0, 0, 0)
    return [pl.BlockSpec((2, CMP_STRIDE * LANES, 4 * LANES), z3), pl.BlockSpec((2, 2 * LANES, LANES), z3),
            pl.BlockSpec((2, 8, CMP_BLOCK * HEAD_DIM), z3), pl.BlockSpec((2, CMP_BLOCK * HEAD_DIM, CMP_HIDDEN), z3),
            pl.BlockSpec((2, 1, 2 * LANES), z3), pl.BlockSpec((2, 1, LANES), z3)]


def _compress_prompt(kvc, b_sz, s_len, cw):
    n_ch = s_len // CMP_STRIDE
    return _pallas(
        functools.partial(_compress_prompt_kernel, n_ch=n_ch),
        grid=(b_sz,),
        in_specs=[pl.BlockSpec((s_len, LANES), lambda b: (b, 0)), pl.BlockSpec((s_len, LANES), lambda b: (b, 1))]
        + _cw_specs(),
        out_specs=pl.BlockSpec((n_ch, KV_WIDTH), lambda b: (b, 0)),
        out_shape=jax.ShapeDtypeStruct((b_sz * n_ch, KV_WIDTH), F32),
        scratch_shapes=[pltpu.VMEM((n_ch, CMP_STRIDE * LANES), BF16)],
        compiler_params=_cparams(("parallel",)),
    )(kvc, kvc, *cw)


def _cover_t(n_cmp_pad, n_cmp, n_sel, n_sel_pad):
    i = np.arange(n_cmp_pad)[None, :]
    j = np.arange(n_sel_pad)[:, None]
    start = i * CMP_STRIDE
    m = (start < (j + 1) * SEL_BLOCK) & (start + CMP_BLOCK > j * SEL_BLOCK) & (i < n_cmp) & (j < n_sel)
    return jnp.asarray(m.astype(np.float32))


def _select_t(imp_t, tp, n_sel):
    j = lax.broadcasted_iota(jnp.int32, imp_t.shape, 0)
    cur = tp // SEL_BLOCK
    valid = j * SEL_BLOCK <= tp
    forced = (j == 0) | (j == cur) | (j == cur - 1)
    score = jnp.where(valid, jnp.where(forced, BIG, imp_t), NEG)
    cnt = jnp.zeros(imp_t.shape, F32)
    for jp in range(n_sel):
        row = score[jp:jp + 1, :]
        cnt = cnt + jnp.where(j > jp, jnp.where(row >= score, 1.0, 0.0), jnp.where(row > score, 1.0, 0.0))
    return jnp.where((cnt < min(N_SEL, n_sel)) & valid, 1.0, 0.0)


def _online_step(s, ok, v, m, l, acc, v_t=False):
    if ok is None:
        m_new = jnp.maximum(m, jnp.max(s, axis=-1, keepdims=True))
        e = jnp.exp2(s - m_new)
    else:
        m_new = jnp.maximum(m, jnp.max(jnp.where(ok, s, NEG), axis=-1, keepdims=True))
        e = jnp.where(ok, jnp.exp2(s - m_new), 0.0)
    alpha = jnp.exp2(m - m_new)
    l = alpha * l + jnp.sum(e, axis=-1, keepdims=True)
    acc = alpha * acc + (_dot_nt if v_t else _dot)(e.astype(BF16), v)
    return m_new, l, acc


WIN_TILES = ((0, WINDOW + TQ),)
CHAIN_ROWS = 512


def _attend(state, s, v_t, mask):
    m_old, l_old, acc_old = state
    rows, tk = s.shape
    if mask is not None:
        s = jnp.where(mask[None], s.reshape(rows // TQ, TQ, tk), NEG).reshape(rows, tk)
    m_new = jnp.maximum(m_old, jnp.broadcast_to(jnp.max(s, axis=-1, keepdims=True), (rows, LANES)))
    e = jnp.concatenate([jnp.exp2(s[:, c:c + LANES] - m_new) for c in range(0, tk, LANES)], axis=1)
    ones = jnp.ones((tk, LANES), BF16)
    pv = _dot(e.astype(BF16), jnp.concatenate([v_t, ones], axis=1))
    alpha = jnp.exp2(m_old - m_new)
    return m_new, alpha * l_old + pv[:, LANES:], alpha * acc_old + pv[:, :LANES]


def _nsa_prompt_kernel(qu_ref, qr_ref, ckv_ref, ks_ref, vs_ref, kw_ref, vw_ref, gate_ref, covt_ref, eb_ref, o_ref,
                       qu_s, qa_s, m_s, l_s, acc_s, owin_s, *, n_sel):
    qi = pl.program_id(1)
    t0 = qi * TQ
    r = NSA_HEADS * TQ
    lane = lax.broadcasted_iota(jnp.int32, (TQ, LANES), 1)
    own = [lane < HEAD_DIM, lane >= HEAD_DIM]
    rpos = lax.broadcasted_iota(jnp.int32, (TQ, 1), 0)
    for g in range(NSA_GROUP):
        qug = qu_ref[:, g * LANES:(g + 1) * LANES]
        qrg = qr_ref[:, g * LANES:(g + 1) * LANES]
        for kvh in range(NSA_KV_HEADS):
            rows = pl.ds((g * NSA_KV_HEADS + kvh) * TQ, TQ)
            qu_s[rows, :] = jnp.where(own[kvh], qug, jnp.zeros_like(qug))
            qa_s[rows, :LANES] = jnp.where(own[kvh], qrg, jnp.zeros_like(qrg))

    for c0 in range(0, r, CHAIN_ROWS):
        cs = pl.ds(c0, CHAIN_ROWS)
        state = (jnp.full((CHAIN_ROWS, LANES), NEG, F32), jnp.zeros((CHAIN_ROWS, LANES), F32),
                 jnp.zeros((CHAIN_ROWS, LANES), F32))
        for off, tk in WIN_TILES:
            rows = pl.ds(pl.multiple_of(t0 + off, TQ), tk)
            kpos = t0 + off - WINDOW + lax.broadcasted_iota(jnp.int32, (1, tk), 1)
            mask = (kpos >= 0) & (kpos <= t0 + rpos) & (kpos > t0 + rpos - WINDOW)
            state = _attend(state, _dot_nt(qa_s[cs, :LANES], kw_ref[rows, :]), vw_ref[rows, :], mask)
        owin_s[cs, :] = state[2] / jnp.maximum(state[1], 1e-30)

    ckv = ckv_ref[...]
    n_cmp_pad = ckv.shape[0]
    c_ok = (lax.broadcasted_iota(jnp.int32, (1, n_cmp_pad), 1) * CMP_STRIDE + (CMP_BLOCK - 1)) <= (t0 + rpos)
    s = _dot_nt(qu_s[...], ckv[:, :LANES].astype(BF16)).reshape(NSA_HEADS, TQ, n_cmp_pad)
    p = _masked_softmax(s, c_ok[None])
    o_cmp = _dot(p.reshape(r, n_cmp_pad).astype(BF16), ckv[:, LANES:].astype(BF16))
    p4 = p.reshape(NSA_GROUP, NSA_KV_HEADS * TQ, n_cmp_pad)
    psum = (p4[0] + p4[1]) + (p4[2] + p4[3])
    imp_t = _dot_nt(covt_ref[...], psum, precision=HIGHEST)
    tp = t0 + (lax.broadcasted_iota(jnp.int32, (n_sel, NSA_KV_HEADS * TQ), 1) % TQ)
    sel_t = _select_t(imp_t[:n_sel], tp, n_sel)
    unsel_t = jnp.concatenate([1.0 - sel_t, jnp.zeros((LANES - n_sel, NSA_KV_HEADS * TQ), F32)], axis=0)
    unsel = unsel_t.T.astype(BF16)
    for g in range(NSA_GROUP):
        qa_s[pl.ds(g * NSA_KV_HEADS * TQ, NSA_KV_HEADS * TQ), LANES:] = unsel

    fresh = (jnp.full((r, LANES), NEG, F32), jnp.zeros((r, LANES), F32), jnp.zeros((r, LANES), F32))

    del fresh
    m_s[...] = jnp.full(m_s.shape, NEG, F32)
    l_s[...] = jnp.zeros(l_s.shape, F32)
    acc_s[...] = jnp.zeros(acc_s.shape, F32)

    def slc_tile(kt, causal):
        rows = pl.ds(pl.multiple_of(kt * TK_SLC, TK_SLC), TK_SLC)
        k_aug = jnp.concatenate([ks_ref[rows, :], eb_ref[rows, :]], axis=1)
        v_t = vs_ref[rows, :]
        mask = None
        if causal:
            mask = (kt * TK_SLC + lax.broadcasted_iota(jnp.int32, (1, TK_SLC), 1)) <= (t0 + rpos)
        chain = CHAIN_ROWS if causal else r
        for c0 in range(0, r, chain):
            cs = pl.ds(c0, chain)
            state = _attend((m_s[cs, :], l_s[cs, :], acc_s[cs, :]), _dot_nt(qa_s[cs, :], k_aug), v_t, mask)
            m_s[cs, :], l_s[cs, :], acc_s[cs, :] = state

    n_full = t0 // TK_SLC

    def slc_body(kt, c):
        slc_tile(kt, False)
        return c

    lax.fori_loop(0, n_full, slc_body, 0)
    slc_tile(n_full, True)
    o_slc = acc_s[...] / jnp.maximum(l_s[...], 1e-30)

    gt = gate_ref[...]
    o_win = owin_s[...]
    for g in range(NSA_GROUP):
        out = jnp.zeros((TQ, LANES), F32)
        for kvh in range(NSA_KV_HEADS):
            h = kvh * NSA_GROUP + g
            sl = slice((g * NSA_KV_HEADS + kvh) * TQ, (g * NSA_KV_HEADS + kvh + 1) * TQ)
            o = (gt[:, 3 * h:3 * h + 1] * o_cmp[sl] + gt[:, 3 * h + 1:3 * h + 2] * o_slc[sl]
                 + gt[:, 3 * h + 2:3 * h + 3] * o_win[sl])
            out = out + jnp.where(own[kvh], o, 0.0)
        o_ref[:, g * LANES:(g + 1) * LANES] = out.astype(BF16)


def _block_neg(n_keys):
    k = np.arange(n_keys)[:, None]
    j = np.arange(LANES)[None, :]
    return jnp.asarray(np.where(k // SEL_BLOCK == j, NEG, 0.0).astype(np.float32)).astype(BF16)


def _nsa_prompt(qu, qr, ckv, kvs_b, kvw_b, gates, b_sz, s_len):
    n_cmp_pad = s_len // CMP_STRIDE
    n_sel = s_len // SEL_BLOCK
    assert n_sel <= LANES and n_cmp_pad % LANES == 0 and s_len % TK_SLC == 0
    covt = _cover_t(n_cmp_pad, n_cmp_pad - 1, n_sel, LANES)
    eb = _block_neg(s_len)
    kvw_pad = jnp.pad(kvw_b.reshape(b_sz, s_len, KV_WIDTH), ((0, 0), (WINDOW, 0), (0, 0)))
    kvw_pad = kvw_pad.reshape(b_sz * (s_len + WINDOW), KV_WIDTH)
    nq = s_len // TQ
    r = NSA_HEADS * TQ
    tile = lambda b, q: (b * nq + q, 0)
    const = lambda b, q: (0, 0)
    return _pallas(
        functools.partial(_nsa_prompt_kernel, n_sel=n_sel),
        grid=(b_sz, nq),
        in_specs=[pl.BlockSpec((TQ, NSA_WIDTH), tile), pl.BlockSpec((TQ, NSA_WIDTH), tile),
                  pl.BlockSpec((n_cmp_pad, KV_WIDTH), lambda b, q: (b, 0)),
                  pl.BlockSpec((s_len, LANES), lambda b, q: (b, 0)), pl.BlockSpec((s_len, LANES), lambda b, q: (b, 1)),
                  pl.BlockSpec((s_len + WINDOW, LANES), lambda b, q: (b, 0)),
                  pl.BlockSpec((s_len + WINDOW, LANES), lambda b, q: (b, 1)),
                  pl.BlockSpec((TQ, LANES), tile),
                  pl.BlockSpec((LANES, n_cmp_pad), const), pl.BlockSpec((s_len, LANES), const)],
        out_specs=pl.BlockSpec((TQ, NSA_WIDTH), tile),
        out_shape=jax.ShapeDtypeStruct((b_sz * s_len, NSA_WIDTH), BF16),
        scratch_shapes=[pltpu.VMEM((r, LANES), BF16), pltpu.VMEM((r, 2 * LANES), BF16),
                        pltpu.VMEM((r, LANES), F32), pltpu.VMEM((r, LANES), F32), pltpu.VMEM((r, LANES), F32),
                        pltpu.VMEM((r, LANES), F32)],
        compiler_params=_cparams(("parallel", "arbitrary")),
    )(qu, qr, ckv, kvs_b, kvs_b, kvw_pad, kvw_pad, gates, covt, eb)


def _block_expand(n_sel_pad, n_keys):
    j = np.arange(n_sel_pad)[:, None]
    k = np.arange(n_keys)[None, :]
    return jnp.asarray((k // SEL_BLOCK == j).astype(np.float32)).astype(BF16)


def _hgrn_consts(c):
    n_lvl = int(math.log2(c))
    hc = HGRN_HEADS * c
    t = np.arange(hc) % c
    head = np.arange(hc) // c
    same_head = head[:, None] == head[None, :]
    u = t[None, :]
    mall = np.zeros(((1 + n_lvl) * hc, hc), np.float32)
    masks = np.zeros((n_lvl, hc, hc), np.float32)
    mall[:hc] = same_head & (u <= t[:, None])
    for li in range(n_lvl):
        n = c >> (li + 1)
        blk = t // (2 * n)
        up_start = blk * 2 * n + n
        upper = t >= up_start
        m_up = upper[:, None] & (u >= up_start[:, None]) & (u <= t[:, None])
        m_lo = (~upper)[:, None] & (u > t[:, None]) & (u < up_start[:, None])
        mall[(1 + li) * hc:(2 + li) * hc] = same_head & (m_up | m_lo)
        masks[li] = same_head & upper[:, None] & (~upper)[None, :] & (blk[:, None] == blk[None, :])
    return jnp.asarray(mall).astype(BF16), jnp.asarray(masks)


def _split2(x):
    a = x.astype(BF16)
    return a, (x - a.astype(F32)).astype(BF16)


def _lower_bound(lbl):
    e = jnp.exp(lbl - jnp.max(lbl, axis=0, keepdims=True))
    return e[0:1] / jnp.sum(e, axis=0, keepdims=True)


def _hgrn_prompt_kernel(hq_ref, hf_ref, hi_ref, hg_ref, lbl_ref, gn_ref, mall_ref, mask_ref, o_ref, s_ref, st_ref,
                        *, c, n_lvl, ts):
    si = pl.program_id(1)

    @pl.when(si == 0)
    def _():
        st_ref[...] = jnp.zeros_like(st_ref)

    hc = HGRN_HEADS * c
    lb_all = _lower_bound(lbl_ref[...])
    lb = jnp.concatenate([jnp.broadcast_to(lb_all[:, h * LANES:(h + 1) * LANES], (c, LANES))
                          for h in range(HGRN_HEADS)], axis=0)
    gn = gn_ref[...]
    rowi = lax.broadcasted_iota(jnp.int32, (hc, 1), 0)

    def chunk(ci, carry):
        r0 = pl.multiple_of(ci * c, c)

        def stacked(ref):
            return jnp.concatenate([ref[pl.ds(r0, c), h * LANES:(h + 1) * LANES] for h in range(HGRN_HEADS)], axis=0)

        q = jax.nn.silu(stacked(hq_ref)) * HGRN_DK ** -0.5
        f = lb + (1.0 - lb) * jax.nn.sigmoid(stacked(hf_ref))
        k = 1.0 - f
        v = stacked(hi_ref)
        a, b = _split2(jnp.log(f))
        sums = _dot(mall_ref[...], jnp.concatenate([a, b], axis=1))
        sums = sums[:, :LANES] + sums[:, LANES:]
        cum = sums[:hc]
        att = jnp.zeros((hc, hc), F32)
        for li in range(n_lvl):
            n = c >> (li + 1)
            e = jnp.exp(sums[(1 + li) * hc:(2 + li) * hc])
            zz = (jnp.where((rowi & n) != 0, q, k) * e).astype(BF16)
            att = att + _dot_nt(zz, zz) * mask_ref[li]
        o = _dot(att.astype(BF16), v.astype(BF16)) + jnp.sum(q * k, axis=-1, keepdims=True) * v
        qe = (q * jnp.exp(cum)).astype(BF16)
        o_heads = []
        for h in range(HGRN_HEADS):
            rows = slice(h * c, (h + 1) * c)
            st = st_ref[h]
            o_heads.append(o[rows] + _dot_nt(qe[rows], st.astype(BF16)))
            last = cum[(h + 1) * c - 1:(h + 1) * c, :]
            kd = (k[rows] * jnp.exp(last - cum[rows])).astype(BF16)
            st_ref[h] = st * jnp.exp(last) + _dot(v[rows].T.astype(BF16), kd)
        o = jnp.concatenate(o_heads, axis=0)
        y = o * lax.rsqrt(jnp.mean(o * o, axis=-1, keepdims=True) + NORM_EPS) * gn * jax.nn.silu(stacked(hg_ref))
        for h in range(HGRN_HEADS):
            o_ref[pl.ds(r0, c), h * LANES:(h + 1) * LANES] = y[h * c:(h + 1) * c].astype(BF16)
        return carry

    lax.fori_loop(0, ts // c, chunk, 0, unroll=2)

    @pl.when(si == pl.num_programs(1) - 1)
    def _():
        for h in range(HGRN_HEADS):
            s_ref[0, h] = st_ref[h].T


def _hgrn_prompt(zh, lb_logits, gnorm, b_sz, s_len):
    c = HG_CHUNK
    ts = min(HG_TS, s_len)
    ns = s_len // ts
    n_lvl = int(math.log2(c))
    mall, masks = _hgrn_consts(c)
    col = lambda k: (lambda b, s: (b * ns + s, k))
    return _pallas(
        functools.partial(_hgrn_prompt_kernel, c=c, n_lvl=n_lvl, ts=ts),
        grid=(b_sz, ns),
        in_specs=[pl.BlockSpec((ts, HGRN_KW), col(0)), pl.BlockSpec((ts, HGRN_KW), col(1)),
                  pl.BlockSpec((ts, HGRN_VW), col(2)), pl.BlockSpec((ts, HGRN_VW), col(3)),
                  pl.BlockSpec((lb_logits.shape[0], HGRN_KW), lambda b, s: (0, 0)),
                  pl.BlockSpec((1, LANES), lambda b, s: (0, 0)),
                  pl.BlockSpec(mall.shape, lambda b, s: (0, 0)),
                  pl.BlockSpec(masks.shape, lambda b, s: (0, 0, 0))],
        out_specs=[pl.BlockSpec((ts, HGRN_VW), lambda b, s: (b * ns + s, 0)),
                   pl.BlockSpec((1, HGRN_HEADS, HGRN_DK, HGRN_DV), lambda b, s: (b, 0, 0, 0))],
        out_shape=[jax.ShapeDtypeStruct((b_sz * s_len, HGRN_VW), BF16),
                   jax.ShapeDtypeStruct((b_sz, HGRN_HEADS, HGRN_DK, HGRN_DV), F32)],
        scratch_shapes=[pltpu.VMEM((HGRN_HEADS, HGRN_DV, HGRN_DK), F32)],
        compiler_params=_cparams(("parallel", "arbitrary")),
    )(zh, zh, zh, zh, lb_logits, gnorm, mall, masks)


_R_EXP0 = 8


def _post_kernel(x_ref, oa_ref, ob_ref, ma_ref, mb_ref, wpa_ref, wpb_ref, wo_ref, g2_ref, wr_ref, br_ref,
                 x1_ref, h2_ref, lg_ref):
    pa = _dot(oa_ref[...], wpa_ref[...])
    pb = _dot(ob_ref[...], wpb_ref[...])
    merged = jax.nn.sigmoid(ma_ref[...]) * pa + jax.nn.sigmoid(mb_ref[...]) * pb
    x1 = x_ref[...] + _dot(merged.astype(BF16), wo_ref[...])
    h2 = _rms(x1, g2_ref[...])
    x1_ref[...] = x1
    h2_ref[...] = h2
    h_hi = h2.astype(BF16)
    h_lo = (h2 - h_hi.astype(F32)).astype(BF16)
    lg = _dot(jnp.concatenate([h_hi, h_hi, h_lo], axis=1), wr_ref[...]) + br_ref[...]
    lg_ref[...] = lg.T


def _router_weights(w_rg, b_rg, w_re, b_re):
    wr = jnp.zeros((D_MODEL, LANES), F32)
    wr = wr.at[:, :N_GROUPS].set(w_rg)
    wr = wr.at[:, _R_EXP0:_R_EXP0 + N_EXPERTS].set(jnp.transpose(w_re, (1, 0, 2)).reshape(D_MODEL, N_EXPERTS))
    w_hi = wr.astype(BF16)
    w_lo = (wr - w_hi.astype(F32)).astype(BF16)
    br = jnp.zeros((1, LANES), F32)
    br = br.at[0, :N_GROUPS].set(b_rg)
    br = br.at[0, _R_EXP0:_R_EXP0 + N_EXPERTS].set(b_re.reshape(N_EXPERTS))
    return jnp.concatenate([w_hi, w_lo, w_hi], axis=0), br


def _post_mixer(x2, o_a, o_b, zm, wpa, wpb, wo, g2, wrt, br):
    n = x2.shape[0]
    tm = _tile(n, TM_POST)
    row = lambda i: (i, 0)
    const = lambda i: (0, 0)
    return _pallas(
        _post_kernel,
        grid=(n // tm,),
        in_specs=[pl.BlockSpec((tm, D_MODEL), row), pl.BlockSpec((tm, NSA_WIDTH), row),
                  pl.BlockSpec((tm, HGRN_VW), row),
                  pl.BlockSpec((tm, D_MODEL), lambda i: (i, 0)), pl.BlockSpec((tm, D_MODEL), lambda i: (i, 1)),
                  pl.BlockSpec((NSA_WIDTH, D_MODEL), const), pl.BlockSpec((HGRN_VW, D_MODEL), const),
                  pl.BlockSpec((D_MODEL, D_MODEL), const), pl.BlockSpec((1, D_MODEL), const),
                  pl.BlockSpec((3 * D_MODEL, LANES), const), pl.BlockSpec((1, LANES), const)],
        out_specs=[pl.BlockSpec((tm, D_MODEL), row), pl.BlockSpec((tm, D_MODEL), row),
                   pl.BlockSpec((LANES, tm), lambda i: (0, i))],
        out_shape=[jax.ShapeDtypeStruct((n, D_MODEL), F32), jax.ShapeDtypeStruct((n, D_MODEL), F32),
                   jax.ShapeDtypeStruct((LANES, n), F32)],
        compiler_params=_cparams(("parallel",)),
    )(x2, o_a, o_b, zm, zm, wpa, wpb, wo, g2, wrt, br)


def _lowest_argmax(vals, top):
    idx = jnp.full(top.shape, len(vals) - 1, jnp.int32)
    for i in range(len(vals) - 2, -1, -1):
        idx = jnp.where(vals[i] == top, i, idx)
    return idx


def _route_kernel(lg_ref, u_ref, eid_ref, gate_ref, rank_ref, cnt_ref, carry_ref):
    @pl.when(pl.program_id(0) == 0)
    def _():
        carry_ref[...] = jnp.zeros_like(carry_ref)

    lg = lg_ref[...]
    tr = lg.shape[1]
    grp = [lg[g:g + 1] for g in range(N_GROUPS)]
    mx = functools.reduce(jnp.maximum, grp)
    ex = [jnp.exp(v - mx) for v in grp]
    den = functools.reduce(lambda a, b: a + b, ex)
    pr = [e / den for e in ex]
    pg = functools.reduce(jnp.maximum, pr)
    gtop = _lowest_argmax(pr, pg)
    le = []
    for j in range(EXPERTS_PER_GROUP):
        v = lg[_R_EXP0 + j:_R_EXP0 + j + 1]
        for g in range(1, N_GROUPS):
            r = _R_EXP0 + g * EXPERTS_PER_GROUP + j
            v = jnp.where(gtop == g, lg[r:r + 1], v)
        le.append(v)
    m1 = functools.reduce(jnp.maximum, le)
    i1 = _lowest_argmax(le, m1)
    le2 = [jnp.where(i1 == j, -jnp.inf, le[j]) for j in range(EXPERTS_PER_GROUP)]
    m2 = functools.reduce(jnp.maximum, le2)
    i2 = _lowest_argmax(le2, m2)
    e2 = jnp.exp(m2 - m1)
    den2 = 1.0 + e2
    gate1 = (1.0 / den2) * pg
    gate2 = (e2 / den2) * pg
    eid1 = gtop * EXPERTS_PER_GROUP + i1
    eid2 = gtop * EXPERTS_PER_GROUP + i2

    eio = lax.broadcasted_iota(jnp.int32, (N_EXPERTS, tr), 0)
    oh1 = jnp.where(eio == eid1, 1.0, 0.0)
    oh2 = jnp.where(eio == eid2, 1.0, 0.0)
    both = (oh1 + oh2).astype(BF16)
    carry = carry_ref[...]
    before = _dot(both, u_ref[...]) + jnp.concatenate([carry] * (tr // LANES), axis=1)
    rank1 = jnp.sum(oh1 * before, axis=0, keepdims=True).astype(jnp.int32)
    rank2 = jnp.sum(oh2 * before, axis=0, keepdims=True).astype(jnp.int32)
    carry = carry + _dot(both, jnp.ones((tr, LANES), BF16))
    carry_ref[...] = carry
    cnt_ref[...] = carry
    zi = jnp.zeros((6, tr), jnp.int32)
    eid_ref[...] = jnp.concatenate([eid1, eid2, zi], axis=0)
    rank_ref[...] = jnp.concatenate([rank1, rank2, zi], axis=0)
    gate_ref[...] = jnp.concatenate([gate1, gate2, jnp.zeros((6, tr), F32)], axis=0)


def _route(lg):
    n = lg.shape[1]
    tr = _tile(n, TR_ROUTE, LANES)
    u = jnp.asarray(np.triu(np.ones((tr, tr), np.float32), 1)).astype(BF16)
    col = lambda i: (0, i)
    return _pallas(
        _route_kernel,
        grid=(n // tr,),
        in_specs=[pl.BlockSpec((LANES, tr), col), pl.BlockSpec((tr, tr), lambda i: (0, 0))],
        out_specs=[pl.BlockSpec((8, tr), col), pl.BlockSpec((8, tr), col), pl.BlockSpec((8, tr), col),
                   pl.BlockSpec((N_EXPERTS, LANES), lambda i: (0, 0))],
        out_shape=[jax.ShapeDtypeStruct((8, n), jnp.int32), jax.ShapeDtypeStruct((8, n), F32),
                   jax.ShapeDtypeStruct((8, n), jnp.int32), jax.ShapeDtypeStruct((N_EXPERTS, LANES), F32)],
        scratch_shapes=[pltpu.VMEM((N_EXPERTS, LANES), F32)],
        compiler_params=_cparams(("arbitrary",)),
    )(lg, u)


def _dest_kernel(ps_ref, eid_ref, rank_ref, dest_ref):
    eid = eid_ref[...]
    dest = rank_ref[...]
    for e in range(N_EXPERTS):
        dest = dest + jnp.where(eid == e, ps_ref[e], 0)
    dest_ref[...] = dest


def _dest_rows(pad_start, eid, rank):
    n = eid.shape[1]
    tr = _tile(n, 4096, LANES)
    col = lambda i: (0, i)
    return _pallas(
        _dest_kernel,
        grid=(n // tr,),
        in_specs=[pl.BlockSpec(memory_space=pltpu.SMEM), pl.BlockSpec((8, tr), col), pl.BlockSpec((8, tr), col)],
        out_specs=pl.BlockSpec((8, tr), col),
        out_shape=jax.ShapeDtypeStruct((8, n), jnp.int32),
        compiler_params=_cparams(("parallel",)),
    )(pad_start, eid, rank)


DMA_UNROLL = 8


def _row_loops(t_n, row_copy):
    def issue(t, c):
        for k in range(TOP_K_EXPERTS):
            row_copy(t, k).start()
        return c

    def wait(t, c):
        for k in range(TOP_K_EXPERTS):
            row_copy(t, k).wait()
        return c

    lax.fori_loop(0, t_n, issue, 0, unroll=DMA_UNROLL)
    lax.fori_loop(0, t_n, wait, 0, unroll=DMA_UNROLL)


def _dispatch_kernel(dest_ref, h_ref, xb_in_ref, xb_ref, sem):
    del xb_in_ref

    def row_copy(t, k):
        return pltpu.make_async_copy(h_ref.at[pl.ds(t, 1), :], xb_ref.at[pl.ds(dest_ref[k, t], 1), :], sem)

    _row_loops(h_ref.shape[0], row_copy)


def _smem_cols(t):
    return pl.BlockSpec((8, t), lambda i: (0, i), memory_space=pltpu.SMEM)


def _dispatch(dest, h2, rows_total):
    n = h2.shape[0]
    t = _tile(n, T_DISP, LANES)
    xb0 = jnp.zeros((rows_total, D_MODEL), F32)
    return _pallas(
        _dispatch_kernel,
        grid=(n // t,),
        in_specs=[_smem_cols(t), pl.BlockSpec((t, D_MODEL), lambda i: (i, 0)), pl.BlockSpec(memory_space=pl.ANY)],
        out_specs=pl.BlockSpec(memory_space=pl.ANY),
        out_shape=jax.ShapeDtypeStruct((rows_total, D_MODEL), F32),
        scratch_shapes=[pltpu.SemaphoreType.DMA(())],
        input_output_aliases={2: 0},
        compiler_params=_cparams(("arbitrary",)),
    )(dest, h2, xb0)


def _expert_kernel(blk_e_ref, x_ref, wg_ref, wu_ref, wd_ref, y_ref):
    del blk_e_ref
    x = x_ref[...].astype(BF16)
    hid = jax.nn.silu(_dot(x, wg_ref[0])) * _dot(x, wu_ref[0])
    y_ref[...] = _dot(hid.astype(BF16), wd_ref[0])


def _experts(blk_e, xb, wg, wu, wd):
    rows_total = xb.shape[0]
    wsel = lambda i, e: (e[i], 0, 0)
    grid_spec = pltpu.PrefetchScalarGridSpec(
        num_scalar_prefetch=1,
        grid=(rows_total // MOE_BM,),
        in_specs=[pl.BlockSpec((MOE_BM, D_MODEL), lambda i, e: (i, 0)),
                  pl.BlockSpec((1, D_MODEL, D_EXPERT), wsel), pl.BlockSpec((1, D_MODEL, D_EXPERT), wsel),
                  pl.BlockSpec((1, D_EXPERT, D_MODEL), wsel)],
        out_specs=pl.BlockSpec((MOE_BM, D_MODEL), lambda i, e: (i, 0)),
    )
    return _pallas(
        _expert_kernel,
        grid_spec=grid_spec,
        out_shape=jax.ShapeDtypeStruct((rows_total, D_MODEL), F32),
        compiler_params=_cparams(("arbitrary",)),
    )(blk_e, xb, wg, wu, wd)


def _combine_kernel(dest_ref, x1_ref, gate_ref, fn_ref, yb_ref, y_ref, buf_ref, sem):
    def row_copy(t, k):
        return pltpu.make_async_copy(yb_ref.at[pl.ds(dest_ref[k, t], 1), :], buf_ref.at[k, pl.ds(t, 1), :], sem)

    _row_loops(x1_ref.shape[0], row_copy)
    g = gate_ref[...]
    out = x1_ref[...] + g[:, 0:1] * buf_ref[0] + g[:, 1:2] * buf_ref[1]
    y_ref[...] = _rms(out, fn_ref[...])


def _combine(dest, x1, gate_t, fnorm, yb):
    n = x1.shape[0]
    t = _tile(n, T_DISP, LANES)
    return _pallas(
        _combine_kernel,
        grid=(n // t,),
        in_specs=[_smem_cols(t),
                  pl.BlockSpec((t, D_MODEL), lambda i: (i, 0)), pl.BlockSpec((t, 8), lambda i: (i, 0)),
                  pl.BlockSpec((1, D_MODEL), lambda i: (0, 0)), pl.BlockSpec(memory_space=pl.ANY)],
        out_specs=pl.BlockSpec((t, D_MODEL), lambda i: (i, 0)),
        out_shape=jax.ShapeDtypeStruct((n, D_MODEL), F32),
        scratch_shapes=[pltpu.VMEM((TOP_K_EXPERTS, t, D_MODEL), F32), pltpu.SemaphoreType.DMA(())],
        compiler_params=_cparams(("arbitrary",)),
    )(dest, x1, gate_t, fnorm, yb)


def _moe_and_final(x1, h2, lg, wg, wu, wd, fnorm):
    n = x1.shape[0]
    eid, gate, rank, cnt = _route(lg)
    counts = cnt[:, 0].astype(jnp.int32)
    padded = (counts + MOE_BM - 1) // MOE_BM * MOE_BM
    pad_end = jnp.cumsum(padded)
    pad_start = (pad_end - padded).astype(jnp.int32)
    n_blocks = -(-(n * TOP_K_EXPERTS + N_EXPERTS * (MOE_BM - 1)) // MOE_BM)
    blk_start = jnp.arange(n_blocks, dtype=jnp.int32) * MOE_BM
    blk_e = jnp.minimum(jnp.sum(pad_end[None, :] <= blk_start[:, None], axis=1), N_EXPERTS - 1).astype(jnp.int32)
    dest = _dest_rows(pad_start, eid, rank)
    xb = _dispatch(dest, h2, n_blocks * MOE_BM)
    yb = _experts(blk_e, xb, wg, wu, wd)
    return _combine(dest, x1, gate.T, fnorm, yb)


def _page_stream(pt_ref, pool_ref, buf_ref, sem, n_pages):
    b = pl.program_id(0)
    slot = b % 2

    def page_copy(bb, sl, p):
        col = pl.ds(pl.multiple_of(p * PAGE_SIZE, PAGE_SIZE), PAGE_SIZE)
        return pltpu.make_async_copy(pool_ref.at[pt_ref[bb, p]], buf_ref.at[sl, :, col], sem.at[sl])

    def issue(bb, sl):
        def body(p, c):
            page_copy(bb, sl, p).start()
            return c
        lax.fori_loop(0, n_pages, body, 0, unroll=DMA_UNROLL)

    def wait():
        def body(p, c):
            page_copy(b, slot, p).wait()
            return c
        lax.fori_loop(0, n_pages, body, 0, unroll=DMA_UNROLL)

    @pl.when(b == 0)
    def _():
        issue(0, 0)

    @pl.when(b + 1 < pl.num_programs(0))
    def _():
        issue(b + 1, 1 - slot)

    return slot, wait


TR_CHUNK = 1024


def _compress_sample_kernel(pt_ref, pool_ref, w1bd_ref, w2bd_ref, pe8_ref, w1_ref, b1_ref, b2_ref, out_ref,
                            buf_ref, kbuf_ref, vbuf_ref, a_ref, sem, *, n_pages, n_ch):
    slot, wait = _page_stream(pt_ref, pool_ref, buf_ref, sem, n_pages)
    wait()
    for c, dst in ((0, kbuf_ref), (1, vbuf_ref)):
        for j in range(0, n_pages * PAGE_SIZE, TR_CHUNK):
            dst[j:j + TR_CHUNK, :] = buf_ref[slot, c * LANES:(c + 1) * LANES, j:j + TR_CHUNK].T
    _compress_rows((kbuf_ref, vbuf_ref), n_ch, a_ref, w1bd_ref, w2bd_ref, pe8_ref, w1_ref, b1_ref, b2_ref, out_ref)


def _compress_sample(page_table, pool, cw):
    b_sz, n_pages = page_table.shape
    past = n_pages * PAGE_SIZE
    n_ch = past // CMP_STRIDE
    z3 = lambda b, pt: (0, 0, 0)
    specs = [pl.BlockSpec((2, CMP_STRIDE * LANES, 4 * LANES), z3), pl.BlockSpec((2, 2 * LANES, LANES), z3),
             pl.BlockSpec((2, 8, CMP_BLOCK * HEAD_DIM), z3), pl.BlockSpec((2, CMP_BLOCK * HEAD_DIM, CMP_HIDDEN), z3),
             pl.BlockSpec((2, 1, 2 * LANES), z3), pl.BlockSpec((2, 1, LANES), z3)]
    grid_spec = pltpu.PrefetchScalarGridSpec(
        num_scalar_prefetch=1,
        grid=(b_sz,),
        in_specs=[pl.BlockSpec(memory_space=pl.ANY)] + specs,
        out_specs=pl.BlockSpec((n_ch, KV_WIDTH), lambda b, pt: (b, 0)),
        scratch_shapes=[pltpu.VMEM((2, KV_WIDTH, past), F32), pltpu.VMEM((past, LANES), F32),
                        pltpu.VMEM((past, LANES), F32), pltpu.VMEM((n_ch, CMP_STRIDE * LANES), BF16),
                        pltpu.SemaphoreType.DMA((2,))],
    )
    return _pallas(
        functools.partial(_compress_sample_kernel, n_pages=n_pages, n_ch=n_ch),
        grid_spec=grid_spec,
        out_shape=jax.ShapeDtypeStruct((b_sz * n_ch, KV_WIDTH), F32),
        compiler_params=_cparams(("arbitrary",)),
    )(page_table, pool, *cw)


def _cmp_sample_kernel(qu_ref, ckv_ref, covt_ref, ocmp_ref, selt_ref, *, past, t_len, n_sel):
    r = qu_ref.shape[1]
    qu = qu_ref[0]
    ckv = ckv_ref[...]
    n_cmp_pad = ckv.shape[0]
    tpos = past + (lax.broadcasted_iota(jnp.int32, (r, 1), 0) % t_len)
    s = _dot_nt(qu, ckv[:, :LANES].astype(BF16))
    c_end = lax.broadcasted_iota(jnp.int32, (1, n_cmp_pad), 1) * CMP_STRIDE + (CMP_BLOCK - 1)
    p = _masked_softmax(s, c_end <= tpos)
    ocmp_ref[0] = _dot(p.astype(BF16), ckv[:, LANES:].astype(BF16))
    psum = []
    for kvh in range(NSA_KV_HEADS):
        acc = None
        for g in range(NSA_GROUP):
            blk = g * NSA_KV_HEADS + kvh
            pg = p[blk * t_len:(blk + 1) * t_len]
            acc = pg if acc is None else acc + pg
        psum.append(acc)
    psum = jnp.concatenate(psum, axis=0)
    imp_t = _dot_nt(covt_ref[...], psum, precision=HIGHEST)
    tp = past + (lax.broadcasted_iota(jnp.int32, imp_t.shape, 1) % t_len)
    selt_ref[0] = _select_t(imp_t, tp, n_sel)


def _cmp_sample(qu32, ckv, b_sz, past, t_len, n_sel, n_sel_pad):
    n_cmp_pad = past // CMP_STRIDE
    covt = _cover_t(n_cmp_pad, n_cmp_pad - 1, n_sel, n_sel_pad)
    r = qu32.shape[1]
    return _pallas(
        functools.partial(_cmp_sample_kernel, past=past, t_len=t_len, n_sel=n_sel),
        grid=(b_sz,),
        in_specs=[pl.BlockSpec((1, r, LANES), lambda b: (b, 0, 0)),
                  pl.BlockSpec((n_cmp_pad, KV_WIDTH), lambda b: (b, 0)),
                  pl.BlockSpec((n_sel_pad, n_cmp_pad), lambda b: (0, 0))],
        out_specs=[pl.BlockSpec((1, r, LANES), lambda b: (b, 0, 0)),
                   pl.BlockSpec((1, n_sel_pad, NSA_KV_HEADS * t_len), lambda b: (b, 0, 0))],
        out_shape=[jax.ShapeDtypeStruct((b_sz, r, LANES), F32),
                   jax.ShapeDtypeStruct((b_sz, n_sel_pad, NSA_KV_HEADS * t_len), F32)],
        compiler_params=_cparams(("parallel",)),
    )(qu32, ckv, covt)


TK_SAMPLE = 1024


def _slc_win_sample_kernel(pt_ref, qr_ref, qa_ref, sel_ref, e_ref, pool_ref, new_s_ref, win_ref, new_w_ref, ocmp_ref,
                           gate_ref, o_ref, buf_ref, sem, *, n_pages, past, t_len, win_buf):
    slot, wait_pages = _page_stream(pt_ref, pool_ref, buf_ref, sem, n_pages)
    qr = qr_ref[0]
    sel = sel_ref[0]
    r = qr.shape[0]
    tpos = past + (lax.broadcasted_iota(jnp.int32, (r, 1), 0) % t_len)
    init = (jnp.full((r, 1), NEG, F32), jnp.zeros((r, 1), F32), jnp.zeros((r, LANES), F32))
    new_i = lax.broadcasted_iota(jnp.int32, (1, LANES), 1)
    new_ok = ((past + new_i) <= tpos) & (new_i < t_len)

    wt = win_ref[0]
    wpos = past - win_buf + lax.broadcasted_iota(jnp.int32, (1, win_buf), 1)
    ok = (wpos >= 0) & (wpos <= tpos) & (wpos > tpos - WINDOW)
    carry = _online_step(_dot(qr, wt[:LANES].astype(BF16)), ok, wt[LANES:].astype(BF16), *init, v_t=True)
    rows = new_w_ref[0]
    ok = new_ok & ((past + new_i) > tpos - WINDOW)
    m, l, acc = _online_step(_dot_nt(qr, rows[:, :LANES].astype(BF16)), ok, rows[:, LANES:].astype(BF16), *carry)
    o_win = acc / jnp.maximum(l, 1e-30)

    wait_pages()

    qa = qa_ref[0]

    def slc_body(kt, carry):
        col = pl.ds(pl.multiple_of(kt * TK_SAMPLE, TK_SAMPLE), TK_SAMPLE)
        k_aug = jnp.concatenate([buf_ref[slot, :LANES, col].astype(BF16), e_ref[:, col]], axis=0)
        return _online_step(_dot(qa, k_aug), None, buf_ref[slot, LANES:, col].astype(BF16), *carry, v_t=True)

    carry = lax.fori_loop(0, past // TK_SAMPLE, slc_body, init, unroll=2)
    rows = new_s_ref[0]
    n_past_blk = past // SEL_BLOCK
    ok = new_ok & (sel[:, n_past_blk:n_past_blk + 1] > 0.5)
    m, l, acc = _online_step(_dot_nt(qr, rows[:, :LANES].astype(BF16)), ok, rows[:, LANES:].astype(BF16), *carry)
    o_slc = acc / jnp.maximum(l, 1e-30)

    g = gate_ref[0]
    o = g[:, 0:1] * ocmp_ref[0] + g[:, 1:2] * o_slc + g[:, 2:3] * o_win
    row = lax.broadcasted_iota(jnp.int32, (r, LANES), 0)
    lane = lax.broadcasted_iota(jnp.int32, (r, LANES), 1)
    o_ref[0] = jnp.where((((row // t_len) % 2) == 0) == (lane < HEAD_DIM), o, 0.0)


def _slc_win_sample(page_table, qr32, sel32, pool, new_s, win_t, new_w, ocmp32, gate32, past, t_len, n_sel_pad):
    b_sz, n_pages = page_table.shape
    win_buf = win_t.shape[2]
    r = qr32.shape[1]
    n_past_blk = past // SEL_BLOCK
    assert n_past_blk <= LANES
    unsel = jnp.pad(1.0 - sel32[:, :, :n_past_blk].astype(F32), ((0, 0), (0, 0), (0, LANES - n_past_blk)))
    qa32 = jnp.concatenate([qr32, unsel.astype(BF16)], axis=2)
    e_mat = _block_neg(past).T
    b3 = lambda b, pt: (b, 0, 0)
    grid_spec = pltpu.PrefetchScalarGridSpec(
        num_scalar_prefetch=1,
        grid=(b_sz,),
        in_specs=[pl.BlockSpec((1, r, LANES), b3), pl.BlockSpec((1, r, 2 * LANES), b3),
                  pl.BlockSpec((1, r, n_sel_pad), b3),
                  pl.BlockSpec((LANES, past), lambda b, pt: (0, 0)),
                  pl.BlockSpec(memory_space=pl.ANY),
                  pl.BlockSpec((1, LANES, KV_WIDTH), b3),
                  pl.BlockSpec((1, KV_WIDTH, win_buf), b3),
                  pl.BlockSpec((1, LANES, KV_WIDTH), b3),
                  pl.BlockSpec((1, r, LANES), b3), pl.BlockSpec((1, r, 3), b3)],
        out_specs=pl.BlockSpec((1, r, LANES), b3),
        scratch_shapes=[pltpu.VMEM((2, KV_WIDTH, past), F32), pltpu.SemaphoreType.DMA((2,))],
    )
    return _pallas(
        functools.partial(_slc_win_sample_kernel, n_pages=n_pages, past=past, t_len=t_len, win_buf=win_buf),
        grid_spec=grid_spec,
        out_shape=jax.ShapeDtypeStruct((b_sz, r, LANES), F32),
        compiler_params=_cparams(("arbitrary",)),
    )(page_table, qr32, qa32, sel32, e_mat, pool, new_s, win_t, new_w, ocmp32, gate32)


def _rows32(q, b_sz, t_len):
    q5 = q.reshape(b_sz, t_len, NSA_GROUP, NSA_KV_HEADS, HEAD_DIM).transpose(0, 2, 3, 1, 4)
    eye = jnp.eye(NSA_KV_HEADS, dtype=q.dtype)
    return jnp.einsum('bgktd,kj->bgktjd', q5, eye).reshape(b_sz, NSA_HEADS * t_len, LANES)


def _from_rows32(o32, b_sz, t_len):
    o6 = o32.reshape(b_sz, NSA_GROUP, NSA_KV_HEADS, t_len, NSA_KV_HEADS, HEAD_DIM)
    o5 = jnp.stack([o6[:, :, k, :, k] for k in range(NSA_KV_HEADS)], axis=2)
    return o5.transpose(0, 3, 1, 2, 4).reshape(b_sz * t_len, NSA_WIDTH)


def _nsa_sample(qu, qr, kvs_new, kvw_new, gates, cache_cmp, cache_slc, cache_win, page_table, cw, t_len):
    b_sz, n_pages = page_table.shape
    past = n_pages * PAGE_SIZE
    total = past + t_len
    assert t_len < CMP_STRIDE and t_len <= SEL_BLOCK and past % TK_SAMPLE == 0
    n_sel = -(-total // SEL_BLOCK)
    n_sel_pad = -(-n_sel // 8) * 8
    pool_c = cache_cmp.reshape(-1, PAGE_SIZE, KV_WIDTH).transpose(0, 2, 1)
    pool_s = cache_slc.reshape(-1, PAGE_SIZE, KV_WIDTH).transpose(0, 2, 1)
    win_t = cache_win.reshape(b_sz, cache_win.shape[1], KV_WIDTH).transpose(0, 2, 1)
    ckv = _compress_sample(page_table, pool_c, cw)
    ocmp32, sel_t = _cmp_sample(_rows32(qu, b_sz, t_len), ckv, b_sz, past, t_len, n_sel, n_sel_pad)
    sel = sel_t.transpose(0, 2, 1).reshape(b_sz, 1, NSA_KV_HEADS * t_len, n_sel_pad)
    sel32 = jnp.broadcast_to(sel, (b_sz, NSA_GROUP, NSA_KV_HEADS * t_len, n_sel_pad))
    sel32 = sel32.reshape(b_sz, NSA_HEADS * t_len, n_sel_pad).astype(BF16)
    pad_rows = lambda a: jnp.pad(a.reshape(b_sz, t_len, KV_WIDTH), ((0, 0), (0, LANES - t_len), (0, 0)))
    g4 = gates[:, :3 * NSA_HEADS].reshape(b_sz, t_len, NSA_KV_HEADS, NSA_GROUP, 3)
    gate32 = g4.transpose(0, 3, 2, 1, 4).reshape(b_sz, NSA_HEADS * t_len, 3)
    o32 = _slc_win_sample(page_table, _rows32(qr, b_sz, t_len), sel32, pool_s, pad_rows(kvs_new), win_t,
                          pad_rows(kvw_new), ocmp32, gate32, past, t_len, n_sel_pad)
    return _from_rows32(o32, b_sz, t_len).astype(BF16)


def _hgrn_sample_kernel(hq_ref, hf_ref, vt_ref, hgt_ref, s0_ref, lbl_ref, gnt_ref, ot_ref, s_ref, *, t_len):
    lb = _lower_bound(lbl_ref[...])
    q_all = jax.nn.silu(hq_ref[0]) * HGRN_DK ** -0.5
    f_all = lb + (1.0 - lb) * jax.nn.sigmoid(hf_ref[0])
    for h in range(HGRN_HEADS):
        hs = slice(h * LANES, (h + 1) * LANES)
        q = q_all[:, hs]
        f = f_all[:, hs]
        k = 1.0 - f
        vt = vt_ref[0, h]
        st = s0_ref[0, h].T
        cols = []
        for t in range(t_len):
            st = st * f[t:t + 1] + vt[:, t:t + 1] * k[t:t + 1]
            cols.append(jnp.sum(st * q[t:t + 1], axis=1, keepdims=True))
        ot = jnp.concatenate(cols, axis=1)
        y = ot * lax.rsqrt(jnp.mean(ot * ot, axis=0, keepdims=True) + NORM_EPS) * gnt_ref[...]
        ot_ref[0, h] = y * jax.nn.silu(hgt_ref[0, h])
        s_ref[0, h] = st.T


def _hgrn_sample(zh, s0, lb_logits, gnorm, b_sz, t_len):
    z3 = zh.reshape(b_sz, t_len, 4 * HGRN_KW)
    to_t = lambda a: a.reshape(b_sz, t_len, HGRN_HEADS, HGRN_DV).transpose(0, 2, 3, 1)
    vt = to_t(z3[:, :, 2 * HGRN_KW:3 * HGRN_KW])
    hgt = to_t(z3[:, :, 3 * HGRN_KW:])
    col = lambda k: (lambda b: (b, 0, k))
    b4 = lambda b: (b, 0, 0, 0)
    ot, s_new = _pallas(
        functools.partial(_hgrn_sample_kernel, t_len=t_len),
        grid=(b_sz,),
        in_specs=[pl.BlockSpec((1, t_len, HGRN_KW), col(0)), pl.BlockSpec((1, t_len, HGRN_KW), col(1)),
                  pl.BlockSpec((1, HGRN_HEADS, HGRN_DV, t_len), b4), pl.BlockSpec((1, HGRN_HEADS, HGRN_DV, t_len), b4),
                  pl.BlockSpec((1, HGRN_HEADS, HGRN_DK, HGRN_DV), b4),
                  pl.BlockSpec((lb_logits.shape[0], HGRN_KW), lambda b: (0, 0)),
                  pl.BlockSpec((HGRN_DV, 1), lambda b: (0, 0))],
        out_specs=[pl.BlockSpec((1, HGRN_HEADS, HGRN_DV, t_len), b4),
                   pl.BlockSpec((1, HGRN_HEADS, HGRN_DK, HGRN_DV), b4)],
        out_shape=[jax.ShapeDtypeStruct((b_sz, HGRN_HEADS, HGRN_DV, t_len), F32),
                   jax.ShapeDtypeStruct((b_sz, HGRN_HEADS, HGRN_DK, HGRN_DV), F32)],
        compiler_params=_cparams(("parallel",)),
    )(z3, z3, vt, hgt, s0, lb_logits, gnorm.reshape(HGRN_DV, 1))
    o_b = ot.transpose(0, 3, 1, 2).reshape(b_sz * t_len, HGRN_VW)
    return o_b.astype(BF16), s_new


def kernel(x_prompt, x_sample, cache_cmp_kv, cache_slc_kv, cache_win_kv, state_hgrn, page_table, norm1, w_in, cmp_pe,
           cmp_w1, cmp_b1, cmp_w2, cmp_b2, hgrn_lb_logits, hgrn_gnorm, w_proj_a, w_proj_b, w_out, norm2,
           w_router_group, b_router_group, w_router_expert, b_router_expert, w_exp_gate, w_exp_up, w_exp_down,
           final_norm):
    assert w_in.shape[0] == 1, "one layer"
    b_sz, s_len, _ = x_prompt.shape
    d_sz, t_len, _ = x_sample.shape
    past = page_table.shape[1] * PAGE_SIZE
    n_p = b_sz * s_len
    n_s = d_sz * t_len
    kv_row = (2, NSA_KV_HEADS, HEAD_DIM)

    offs = np.cumsum((0,) + IN_SPLITS)
    w_att = _attn_weight(w_in[0])
    w_h = w_in[0][:, offs[5]:offs[9]].astype(BF16)
    w_m = w_in[0][:, offs[9]:offs[11]].astype(BF16)
    g1 = norm1[0][None]
    cw = _compress_weights(cmp_pe[0], cmp_w1[0], cmp_b1[0], cmp_w2[0], cmp_b2[0])
    perm = _pair_perm()
    wpa = w_proj_a[0][perm].astype(BF16)
    wpb = w_proj_b[0].astype(BF16)
    wo = w_out[0].astype(BF16)
    wrt, br = _router_weights(w_router_group[0], b_router_group[0], w_router_expert[0], b_router_expert[0])
    gn = hgrn_gnorm[0][None]

    xp = x_prompt.reshape(n_p, D_MODEL)
    cos, sin = _rope_tables(jnp.arange(s_len))
    qu, qr, kvc_p, kvs_p, kvw_p, kvs_b, kvw_b, gates = _attn_proj(xp, g1, w_att, cos, sin)
    zh = _norm_proj(xp, g1, w_h)
    zm = _norm_proj(xp, g1, w_m)
    ckv = _compress_prompt(kvc_p, b_sz, s_len, cw)
    o_a = _nsa_prompt(qu, qr, ckv, kvs_b, kvw_b, gates, b_sz, s_len)
    o_b, hg_p = _hgrn_prompt(zh, hgrn_lb_logits, gn, b_sz, s_len)
    x1_p, h2_p, lg_p = _post_mixer(xp, o_a, o_b, zm, wpa, wpb, wo, norm2[0][None], wrt, br)

    xs = x_sample.reshape(n_s, D_MODEL)
    cos, sin = _rope_tables(past + jnp.tile(jnp.arange(t_len), d_sz))
    qu, qr, kvc_s, kvs_s, kvw_s, _, _, gates = _attn_proj(xs, g1, w_att, cos, sin)
    zh = _norm_proj(xs, g1, w_h)
    zm = _norm_proj(xs, g1, w_m)
    o_a = _nsa_sample(qu, qr, kvs_s, kvw_s, gates, cache_cmp_kv[0], cache_slc_kv[0], cache_win_kv[0], page_table, cw,
                      t_len)
    o_b, hg_s = _hgrn_sample(zh, state_hgrn[0], hgrn_lb_logits, hgrn_gnorm[0], d_sz, t_len)
    x1_s, h2_s, lg_s = _post_mixer(xs, o_a, o_b, zm, wpa, wpb, wo, norm2[0][None], wrt, br)

    w_exp = (w_exp_gate[0].astype(BF16), w_exp_up[0].astype(BF16), w_exp_down[0].astype(BF16))
    y_p = _moe_and_final(x1_p, h2_p, lg_p, *w_exp, final_norm[None]).reshape(b_sz, s_len, D_MODEL)
    y_s = _moe_and_final(x1_s, h2_s, lg_s, *w_exp, final_norm[None]).reshape(d_sz, t_len, D_MODEL)

    win_p = kvw_p.reshape((b_sz, s_len) + kv_row)[:, s_len - min(WINDOW, s_len):]
    win_rows = jnp.concatenate([cache_win_kv[0], kvw_s.reshape((d_sz, t_len) + kv_row)], axis=1)
    win_s = win_rows[:, win_rows.shape[1] - min(WINDOW, past + t_len):]
    return (y_p, y_s,
            kvc_p.reshape((1, b_sz, s_len) + kv_row), kvc_s.reshape((1, d_sz, t_len) + kv_row),
            kvs_p.reshape((1, b_sz, s_len) + kv_row), kvs_s.reshape((1, d_sz, t_len) + kv_row),
            win_p[None], win_s[None], hg_p[None], hg_s[None])
```

```python
import functools
import math

import numpy as np
import jax
import jax.numpy as jnp
from jax import lax
from jax.experimental import pallas as pl
from jax.experimental.pallas import tpu as pltpu

D_MODEL = 1024
PAGE_SIZE = 128
NSA_HEADS = 8
NSA_KV_HEADS = 2
NSA_GROUP = NSA_HEADS // NSA_KV_HEADS
HEAD_DIM = 64
CMP_BLOCK = 32
CMP_STRIDE = 16
CMP_HIDDEN = 2 * HEAD_DIM
SEL_BLOCK = 64
N_SEL = 16
WINDOW = 512
ROPE_THETA = 10000.0
HGRN_HEADS = 4
HGRN_DK = 128
HGRN_DV = 128
N_GROUPS = 4
EXPERTS_PER_GROUP = 8
N_EXPERTS = N_GROUPS * EXPERTS_PER_GROUP
TOP_K_EXPERTS = 2
D_EXPERT = 512
NSA_WIDTH = NSA_HEADS * HEAD_DIM
KV_WIDTH = 2 * NSA_KV_HEADS * HEAD_DIM
HGRN_KW = HGRN_HEADS * HGRN_DK
HGRN_VW = HGRN_HEADS * HGRN_DV
IN_SPLITS = (NSA_WIDTH, KV_WIDTH, KV_WIDTH, KV_WIDTH, 3 * NSA_HEADS, HGRN_KW, HGRN_KW, HGRN_VW, HGRN_VW,
             D_MODEL, D_MODEL)
NORM_EPS = 1e-6
NEG = -1e9
BIG = 1e9

LANES = 128
VMEM_LIMIT = 56 * 1024 * 1024
TM_PROJ = 512
TQ = 128
TK_SLC = 512
HG_CHUNK = 64
HG_TS = 256
HG_SEQS = 2
HG_UNROLL = 2
TM_POST = 256
TR_ROUTE = 512
MOE_BM = 256
T_DISP = 256

F32 = jnp.float32
BF16 = jnp.bfloat16
HIGHEST = lax.Precision.HIGHEST


def _dot(a, b, precision=None):
    return jnp.dot(a, b, preferred_element_type=F32, precision=precision)


def _dot_nt(a, b, precision=None):
    return lax.dot_general(a, b, (((1,), (1,)), ((), ())), preferred_element_type=F32, precision=precision)


def _tile(n, target, align=8):
    if n <= target:
        return n
    t = target - target % align
    while n % t:
        t -= align
    return t


def _pallas(body, **kw):
    fn = getattr(body, "func", body)
    return pl.pallas_call(body, name=fn.__name__.strip("_").removesuffix("_kernel"), **kw)


def _cparams(sem):
    return pltpu.CompilerParams(dimension_semantics=sem, vmem_limit_bytes=VMEM_LIMIT)


def _masked_softmax(s, ok):
    m = jnp.max(jnp.where(ok, s, NEG), axis=-1, keepdims=True)
    e = jnp.exp2(jnp.where(ok, s - m, NEG))
    return e / jnp.maximum(jnp.sum(e, axis=-1, keepdims=True), 1e-30)


def _rms(x, g):
    return x * lax.rsqrt(jnp.mean(x * x, axis=-1, keepdims=True) + NORM_EPS) * g


_A_Q, _A_QR, _A_KC, _A_KS, _A_KSR, _A_KW, _A_KWR, _A_G, _A_END = 0, 512, 1024, 1280, 1536, 1664, 1920, 2048, 2176


def _attn_proj_kernel(x_ref, g_ref, w_ref, cos_ref, sin_ref,
                      qu_ref, qr_ref, kvc_ref, kvs_ref, kvw_ref, kvsb_ref, kvwb_ref, gate_ref):
    h = _rms(x_ref[...], g_ref[...]).astype(BF16)
    z = _dot(h, w_ref[...])
    cos = cos_ref[...]
    sin = sin_ref[...]
    q = z[:, _A_Q:_A_QR]
    qu_ref[...] = q.astype(BF16)
    qr_ref[...] = (q * cos + z[:, _A_QR:_A_KC] * sin).astype(BF16)
    kvc_ref[...] = z[:, _A_KC:_A_KS]
    ck = cos[:, :LANES]
    sk = sin[:, :LANES]
    ks = z[:, _A_KS:_A_KS + 128] * ck + z[:, _A_KSR:_A_KW] * sk
    vs = z[:, _A_KS + 128:_A_KSR]
    kvs_ref[:, :128] = ks
    kvs_ref[:, 128:] = vs
    kvsb_ref[:, :128] = ks.astype(BF16)
    kvsb_ref[:, 128:] = vs.astype(BF16)
    kw = z[:, _A_KW:_A_KW + 128] * ck + z[:, _A_KWR:_A_G] * sk
    vw = z[:, _A_KW + 128:_A_KWR]
    kvw_ref[:, :128] = kw
    kvw_ref[:, 128:] = vw
    kvwb_ref[:, :128] = kw.astype(BF16)
    kvwb_ref[:, 128:] = vw.astype(BF16)
    gate_ref[...] = jax.nn.sigmoid(z[:, _A_G:_A_END])


def _rot_cols(w):
    d, n = w.shape
    w4 = w.reshape(d, n // HEAD_DIM, 2, HEAD_DIM // 2)
    return jnp.stack([-w4[:, :, 1], w4[:, :, 0]], axis=2).reshape(d, n)


def _pair_perm():
    idx = []
    for g in range(NSA_GROUP):
        for kvh in range(NSA_KV_HEADS):
            h = kvh * NSA_GROUP + g
            idx.extend(range(h * HEAD_DIM, (h + 1) * HEAD_DIM))
    return np.asarray(idx, np.int32)


def _attn_weight(w_in):
    offs = np.cumsum((0,) + IN_SPLITS)
    scale = HEAD_DIM ** -0.5 * math.log2(math.e)
    wq = w_in[:, offs[0]:offs[1]][:, _pair_perm()] * scale
    wkc = w_in[:, offs[1]:offs[2]]
    wks = w_in[:, offs[2]:offs[3]]
    wkw = w_in[:, offs[3]:offs[4]]
    wg = w_in[:, offs[4]:offs[5]]
    wg = jnp.pad(wg, ((0, 0), (0, LANES - wg.shape[1])))
    half = KV_WIDTH // 2
    w = jnp.concatenate([wq, _rot_cols(wq), wkc, wks, _rot_cols(wks[:, :half]), wkw, _rot_cols(wkw[:, :half]), wg],
                        axis=1)
    return w.astype(BF16)


def _rope_tables(pos):
    half = HEAD_DIM // 2
    inv = ROPE_THETA ** (-jnp.arange(half, dtype=F32) / half)
    ang = pos.astype(F32)[:, None] * inv[None, :]
    cos = jnp.tile(jnp.cos(ang), (1, 2 * NSA_HEADS))
    sin = jnp.tile(jnp.sin(ang), (1, 2 * NSA_HEADS))
    return cos, sin


def _attn_proj(x2, g, w_att, cos, sin):
    n = x2.shape[0]
    tm = _tile(n, TM_PROJ)
    n_tab = cos.shape[0] // tm
    row = lambda i: (i, 0)
    tab = lambda i: (i % n_tab, 0)
    const = lambda i: (0, 0)
    outs = [
        jax.ShapeDtypeStruct((n, NSA_WIDTH), BF16), jax.ShapeDtypeStruct((n, NSA_WIDTH), BF16),
        jax.ShapeDtypeStruct((n, KV_WIDTH), F32), jax.ShapeDtypeStruct((n, KV_WIDTH), F32),
        jax.ShapeDtypeStruct((n, KV_WIDTH), F32), jax.ShapeDtypeStruct((n, KV_WIDTH), BF16),
        jax.ShapeDtypeStruct((n, KV_WIDTH), BF16), jax.ShapeDtypeStruct((n, LANES), F32),
    ]
    return _pallas(
        _attn_proj_kernel,
        grid=(n // tm,),
        in_specs=[pl.BlockSpec((tm, D_MODEL), row), pl.BlockSpec((1, D_MODEL), const),
                  pl.BlockSpec((D_MODEL, _A_END), const),
                  pl.BlockSpec((tm, NSA_WIDTH), tab), pl.BlockSpec((tm, NSA_WIDTH), tab)],
        out_specs=[pl.BlockSpec((tm, o.shape[1]), row) for o in outs],
        out_shape=outs,
        compiler_params=_cparams(("parallel",)),
    )(x2, g, w_att, cos, sin)


def _norm_proj_kernel(x_ref, g_ref, w_ref, o_ref):
    h = _rms(x_ref[...], g_ref[...]).astype(BF16)
    o_ref[...] = _dot(h, w_ref[...])


def _norm_proj(x2, g, w):
    n = x2.shape[0]
    tm = _tile(n, TM_PROJ)
    width = w.shape[1]
    return _pallas(
        _norm_proj_kernel,
        grid=(n // tm,),
        in_specs=[pl.BlockSpec((tm, D_MODEL), lambda i: (i, 0)), pl.BlockSpec((1, D_MODEL), lambda i: (0, 0)),
                  pl.BlockSpec((D_MODEL, width), lambda i: (0, 0))],
        out_specs=pl.BlockSpec((tm, width), lambda i: (i, 0)),
        out_shape=jax.ShapeDtypeStruct((n, width), F32),
        compiler_params=_cparams(("parallel",)),
    )(x2, g, w)


def _compress_weights(cmp_pe, cmp_w1, cmp_b1, cmp_w2, cmp_b2):
    w1r = cmp_w1.reshape(2, 2, CMP_STRIDE, HEAD_DIM, CMP_HIDDEN)
    eye = jnp.eye(NSA_KV_HEADS, dtype=F32)
    w1bd = jnp.einsum('casdh,kj->cskdajh', w1r, eye).reshape(2, CMP_STRIDE * NSA_KV_HEADS * HEAD_DIM,
                                                             2 * NSA_KV_HEADS * CMP_HIDDEN)
    w2bd = jnp.einsum('chd,kj->ckhjd', cmp_w2, eye).reshape(2, NSA_KV_HEADS * CMP_HIDDEN, NSA_KV_HEADS * HEAD_DIM)
    pe8 = jnp.broadcast_to(cmp_pe.reshape(2, 1, CMP_BLOCK * HEAD_DIM), (2, 8, CMP_BLOCK * HEAD_DIM))
    b1t = jnp.tile(cmp_b1[:, None, :], (1, 1, NSA_KV_HEADS))
    b2t = jnp.tile(cmp_b2[:, None, :], (1, 1, NSA_KV_HEADS))
    c0 = _pallas(
        _compress_bias_kernel,
        out_shape=jax.ShapeDtypeStruct((2, 1, NSA_KV_HEADS * CMP_HIDDEN), F32),
    )(pe8, cmp_w1, b1t)
    return w1bd.astype(BF16), w2bd.astype(BF16), c0, b2t


def _compress_bias_kernel(pe8_ref, w1_ref, b1_ref, c0_ref):
    for c in range(2):
        c0 = _dot(pe8_ref[c], w1_ref[c], precision=HIGHEST)[0:1, :]
        c0_ref[c] = jnp.concatenate([c0] * NSA_KV_HEADS, axis=1) + b1_ref[c]


def _compress_rows(rows_refs, n_ch, a_ref, w1bd_ref, w2bd_ref, c0_ref, b2_ref, out_ref):
    for c in range(2):
        for s in range(CMP_STRIDE):
            a_ref[:, s * LANES:(s + 1) * LANES] = rows_refs[c][pl.ds(s, n_ch, stride=CMP_STRIDE), :].astype(BF16)
        _compress_mlp(c, a_ref, n_ch, w1bd_ref, w2bd_ref, c0_ref, b2_ref, out_ref)


def _compress_mlp(c, a_ref, n_ch, w1bd_ref, w2bd_ref, c0_ref, b2_ref, out_ref):
    hcat = _dot(a_ref[...], w1bd_ref[c])
    h_b = pltpu.roll(hcat[:, 2 * LANES:], n_ch - 1, 0)
    hid = jax.nn.gelu(hcat[:, :2 * LANES] + h_b + c0_ref[c])
    out_ref[:, c * LANES:(c + 1) * LANES] = _dot(hid.astype(BF16), w2bd_ref[c]) + b2_ref[c]


def _compress_prompt_kernel(k_ref, v_ref, w1bd_ref, w2bd_ref, c0_ref, b2_ref, out_ref, a_ref, *, n_ch):
    _compress_rows((k_ref, v_ref), n_ch, a_ref, w1bd_ref, w2bd_ref, c0_ref, b2_ref, out_ref)


def _cw_specs():
    z3 = lambda *a: (0, 0, 0)
    return [pl.BlockSpec((2, CMP_STRIDE * LANES, 4 * LANES), z3), pl.BlockSpec((2, 2 * LANES, LANES), z3),
            pl.BlockSpec((2, 1, 2 * LANES), z3), pl.BlockSpec((2, 1, LANES), z3)]


def _compress_prompt(kvc, b_sz, s_len, cw):
    n_ch = s_len // CMP_STRIDE
    return _pallas(
        functools.partial(_compress_prompt_kernel, n_ch=n_ch),
        grid=(b_sz,),
        in_specs=[pl.BlockSpec((s_len, LANES), lambda b: (b, 0)), pl.BlockSpec((s_len, LANES), lambda b: (b, 1))]
        + _cw_specs(),
        out_specs=pl.BlockSpec((n_ch, KV_WIDTH), lambda b: (b, 0)),
        out_shape=jax.ShapeDtypeStruct((b_sz * n_ch, KV_WIDTH), F32),
        scratch_shapes=[pltpu.VMEM((n_ch, CMP_STRIDE * LANES), BF16)],
        compiler_params=_cparams(("parallel",)),
    )(kvc, kvc, *cw)


def _cover_t(n_cmp_pad, n_cmp, n_sel, n_sel_pad):
    i = np.arange(n_cmp_pad)[None, :]
    j = np.arange(n_sel_pad)[:, None]
    start = i * CMP_STRIDE
    m = (start < (j + 1) * SEL_BLOCK) & (start + CMP_BLOCK > j * SEL_BLOCK) & (i < n_cmp) & (j < n_sel)
    return jnp.asarray(m.astype(np.float32))


def _select_t(imp_t, tp, n_sel):
    j = lax.broadcasted_iota(jnp.int32, imp_t.shape, 0)
    cur = tp // SEL_BLOCK
    valid = j * SEL_BLOCK <= tp
    forced = (j == 0) | (j == cur) | (j == cur - 1)
    score = jnp.where(valid, jnp.where(forced, BIG, imp_t), NEG)
    cnt = jnp.zeros(imp_t.shape, F32)
    for jp in range(n_sel):
        row = score[jp:jp + 1, :]
        cnt = cnt + jnp.where(j > jp, jnp.where(row >= score, 1.0, 0.0), jnp.where(row > score, 1.0, 0.0))
    return jnp.where((cnt < min(N_SEL, n_sel)) & valid, 1.0, 0.0)


def _online_step(s, ok, v, m, l, acc, v_t=False):
    if ok is None:
        m_new = jnp.maximum(m, jnp.max(s, axis=-1, keepdims=True))
        e = jnp.exp2(s - m_new)
    else:
        m_new = jnp.maximum(m, jnp.max(jnp.where(ok, s, NEG), axis=-1, keepdims=True))
        e = jnp.where(ok, jnp.exp2(s - m_new), 0.0)
    alpha = jnp.exp2(m - m_new)
    l = alpha * l + jnp.sum(e, axis=-1, keepdims=True)
    acc = alpha * acc + (_dot_nt if v_t else _dot)(e.astype(BF16), v)
    return m_new, l, acc


WIN_TILES = ((0, WINDOW + TQ),)
CHAIN_ROWS = 512


def _attend(state, s, v_t, mask):
    m_old, l_old, acc_old = state
    rows, tk = s.shape
    if mask is not None:
        s = jnp.where(mask[None], s.reshape(rows // TQ, TQ, tk), NEG).reshape(rows, tk)
    m_new = jnp.maximum(m_old, jnp.broadcast_to(jnp.max(s, axis=-1, keepdims=True), (rows, LANES)))
    e = jnp.concatenate([jnp.exp2(s[:, c:c + LANES] - m_new) for c in range(0, tk, LANES)], axis=1)
    ones = jnp.ones((tk, LANES), BF16)
    pv = _dot(e.astype(BF16), jnp.concatenate([v_t, ones], axis=1))
    alpha = jnp.exp2(m_old - m_new)
    return m_new, alpha * l_old + pv[:, LANES:], alpha * acc_old + pv[:, :LANES]


def _nsa_prompt_kernel(qu_ref, qr_ref, ckv_ref, ks_ref, vs_ref, kw_ref, vw_ref, gate_ref, covt_ref, eb_ref, ge_ref,
                       o_ref, qu_s, qa_s, m_s, l_s, acc_s, owin_s, *, n_sel):
    qi = pl.program_id(1)
    t0 = qi * TQ
    r = NSA_HEADS * TQ
    lane = lax.broadcasted_iota(jnp.int32, (TQ, LANES), 1)
    own = [lane < HEAD_DIM, lane >= HEAD_DIM]
    rpos = lax.broadcasted_iota(jnp.int32, (TQ, 1), 0)
    for g in range(NSA_GROUP):
        qug = qu_ref[:, g * LANES:(g + 1) * LANES]
        qrg = qr_ref[:, g * LANES:(g + 1) * LANES]
        for kvh in range(NSA_KV_HEADS):
            rows = pl.ds((g * NSA_KV_HEADS + kvh) * TQ, TQ)
            qu_s[rows, :] = jnp.where(own[kvh], qug, jnp.zeros_like(qug))
            qa_s[rows, :LANES] = jnp.where(own[kvh], qrg, jnp.zeros_like(qrg))

    for c0 in range(0, r, CHAIN_ROWS):
        cs = pl.ds(c0, CHAIN_ROWS)
        state = (jnp.full((CHAIN_ROWS, LANES), NEG, F32), jnp.zeros((CHAIN_ROWS, LANES), F32),
                 jnp.zeros((CHAIN_ROWS, LANES), F32))
        for off, tk in WIN_TILES:
            rows = pl.ds(pl.multiple_of(t0 + off, TQ), tk)
            kpos = t0 + off - WINDOW + lax.broadcasted_iota(jnp.int32, (1, tk), 1)
            mask = (kpos >= 0) & (kpos <= t0 + rpos) & (kpos > t0 + rpos - WINDOW)
            state = _attend(state, _dot_nt(qa_s[cs, :LANES], kw_ref[rows, :]), vw_ref[rows, :], mask)
        owin_s[cs, :] = state[2] / jnp.maximum(state[1], 1e-30)

    ckv = ckv_ref[...]
    n_cmp_pad = ckv.shape[0]
    c_ok = (lax.broadcasted_iota(jnp.int32, (1, n_cmp_pad), 1) * CMP_STRIDE + (CMP_BLOCK - 1)) <= (t0 + rpos)
    s = _dot_nt(qu_s[...], ckv[:, :LANES].astype(BF16)).reshape(NSA_HEADS, TQ, n_cmp_pad)
    p = _masked_softmax(s, c_ok[None])
    o_cmp = _dot(p.reshape(r, n_cmp_pad).astype(BF16), ckv[:, LANES:].astype(BF16))
    p4 = p.reshape(NSA_GROUP, NSA_KV_HEADS * TQ, n_cmp_pad)
    psum = (p4[0] + p4[1]) + (p4[2] + p4[3])
    imp_t = _dot_nt(covt_ref[...], psum, precision=HIGHEST)
    tp = t0 + (lax.broadcasted_iota(jnp.int32, (n_sel, NSA_KV_HEADS * TQ), 1) % TQ)
    sel_t = _select_t(imp_t[:n_sel], tp, n_sel)
    unsel_t = jnp.concatenate([1.0 - sel_t, jnp.zeros((LANES - n_sel, NSA_KV_HEADS * TQ), F32)], axis=0)
    unsel = unsel_t.T.astype(BF16)
    for g in range(NSA_GROUP):
        qa_s[pl.ds(g * NSA_KV_HEADS * TQ, NSA_KV_HEADS * TQ), LANES:] = unsel

    fresh = (jnp.full((r, LANES), NEG, F32), jnp.zeros((r, LANES), F32), jnp.zeros((r, LANES), F32))

    del fresh
    m_s[...] = jnp.full(m_s.shape, NEG, F32)
    l_s[...] = jnp.zeros(l_s.shape, F32)
    acc_s[...] = jnp.zeros(acc_s.shape, F32)

    def slc_tile(kt, causal):
        rows = pl.ds(pl.multiple_of(kt * TK_SLC, TK_SLC), TK_SLC)
        k_aug = jnp.concatenate([ks_ref[rows, :], eb_ref[rows, :]], axis=1)
        v_t = vs_ref[rows, :]
        mask = None
        if causal:
            mask = (kt * TK_SLC + lax.broadcasted_iota(jnp.int32, (1, TK_SLC), 1)) <= (t0 + rpos)
        chain = CHAIN_ROWS if causal else r
        for c0 in range(0, r, chain):
            cs = pl.ds(c0, chain)
            state = _attend((m_s[cs, :], l_s[cs, :], acc_s[cs, :]), _dot_nt(qa_s[cs, :], k_aug), v_t, mask)
            m_s[cs, :], l_s[cs, :], acc_s[cs, :] = state

    n_full = t0 // TK_SLC

    def slc_body(kt, c):
        slc_tile(kt, False)
        return c

    lax.fori_loop(0, n_full, slc_body, 0)
    slc_tile(n_full, True)
    o_slc = acc_s[...] / jnp.maximum(l_s[...], 1e-30)

    gt = gate_ref[...]
    g_hi = gt.astype(BF16)
    g_lo = (gt - g_hi.astype(F32)).astype(BF16)
    gexp = _dot(jnp.concatenate([g_hi, g_lo], axis=1), ge_ref[...])
    o_win = owin_s[...]
    for g in range(NSA_GROUP):
        lo = slice((g * NSA_KV_HEADS) * TQ, (g * NSA_KV_HEADS + 1) * TQ)
        hi = slice((g * NSA_KV_HEADS + 1) * TQ, (g * NSA_KV_HEADS + 2) * TQ)
        out = None
        for i, o_i in enumerate((o_cmp, o_slc, o_win)):
            col = (i * NSA_GROUP + g) * LANES
            term = gexp[:, col:col + LANES] * jnp.where(own[0], o_i[lo], o_i[hi])
            out = term if out is None else out + term
        o_ref[:, g * LANES:(g + 1) * LANES] = out.astype(BF16)


def _gate_expand():
    ge = np.zeros((LANES, 3, NSA_GROUP, NSA_KV_HEADS, HEAD_DIM), np.float32)
    for kvh in range(NSA_KV_HEADS):
        for g in range(NSA_GROUP):
            for i in range(3):
                ge[3 * (kvh * NSA_GROUP + g) + i, i, g, kvh, :] = 1.0
    ge = ge.reshape(LANES, 3 * NSA_WIDTH)
    return jnp.asarray(np.concatenate([ge, ge], axis=0)).astype(BF16)


def _block_neg(n_keys):
    k = np.arange(n_keys)[:, None]
    j = np.arange(LANES)[None, :]
    return jnp.asarray(np.where(k // SEL_BLOCK == j, NEG, 0.0).astype(np.float32)).astype(BF16)


def _nsa_prompt(qu, qr, ckv, kvs_b, kvw_b, gates, b_sz, s_len):
    n_cmp_pad = s_len // CMP_STRIDE
    n_sel = s_len // SEL_BLOCK
    assert n_sel <= LANES and n_cmp_pad % LANES == 0 and s_len % TK_SLC == 0
    covt = _cover_t(n_cmp_pad, n_cmp_pad - 1, n_sel, LANES)
    eb = _block_neg(s_len)
    kvw_pad = jnp.pad(kvw_b.reshape(b_sz, s_len, KV_WIDTH), ((0, 0), (WINDOW, 0), (0, 0)))
    kvw_pad = kvw_pad.reshape(b_sz * (s_len + WINDOW), KV_WIDTH)
    nq = s_len // TQ
    r = NSA_HEADS * TQ
    tile = lambda b, q: (b * nq + q, 0)
    const = lambda b, q: (0, 0)
    return _pallas(
        functools.partial(_nsa_prompt_kernel, n_sel=n_sel),
        grid=(b_sz, nq),
        in_specs=[pl.BlockSpec((TQ, NSA_WIDTH), tile), pl.BlockSpec((TQ, NSA_WIDTH), tile),
                  pl.BlockSpec((n_cmp_pad, KV_WIDTH), lambda b, q: (b, 0)),
                  pl.BlockSpec((s_len, LANES), lambda b, q: (b, 0)), pl.BlockSpec((s_len, LANES), lambda b, q: (b, 1)),
                  pl.BlockSpec((s_len + WINDOW, LANES), lambda b, q: (b, 0)),
                  pl.BlockSpec((s_len + WINDOW, LANES), lambda b, q: (b, 1)),
                  pl.BlockSpec((TQ, LANES), tile),
                  pl.BlockSpec((LANES, n_cmp_pad), const), pl.BlockSpec((s_len, LANES), const),
                  pl.BlockSpec((2 * LANES, 3 * NSA_WIDTH), const)],
        out_specs=pl.BlockSpec((TQ, NSA_WIDTH), tile),
        out_shape=jax.ShapeDtypeStruct((b_sz * s_len, NSA_WIDTH), BF16),
        scratch_shapes=[pltpu.VMEM((r, LANES), BF16), pltpu.VMEM((r, 2 * LANES), BF16),
                        pltpu.VMEM((r, LANES), F32), pltpu.VMEM((r, LANES), F32), pltpu.VMEM((r, LANES), F32),
                        pltpu.VMEM((r, LANES), F32)],
        compiler_params=_cparams(("parallel", "arbitrary")),
    )(qu, qr, ckv, kvs_b, kvs_b, kvw_pad, kvw_pad, gates, covt, eb, _gate_expand())


def _block_expand(n_sel_pad, n_keys):
    j = np.arange(n_sel_pad)[:, None]
    k = np.arange(n_keys)[None, :]
    return jnp.asarray((k // SEL_BLOCK == j).astype(np.float32)).astype(BF16)


def _hgrn_consts(c):
    n_lvl = int(math.log2(c))
    hc = HGRN_HEADS * c
    t = np.arange(hc) % c
    head = np.arange(hc) // c
    same_head = head[:, None] == head[None, :]
    u = t[None, :]
    mall = np.zeros(((1 + n_lvl) * hc, hc), np.float32)
    masks = np.zeros((n_lvl, hc, hc), np.float32)
    mall[:hc] = same_head & (u <= t[:, None])
    for li in range(n_lvl):
        n = c >> (li + 1)
        blk = t // (2 * n)
        up_start = blk * 2 * n + n
        upper = t >= up_start
        m_up = upper[:, None] & (u >= up_start[:, None]) & (u <= t[:, None])
        m_lo = (~upper)[:, None] & (u > t[:, None]) & (u < up_start[:, None])
        mall[(1 + li) * hc:(2 + li) * hc] = same_head & (m_up | m_lo)
        masks[li] = same_head & upper[:, None] & (~upper)[None, :] & (blk[:, None] == blk[None, :])
    return jnp.asarray(mall).astype(BF16), jnp.asarray(masks)


def _split2(x):
    a = x.astype(BF16)
    return a, (x - a.astype(F32)).astype(BF16)


def _lower_bound(lbl):
    e = jnp.exp(lbl - jnp.max(lbl, axis=0, keepdims=True))
    return e[0:1] / jnp.sum(e, axis=0, keepdims=True)


def _hgrn_prompt_kernel(hq_ref, hf_ref, hi_ref, hg_ref, lbl_ref, gn_ref, mall_ref, mask_ref, o_ref, s_ref, st_ref,
                        *, c, n_lvl, ts, n_seq):
    si = pl.program_id(1)

    @pl.when(si == 0)
    def _():
        st_ref[...] = jnp.zeros_like(st_ref)

    hc = HGRN_HEADS * c
    lb_all = _lower_bound(lbl_ref[...])
    lb = jnp.concatenate([jnp.broadcast_to(lb_all[:, h * LANES:(h + 1) * LANES], (c, LANES))
                          for h in range(HGRN_HEADS)], axis=0)
    gn = gn_ref[...]
    rowi = lax.broadcasted_iota(jnp.int32, (hc, 1), 0)

    def chunk(ci, carry):
        for sq in range(n_seq):
            seq_chunk(sq, pl.multiple_of(ci * c, c))
        return carry

    def seq_chunk(sq, r0):
        def stacked(ref):
            return jnp.concatenate([ref[sq, pl.ds(r0, c), h * LANES:(h + 1) * LANES] for h in range(HGRN_HEADS)],
                                   axis=0)

        q = jax.nn.silu(stacked(hq_ref)) * HGRN_DK ** -0.5
        f = lb + (1.0 - lb) * jax.nn.sigmoid(stacked(hf_ref))
        k = 1.0 - f
        v = stacked(hi_ref)
        a, b = _split2(jnp.log(f))
        sums = _dot(mall_ref[...], jnp.concatenate([a, b], axis=1))
        sums = sums[:, :LANES] + sums[:, LANES:]
        cum = sums[:hc]
        att = jnp.zeros((hc, hc), F32)
        for li in range(n_lvl):
            n = c >> (li + 1)
            e = jnp.exp(sums[(1 + li) * hc:(2 + li) * hc])
            zz = (jnp.where((rowi & n) != 0, q, k) * e).astype(BF16)
            att = att + _dot_nt(zz, zz) * mask_ref[li]
        o = _dot(att.astype(BF16), v.astype(BF16)) + jnp.sum(q * k, axis=-1, keepdims=True) * v
        qe = (q * jnp.exp(cum)).astype(BF16)
        o_heads = []
        for h in range(HGRN_HEADS):
            rows = slice(h * c, (h + 1) * c)
            st = st_ref[sq, h]
            o_heads.append(o[rows] + _dot_nt(qe[rows], st.astype(BF16)))
            last = cum[(h + 1) * c - 1:(h + 1) * c, :]
            kd = (k[rows] * jnp.exp(last - cum[rows])).astype(BF16)
            st_ref[sq, h] = st * jnp.exp(last) + _dot(v[rows].T.astype(BF16), kd)
        o = jnp.concatenate(o_heads, axis=0)
        y = o * lax.rsqrt(jnp.mean(o * o, axis=-1, keepdims=True) + NORM_EPS) * gn * jax.nn.silu(stacked(hg_ref))
        for h in range(HGRN_HEADS):
            o_ref[sq, pl.ds(r0, c), h * LANES:(h + 1) * LANES] = y[h * c:(h + 1) * c].astype(BF16)

    lax.fori_loop(0, ts // c, chunk, 0, unroll=HG_UNROLL)

    @pl.when(si == pl.num_programs(1) - 1)
    def _():
        for sq in range(n_seq):
            for h in range(HGRN_HEADS):
                s_ref[sq, h] = st_ref[sq, h].T


def _hgrn_prompt(zh, lb_logits, gnorm, b_sz, s_len):
    c = HG_CHUNK
    ts = min(HG_TS, s_len)
    ns = s_len // ts
    n_lvl = int(math.log2(c))
    mall, masks = _hgrn_consts(c)
    n_seq = HG_SEQS if b_sz % HG_SEQS == 0 else 1
    z3 = zh.reshape(b_sz, s_len, zh.shape[1])
    col = lambda k: (lambda b, s: (b, s, k))
    o_b, s_fin = _pallas(
        functools.partial(_hgrn_prompt_kernel, c=c, n_lvl=n_lvl, ts=ts, n_seq=n_seq),
        grid=(b_sz // n_seq, ns),
        in_specs=[pl.BlockSpec((n_seq, ts, HGRN_KW), col(0)), pl.BlockSpec((n_seq, ts, HGRN_KW), col(1)),
                  pl.BlockSpec((n_seq, ts, HGRN_VW), col(2)), pl.BlockSpec((n_seq, ts, HGRN_VW), col(3)),
                  pl.BlockSpec((lb_logits.shape[0], HGRN_KW), lambda b, s: (0, 0)),
                  pl.BlockSpec((1, LANES), lambda b, s: (0, 0)),
                  pl.BlockSpec(mall.shape, lambda b, s: (0, 0)),
                  pl.BlockSpec(masks.shape, lambda b, s: (0, 0, 0))],
        out_specs=[pl.BlockSpec((n_seq, ts, HGRN_VW), lambda b, s: (b, s, 0)),
                   pl.BlockSpec((n_seq, HGRN_HEADS, HGRN_DK, HGRN_DV), lambda b, s: (b, 0, 0, 0))],
        out_shape=[jax.ShapeDtypeStruct((b_sz, s_len, HGRN_VW), BF16),
                   jax.ShapeDtypeStruct((b_sz, HGRN_HEADS, HGRN_DK, HGRN_DV), F32)],
        scratch_shapes=[pltpu.VMEM((n_seq, HGRN_HEADS, HGRN_DV, HGRN_DK), F32)],
        compiler_params=_cparams(("parallel", "arbitrary")),
    )(z3, z3, z3, z3, lb_logits, gnorm, mall, masks)
    return o_b.reshape(b_sz * s_len, HGRN_VW), s_fin


_R_EXP0 = 8


def _post_kernel(x_ref, oa_ref, ob_ref, ma_ref, mb_ref, wpa_ref, wpb_ref, wo_ref, g2_ref, wr_ref, br_ref,
                 x1_ref, h2_ref, lg_ref):
    pa = _dot(oa_ref[...], wpa_ref[...])
    pb = _dot(ob_ref[...], wpb_ref[...])
    merged = jax.nn.sigmoid(ma_ref[...]) * pa + jax.nn.sigmoid(mb_ref[...]) * pb
    x1 = x_ref[...] + _dot(merged.astype(BF16), wo_ref[...])
    h2 = _rms(x1, g2_ref[...])
    x1_ref[...] = x1
    h2_ref[...] = h2
    h_hi = h2.astype(BF16)
    h_lo = (h2 - h_hi.astype(F32)).astype(BF16)
    lg = _dot(jnp.concatenate([h_hi, h_hi, h_lo], axis=1), wr_ref[...]) + br_ref[...]
    lg_ref[...] = lg.T


def _router_weights(w_rg, b_rg, w_re, b_re):
    wr = jnp.zeros((D_MODEL, LANES), F32)
    wr = wr.at[:, :N_GROUPS].set(w_rg)
    wr = wr.at[:, _R_EXP0:_R_EXP0 + N_EXPERTS].set(jnp.transpose(w_re, (1, 0, 2)).reshape(D_MODEL, N_EXPERTS))
    w_hi = wr.astype(BF16)
    w_lo = (wr - w_hi.astype(F32)).astype(BF16)
    br = jnp.zeros((1, LANES), F32)
    br = br.at[0, :N_GROUPS].set(b_rg)
    br = br.at[0, _R_EXP0:_R_EXP0 + N_EXPERTS].set(b_re.reshape(N_EXPERTS))
    return jnp.concatenate([w_hi, w_lo, w_hi], axis=0), br


def _post_mixer(x2, o_a, o_b, zm, wpa, wpb, wo, g2, wrt, br):
    n = x2.shape[0]
    tm = _tile(n, TM_POST)
    row = lambda i: (i, 0)
    const = lambda i: (0, 0)
    return _pallas(
        _post_kernel,
        grid=(n // tm,),
        in_specs=[pl.BlockSpec((tm, D_MODEL), row), pl.BlockSpec((tm, NSA_WIDTH), row),
                  pl.BlockSpec((tm, HGRN_VW), row),
                  pl.BlockSpec((tm, D_MODEL), lambda i: (i, 0)), pl.BlockSpec((tm, D_MODEL), lambda i: (i, 1)),
                  pl.BlockSpec((NSA_WIDTH, D_MODEL), const), pl.BlockSpec((HGRN_VW, D_MODEL), const),
                  pl.BlockSpec((D_MODEL, D_MODEL), const), pl.BlockSpec((1, D_MODEL), const),
                  pl.BlockSpec((3 * D_MODEL, LANES), const), pl.BlockSpec((1, LANES), const)],
        out_specs=[pl.BlockSpec((tm, D_MODEL), row), pl.BlockSpec((tm, D_MODEL), row),
                   pl.BlockSpec((LANES, tm), lambda i: (0, i))],
        out_shape=[jax.ShapeDtypeStruct((n, D_MODEL), F32), jax.ShapeDtypeStruct((n, D_MODEL), F32),
                   jax.ShapeDtypeStruct((LANES, n), F32)],
        compiler_params=_cparams(("parallel",)),
    )(x2, o_a, o_b, zm, zm, wpa, wpb, wo, g2, wrt, br)


def _lowest_argmax(vals, top):
    idx = jnp.full(top.shape, len(vals) - 1, jnp.int32)
    for i in range(len(vals) - 2, -1, -1):
        idx = jnp.where(vals[i] == top, i, idx)
    return idx


def _route_kernel(lg_ref, u_ref, eid_ref, gate_ref, rank_ref, cnt_ref, carry_ref):
    @pl.when(pl.program_id(0) == 0)
    def _():
        carry_ref[...] = jnp.zeros_like(carry_ref)

    lg = lg_ref[...]
    tr = lg.shape[1]
    grp = [lg[g:g + 1] for g in range(N_GROUPS)]
    mx = functools.reduce(jnp.maximum, grp)
    ex = [jnp.exp(v - mx) for v in grp]
    den = functools.reduce(lambda a, b: a + b, ex)
    pr = [e / den for e in ex]
    pg = functools.reduce(jnp.maximum, pr)
    gtop = _lowest_argmax(pr, pg)
    le = []
    for j in range(EXPERTS_PER_GROUP):
        v = lg[_R_EXP0 + j:_R_EXP0 + j + 1]
        for g in range(1, N_GROUPS):
            r = _R_EXP0 + g * EXPERTS_PER_GROUP + j
            v = jnp.where(gtop == g, lg[r:r + 1], v)
        le.append(v)
    m1 = functools.reduce(jnp.maximum, le)
    i1 = _lowest_argmax(le, m1)
    le2 = [jnp.where(i1 == j, -jnp.inf, le[j]) for j in range(EXPERTS_PER_GROUP)]
    m2 = functools.reduce(jnp.maximum, le2)
    i2 = _lowest_argmax(le2, m2)
    e2 = jnp.exp(m2 - m1)
    den2 = 1.0 + e2
    gate1 = (1.0 / den2) * pg
    gate2 = (e2 / den2) * pg
    eid1 = gtop * EXPERTS_PER_GROUP + i1
    eid2 = gtop * EXPERTS_PER_GROUP + i2

    eio = lax.broadcasted_iota(jnp.int32, (N_EXPERTS, tr), 0)
    oh1 = jnp.where(eio == eid1, 1.0, 0.0)
    oh2 = jnp.where(eio == eid2, 1.0, 0.0)
    both = (oh1 + oh2).astype(BF16)
    carry = carry_ref[...]
    before = _dot(both, u_ref[...]) + jnp.concatenate([carry] * (tr // LANES), axis=1)
    rank1 = jnp.sum(oh1 * before, axis=0, keepdims=True).astype(jnp.int32)
    rank2 = jnp.sum(oh2 * before, axis=0, keepdims=True).astype(jnp.int32)
    carry = carry + _dot(both, jnp.ones((tr, LANES), BF16))
    carry_ref[...] = carry
    cnt_ref[...] = carry
    zi = jnp.zeros((6, tr), jnp.int32)
    eid_ref[...] = jnp.concatenate([eid1, eid2, zi], axis=0)
    rank_ref[...] = jnp.concatenate([rank1, rank2, zi], axis=0)
    gate_ref[...] = jnp.concatenate([gate1, gate2, jnp.zeros((6, tr), F32)], axis=0)


def _route(lg):
    n = lg.shape[1]
    tr = _tile(n, TR_ROUTE, LANES)
    u = jnp.asarray(np.triu(np.ones((tr, tr), np.float32), 1)).astype(BF16)
    col = lambda i: (0, i)
    return _pallas(
        _route_kernel,
        grid=(n // tr,),
        in_specs=[pl.BlockSpec((LANES, tr), col), pl.BlockSpec((tr, tr), lambda i: (0, 0))],
        out_specs=[pl.BlockSpec((8, tr), col), pl.BlockSpec((8, tr), col), pl.BlockSpec((8, tr), col),
                   pl.BlockSpec((N_EXPERTS, LANES), lambda i: (0, 0))],
        out_shape=[jax.ShapeDtypeStruct((8, n), jnp.int32), jax.ShapeDtypeStruct((8, n), F32),
                   jax.ShapeDtypeStruct((8, n), jnp.int32), jax.ShapeDtypeStruct((N_EXPERTS, LANES), F32)],
        scratch_shapes=[pltpu.VMEM((N_EXPERTS, LANES), F32)],
        compiler_params=_cparams(("arbitrary",)),
    )(lg, u)


def _dest_kernel(ps_ref, eid_ref, rank_ref, dest_ref):
    eid = eid_ref[...]
    dest = rank_ref[...]
    for e in range(N_EXPERTS):
        dest = dest + jnp.where(eid == e, ps_ref[e], 0)
    dest_ref[...] = dest


def _dest_rows(pad_start, eid, rank):
    n = eid.shape[1]
    tr = _tile(n, 4096, LANES)
    col = lambda i: (0, i)
    return _pallas(
        _dest_kernel,
        grid=(n // tr,),
        in_specs=[pl.BlockSpec(memory_space=pltpu.SMEM), pl.BlockSpec((8, tr), col), pl.BlockSpec((8, tr), col)],
        out_specs=pl.BlockSpec((8, tr), col),
        out_shape=jax.ShapeDtypeStruct((8, n), jnp.int32),
        compiler_params=_cparams(("parallel",)),
    )(pad_start, eid, rank)


DMA_UNROLL = 8


def _issue_rows(t_n, row_copy):
    def issue(t, c):
        for k in range(TOP_K_EXPERTS):
            row_copy(t, k).start()
        return c

    lax.fori_loop(0, t_n, issue, 0, unroll=DMA_UNROLL)


def _wait_rows(t_n, row_copy):
    def wait(t, c):
        for k in range(TOP_K_EXPERTS):
            row_copy(t, k).wait()
        return c

    lax.fori_loop(0, t_n, wait, 0, unroll=DMA_UNROLL)


def _row_loops(t_n, row_copy):
    _issue_rows(t_n, row_copy)
    _wait_rows(t_n, row_copy)


def _dispatch_kernel(dest_ref, h_ref, xb_in_ref, xb_ref, sem):
    del xb_in_ref

    def row_copy(t, k):
        return pltpu.make_async_copy(h_ref.at[pl.ds(t, 1), :], xb_ref.at[pl.ds(dest_ref[k, t], 1), :], sem)

    _row_loops(h_ref.shape[0], row_copy)


def _smem_cols(t):
    return pl.BlockSpec((8, t), lambda i: (0, i), memory_space=pltpu.SMEM)


def _dispatch(dest, h2, rows_total):
    n = h2.shape[0]
    t = _tile(n, T_DISP, LANES)
    xb0 = jnp.zeros((rows_total, D_MODEL), F32)
    return _pallas(
        _dispatch_kernel,
        grid=(n // t,),
        in_specs=[_smem_cols(t), pl.BlockSpec((t, D_MODEL), lambda i: (i, 0)), pl.BlockSpec(memory_space=pl.ANY)],
        out_specs=pl.BlockSpec(memory_space=pl.ANY),
        out_shape=jax.ShapeDtypeStruct((rows_total, D_MODEL), F32),
        scratch_shapes=[pltpu.SemaphoreType.DMA(())],
        input_output_aliases={2: 0},
        compiler_params=_cparams(("arbitrary",)),
    )(dest, h2, xb0)


def _expert_kernel(blk_e_ref, x_ref, wg_ref, wu_ref, wd_ref, y_ref):
    del blk_e_ref
    x = x_ref[...].astype(BF16)
    hid = jax.nn.silu(_dot(x, wg_ref[0])) * _dot(x, wu_ref[0])
    y_ref[...] = _dot(hid.astype(BF16), wd_ref[0])


def _experts(blk_e, xb, wg, wu, wd):
    rows_total = xb.shape[0]
    wsel = lambda i, e: (e[i], 0, 0)
    grid_spec = pltpu.PrefetchScalarGridSpec(
        num_scalar_prefetch=1,
        grid=(rows_total // MOE_BM,),
        in_specs=[pl.BlockSpec((MOE_BM, D_MODEL), lambda i, e: (i, 0)),
                  pl.BlockSpec((1, D_MODEL, D_EXPERT), wsel), pl.BlockSpec((1, D_MODEL, D_EXPERT), wsel),
                  pl.BlockSpec((1, D_EXPERT, D_MODEL), wsel)],
        out_specs=pl.BlockSpec((MOE_BM, D_MODEL), lambda i, e: (i, 0)),
    )
    return _pallas(
        _expert_kernel,
        grid_spec=grid_spec,
        out_shape=jax.ShapeDtypeStruct((rows_total, D_MODEL), F32),
        compiler_params=_cparams(("arbitrary",)),
    )(blk_e, xb, wg, wu, wd)


def _combine_kernel(dest_ref, next_ref, x1_ref, gate_ref, fn_ref, yb_ref, y_ref, buf_ref, sem):
    i = pl.program_id(0)
    slot = i % 2
    t_n = x1_ref.shape[0]

    def row_copy(idx_ref, sl):
        def copy(t, k):
            return pltpu.make_async_copy(yb_ref.at[pl.ds(idx_ref[k, t], 1), :], buf_ref.at[sl, k, pl.ds(t, 1), :],
                                         sem.at[sl])
        return copy

    @pl.when(i == 0)
    def _():
        _issue_rows(t_n, row_copy(dest_ref, 0))

    @pl.when(i + 1 < pl.num_programs(0))
    def _():
        _issue_rows(t_n, row_copy(next_ref, 1 - slot))

    _wait_rows(t_n, row_copy(dest_ref, slot))
    g = gate_ref[...]
    out = x1_ref[...] + g[:, 0:1] * buf_ref[slot, 0] + g[:, 1:2] * buf_ref[slot, 1]
    y_ref[...] = _rms(out, fn_ref[...])


def _combine(dest, x1, gate_t, fnorm, yb):
    n = x1.shape[0]
    t = _tile(n, T_DISP, LANES)
    return _pallas(
        _combine_kernel,
        grid=(n // t,),
        in_specs=[_smem_cols(t),
                  pl.BlockSpec((8, t), lambda i: (0, jnp.minimum(i + 1, n // t - 1)), memory_space=pltpu.SMEM),
                  pl.BlockSpec((t, D_MODEL), lambda i: (i, 0)), pl.BlockSpec((t, 8), lambda i: (i, 0)),
                  pl.BlockSpec((1, D_MODEL), lambda i: (0, 0)), pl.BlockSpec(memory_space=pl.ANY)],
        out_specs=pl.BlockSpec((t, D_MODEL), lambda i: (i, 0)),
        out_shape=jax.ShapeDtypeStruct((n, D_MODEL), F32),
        scratch_shapes=[pltpu.VMEM((2, TOP_K_EXPERTS, t, D_MODEL), F32), pltpu.SemaphoreType.DMA((2,))],
        compiler_params=_cparams(("arbitrary",)),
    )(dest, dest, x1, gate_t, fnorm, yb)


def _moe_and_final(x1, h2, lg, wg, wu, wd, fnorm):
    n = x1.shape[0]
    eid, gate, rank, cnt = _route(lg)
    counts = cnt[:, 0].astype(jnp.int32)
    padded = (counts + MOE_BM - 1) // MOE_BM * MOE_BM
    pad_end = jnp.cumsum(padded)
    pad_start = (pad_end - padded).astype(jnp.int32)
    n_blocks = -(-(n * TOP_K_EXPERTS + N_EXPERTS * (MOE_BM - 1)) // MOE_BM)
    blk_start = jnp.arange(n_blocks, dtype=jnp.int32) * MOE_BM
    blk_e = jnp.minimum(jnp.sum(pad_end[None, :] <= blk_start[:, None], axis=1), N_EXPERTS - 1).astype(jnp.int32)
    dest = _dest_rows(pad_start, eid, rank)
    xb = _dispatch(dest, h2, n_blocks * MOE_BM)
    yb = _experts(blk_e, xb, wg, wu, wd)
    return _combine(dest, x1, gate.T, fnorm, yb)


def _page_stream(pt_ref, pool_ref, buf_ref, sem, n_pages):
    b = pl.program_id(0)
    slot = b % 2

    def page_copy(bb, sl, p):
        col = pl.ds(pl.multiple_of(p * PAGE_SIZE, PAGE_SIZE), PAGE_SIZE)
        return pltpu.make_async_copy(pool_ref.at[pt_ref[bb, p]], buf_ref.at[sl, :, col], sem.at[sl])

    def issue(bb, sl):
        def body(p, c):
            page_copy(bb, sl, p).start()
            return c
        lax.fori_loop(0, n_pages, body, 0, unroll=DMA_UNROLL)

    def wait():
        def body(p, c):
            page_copy(b, slot, p).wait()
            return c
        lax.fori_loop(0, n_pages, body, 0, unroll=DMA_UNROLL)

    @pl.when(b == 0)
    def _():
        issue(0, 0)

    @pl.when(b + 1 < pl.num_programs(0))
    def _():
        issue(b + 1, 1 - slot)

    return slot, wait


TR_BLOCK = 256


def _chunk_perm():
    n_loc = TR_BLOCK // CMP_STRIDE
    r = np.arange(TR_BLOCK)
    perm = np.zeros((TR_BLOCK, TR_BLOCK), np.float32)
    perm[r, (r % n_loc) * CMP_STRIDE + r // n_loc] = 1.0
    return jnp.asarray(perm).astype(BF16)


def _compress_sample_kernel(pt_ref, pool_ref, perm_ref, w1bd_ref, w2bd_ref, c0_ref, b2_ref, out_ref,
                            buf_ref, ak_ref, av_ref, sem, *, n_pages, n_ch):
    slot, wait = _page_stream(pt_ref, pool_ref, buf_ref, sem, n_pages)
    wait()
    n_loc = TR_BLOCK // CMP_STRIDE

    def block(j, carry):
        xt = buf_ref[slot, :, pl.ds(pl.multiple_of(j * TR_BLOCK, TR_BLOCK), TR_BLOCK)].astype(BF16)
        rows = _dot_nt(perm_ref[...], xt)
        dst = pl.ds(pl.multiple_of(j * n_loc, n_loc), n_loc)
        for s in range(CMP_STRIDE):
            piece = rows[s * n_loc:(s + 1) * n_loc]
            ak_ref[dst, s * LANES:(s + 1) * LANES] = piece[:, :LANES].astype(BF16)
            av_ref[dst, s * LANES:(s + 1) * LANES] = piece[:, LANES:].astype(BF16)
        return carry

    lax.fori_loop(0, n_pages * PAGE_SIZE // TR_BLOCK, block, 0, unroll=4)
    for c, a_ref in ((0, ak_ref), (1, av_ref)):
        _compress_mlp(c, a_ref, n_ch, w1bd_ref, w2bd_ref, c0_ref, b2_ref, out_ref)


def _compress_sample(page_table, pool, cw):
    b_sz, n_pages = page_table.shape
    past = n_pages * PAGE_SIZE
    n_ch = past // CMP_STRIDE
    z3 = lambda b, pt: (0, 0, 0)
    specs = [pl.BlockSpec((2, CMP_STRIDE * LANES, 4 * LANES), z3), pl.BlockSpec((2, 2 * LANES, LANES), z3),
             pl.BlockSpec((2, 1, 2 * LANES), z3), pl.BlockSpec((2, 1, LANES), z3)]
    grid_spec = pltpu.PrefetchScalarGridSpec(
        num_scalar_prefetch=1,
        grid=(b_sz,),
        in_specs=[pl.BlockSpec(memory_space=pl.ANY), pl.BlockSpec((TR_BLOCK, TR_BLOCK), lambda b, pt: (0, 0))] + specs,
        out_specs=pl.BlockSpec((n_ch, KV_WIDTH), lambda b, pt: (b, 0)),
        scratch_shapes=[pltpu.VMEM((2, KV_WIDTH, past), F32), pltpu.VMEM((n_ch, CMP_STRIDE * LANES), BF16),
                        pltpu.VMEM((n_ch, CMP_STRIDE * LANES), BF16), pltpu.SemaphoreType.DMA((2,))],
    )
    assert past % TR_BLOCK == 0
    return _pallas(
        functools.partial(_compress_sample_kernel, n_pages=n_pages, n_ch=n_ch),
        grid_spec=grid_spec,
        out_shape=jax.ShapeDtypeStruct((b_sz * n_ch, KV_WIDTH), F32),
        compiler_params=_cparams(("arbitrary",)),
    )(page_table, pool, _chunk_perm(), *cw)


def _cmp_sample_kernel(qu_ref, ckv_ref, covt_ref, ocmp_ref, selt_ref, ps_s, *, past, t_len, n_sel, n_grp, n_cmp_pad):
    r = qu_ref.shape[1]
    rows_kt = NSA_KV_HEADS * t_len
    tpos = past + (lax.broadcasted_iota(jnp.int32, (r, 1), 0) % t_len)
    c_end = lax.broadcasted_iota(jnp.int32, (1, n_cmp_pad), 1) * CMP_STRIDE + (CMP_BLOCK - 1)

    def one_seq(sq, carry):
        ckv = ckv_ref[pl.ds(pl.multiple_of(sq * n_cmp_pad, n_cmp_pad), n_cmp_pad), :]
        p = _masked_softmax(_dot_nt(qu_ref[sq], ckv[:, :LANES].astype(BF16)), c_end <= tpos)
        ocmp_ref[sq] = _dot(p.astype(BF16), ckv[:, LANES:].astype(BF16))
        psum = []
        for kvh in range(NSA_KV_HEADS):
            acc = None
            for g in range(NSA_GROUP):
                blk = g * NSA_KV_HEADS + kvh
                pg = p[blk * t_len:(blk + 1) * t_len]
                acc = pg if acc is None else acc + pg
            psum.append(acc)
        ps_s[pl.ds(pl.multiple_of(sq * rows_kt, rows_kt), rows_kt), :] = jnp.concatenate(psum, axis=0)
        return carry

    lax.fori_loop(0, n_grp, one_seq, 0)
    imp_t = _dot_nt(covt_ref[...], ps_s[...], precision=HIGHEST)
    tp = past + (lax.broadcasted_iota(jnp.int32, imp_t.shape, 1) % t_len)
    selt_ref[0] = _select_t(imp_t, tp, n_sel)


def _cmp_sample(qu32, ckv, b_sz, past, t_len, n_sel, n_sel_pad):
    n_cmp_pad = past // CMP_STRIDE
    covt = _cover_t(n_cmp_pad, n_cmp_pad - 1, n_sel, n_sel_pad)
    r = qu32.shape[1]
    rows_kt = NSA_KV_HEADS * t_len
    n_grp = math.gcd(b_sz, LANES // rows_kt)
    assert rows_kt % 8 == 0
    ocmp, sel_t = _pallas(
        functools.partial(_cmp_sample_kernel, past=past, t_len=t_len, n_sel=n_sel, n_grp=n_grp, n_cmp_pad=n_cmp_pad),
        grid=(b_sz // n_grp,),
        in_specs=[pl.BlockSpec((n_grp, r, LANES), lambda b: (b, 0, 0)),
                  pl.BlockSpec((n_grp * n_cmp_pad, KV_WIDTH), lambda b: (b, 0)),
                  pl.BlockSpec((n_sel_pad, n_cmp_pad), lambda b: (0, 0))],
        out_specs=[pl.BlockSpec((n_grp, r, LANES), lambda b: (b, 0, 0)),
                   pl.BlockSpec((1, n_sel_pad, n_grp * rows_kt), lambda b: (b, 0, 0))],
        out_shape=[jax.ShapeDtypeStruct((b_sz, r, LANES), F32),
                   jax.ShapeDtypeStruct((b_sz // n_grp, n_sel_pad, n_grp * rows_kt), F32)],
        scratch_shapes=[pltpu.VMEM((n_grp * rows_kt, n_cmp_pad), F32)],
        compiler_params=_cparams(("parallel",)),
    )(qu32, ckv, covt)
    sel = sel_t.reshape(b_sz // n_grp, n_sel_pad, n_grp, rows_kt).transpose(0, 2, 3, 1)
    return ocmp, sel.reshape(b_sz, rows_kt, n_sel_pad)


TK_SAMPLE = 1024


def _slc_win_sample_kernel(pt_ref, qr_ref, qa_ref, sel_ref, e_ref, pool_ref, new_s_ref, win_ref, new_w_ref, ocmp_ref,
                           gate_ref, o_ref, buf_ref, sem, *, n_pages, past, t_len, win_buf):
    slot, wait_pages = _page_stream(pt_ref, pool_ref, buf_ref, sem, n_pages)
    qr = qr_ref[0]
    sel = sel_ref[0]
    r = qr.shape[0]
    tpos = past + (lax.broadcasted_iota(jnp.int32, (r, 1), 0) % t_len)
    init = (jnp.full((r, 1), NEG, F32), jnp.zeros((r, 1), F32), jnp.zeros((r, LANES), F32))
    new_i = lax.broadcasted_iota(jnp.int32, (1, LANES), 1)
    new_ok = ((past + new_i) <= tpos) & (new_i < t_len)

    wt = win_ref[0]
    wpos = past - win_buf + lax.broadcasted_iota(jnp.int32, (1, win_buf), 1)
    ok = (wpos >= 0) & (wpos <= tpos) & (wpos > tpos - WINDOW)
    carry = _online_step(_dot(qr, wt[:LANES].astype(BF16)), ok, wt[LANES:].astype(BF16), *init, v_t=True)
    rows = new_w_ref[0]
    ok = new_ok & ((past + new_i) > tpos - WINDOW)
    m, l, acc = _online_step(_dot_nt(qr, rows[:, :LANES].astype(BF16)), ok, rows[:, LANES:].astype(BF16), *carry)
    o_win = acc / jnp.maximum(l, 1e-30)

    wait_pages()

    qa = qa_ref[0]

    def slc_body(kt, carry):
        col = pl.ds(pl.multiple_of(kt * TK_SAMPLE, TK_SAMPLE), TK_SAMPLE)
        k_aug = jnp.concatenate([buf_ref[slot, :LANES, col].astype(BF16), e_ref[:, col]], axis=0)
        return _online_step(_dot(qa, k_aug), None, buf_ref[slot, LANES:, col].astype(BF16), *carry, v_t=True)

    carry = lax.fori_loop(0, past // TK_SAMPLE, slc_body, init, unroll=2)
    rows = new_s_ref[0]
    n_past_blk = past // SEL_BLOCK
    ok = new_ok & (sel[:, n_past_blk:n_past_blk + 1] > 0.5)
    m, l, acc = _online_step(_dot_nt(qr, rows[:, :LANES].astype(BF16)), ok, rows[:, LANES:].astype(BF16), *carry)
    o_slc = acc / jnp.maximum(l, 1e-30)

    g = gate_ref[0]
    o = g[:, 0:1] * ocmp_ref[0] + g[:, 1:2] * o_slc + g[:, 2:3] * o_win
    row = lax.broadcasted_iota(jnp.int32, (r, LANES), 0)
    lane = lax.broadcasted_iota(jnp.int32, (r, LANES), 1)
    o_ref[0] = jnp.where((((row // t_len) % 2) == 0) == (lane < HEAD_DIM), o, 0.0)


def _slc_win_sample(page_table, qr32, sel32, pool, new_s, win_t, new_w, ocmp32, gate32, past, t_len, n_sel_pad):
    b_sz, n_pages = page_table.shape
    win_buf = win_t.shape[2]
    r = qr32.shape[1]
    n_past_blk = past // SEL_BLOCK
    assert n_past_blk <= LANES
    unsel = jnp.pad(1.0 - sel32[:, :, :n_past_blk].astype(F32), ((0, 0), (0, 0), (0, LANES - n_past_blk)))
    qa32 = jnp.concatenate([qr32, unsel.astype(BF16)], axis=2)
    e_mat = _block_neg(past).T
    b3 = lambda b, pt: (b, 0, 0)
    grid_spec = pltpu.PrefetchScalarGridSpec(
        num_scalar_prefetch=1,
        grid=(b_sz,),
        in_specs=[pl.BlockSpec((1, r, LANES), b3), pl.BlockSpec((1, r, 2 * LANES), b3),
                  pl.BlockSpec((1, r, n_sel_pad), b3),
                  pl.BlockSpec((LANES, past), lambda b, pt: (0, 0)),
                  pl.BlockSpec(memory_space=pl.ANY),
                  pl.BlockSpec((1, LANES, KV_WIDTH), b3),
                  pl.BlockSpec((1, KV_WIDTH, win_buf), b3),
                  pl.BlockSpec((1, LANES, KV_WIDTH), b3),
                  pl.BlockSpec((1, r, LANES), b3), pl.BlockSpec((1, r, 3), b3)],
        out_specs=pl.BlockSpec((1, r, LANES), b3),
        scratch_shapes=[pltpu.VMEM((2, KV_WIDTH, past), F32), pltpu.SemaphoreType.DMA((2,))],
    )
    return _pallas(
        functools.partial(_slc_win_sample_kernel, n_pages=n_pages, past=past, t_len=t_len, win_buf=win_buf),
        grid_spec=grid_spec,
        out_shape=jax.ShapeDtypeStruct((b_sz, r, LANES), F32),
        compiler_params=_cparams(("arbitrary",)),
    )(page_table, qr32, qa32, sel32, e_mat, pool, new_s, win_t, new_w, ocmp32, gate32)


def _rows32(q, b_sz, t_len):
    q5 = q.reshape(b_sz, t_len, NSA_GROUP, NSA_KV_HEADS, HEAD_DIM).transpose(0, 2, 3, 1, 4)
    eye = jnp.eye(NSA_KV_HEADS, dtype=q.dtype)
    return jnp.einsum('bgktd,kj->bgktjd', q5, eye).reshape(b_sz, NSA_HEADS * t_len, LANES)


def _from_rows32(o32, b_sz, t_len):
    o6 = o32.reshape(b_sz, NSA_GROUP, NSA_KV_HEADS, t_len, NSA_KV_HEADS, HEAD_DIM)
    o5 = jnp.stack([o6[:, :, k, :, k] for k in range(NSA_KV_HEADS)], axis=2)
    return o5.transpose(0, 3, 1, 2, 4).reshape(b_sz * t_len, NSA_WIDTH)


def _nsa_sample(qu, qr, kvs_new, kvw_new, gates, cache_cmp, cache_slc, cache_win, page_table, cw, t_len):
    b_sz, n_pages = page_table.shape
    past = n_pages * PAGE_SIZE
    total = past + t_len
    assert t_len < CMP_STRIDE and t_len <= SEL_BLOCK and past % TK_SAMPLE == 0
    n_sel = -(-total // SEL_BLOCK)
    n_sel_pad = -(-n_sel // 8) * 8
    pool_c = cache_cmp.reshape(-1, PAGE_SIZE, KV_WIDTH).transpose(0, 2, 1)
    pool_s = cache_slc.reshape(-1, PAGE_SIZE, KV_WIDTH).transpose(0, 2, 1)
    win_t = cache_win.reshape(b_sz, cache_win.shape[1], KV_WIDTH).transpose(0, 2, 1)
    ckv = _compress_sample(page_table, pool_c, cw)
    ocmp32, sel = _cmp_sample(_rows32(qu, b_sz, t_len), ckv, b_sz, past, t_len, n_sel, n_sel_pad)
    sel = sel.reshape(b_sz, 1, NSA_KV_HEADS * t_len, n_sel_pad)
    sel32 = jnp.broadcast_to(sel, (b_sz, NSA_GROUP, NSA_KV_HEADS * t_len, n_sel_pad))
    sel32 = sel32.reshape(b_sz, NSA_HEADS * t_len, n_sel_pad).astype(BF16)
    pad_rows = lambda a: jnp.pad(a.reshape(b_sz, t_len, KV_WIDTH), ((0, 0), (0, LANES - t_len), (0, 0)))
    g4 = gates[:, :3 * NSA_HEADS].reshape(b_sz, t_len, NSA_KV_HEADS, NSA_GROUP, 3)
    gate32 = g4.transpose(0, 3, 2, 1, 4).reshape(b_sz, NSA_HEADS * t_len, 3)
    o32 = _slc_win_sample(page_table, _rows32(qr, b_sz, t_len), sel32, pool_s, pad_rows(kvs_new), win_t,
                          pad_rows(kvw_new), ocmp32, gate32, past, t_len, n_sel_pad)
    return _from_rows32(o32, b_sz, t_len).astype(BF16)


def _hgrn_sample_kernel(hq_ref, hf_ref, vt_ref, hgt_ref, s0_ref, lbl_ref, gnt_ref, ot_ref, s_ref, *, t_len):
    lb = _lower_bound(lbl_ref[...])
    q_all = jax.nn.silu(hq_ref[0]) * HGRN_DK ** -0.5
    f_all = lb + (1.0 - lb) * jax.nn.sigmoid(hf_ref[0])
    for h in range(HGRN_HEADS):
        hs = slice(h * LANES, (h + 1) * LANES)
        q = q_all[:, hs]
        f = f_all[:, hs]
        k = 1.0 - f
        vt = vt_ref[0, h]
        st = s0_ref[0, h].T
        cols = []
        for t in range(t_len):
            st = st * f[t:t + 1] + vt[:, t:t + 1] * k[t:t + 1]
            cols.append(jnp.sum(st * q[t:t + 1], axis=1, keepdims=True))
        ot = jnp.concatenate(cols, axis=1)
        y = ot * lax.rsqrt(jnp.mean(ot * ot, axis=0, keepdims=True) + NORM_EPS) * gnt_ref[...]
        ot_ref[0, h] = y * jax.nn.silu(hgt_ref[0, h])
        s_ref[0, h] = st.T


def _hgrn_sample(zh, s0, lb_logits, gnorm, b_sz, t_len):
    z3 = zh.reshape(b_sz, t_len, 4 * HGRN_KW)
    to_t = lambda a: a.reshape(b_sz, t_len, HGRN_HEADS, HGRN_DV).transpose(0, 2, 3, 1)
    vt = to_t(z3[:, :, 2 * HGRN_KW:3 * HGRN_KW])
    hgt = to_t(z3[:, :, 3 * HGRN_KW:])
    col = lambda k: (lambda b: (b, 0, k))
    b4 = lambda b: (b, 0, 0, 0)
    ot, s_new = _pallas(
        functools.partial(_hgrn_sample_kernel, t_len=t_len),
        grid=(b_sz,),
        in_specs=[pl.BlockSpec((1, t_len, HGRN_KW), col(0)), pl.BlockSpec((1, t_len, HGRN_KW), col(1)),
                  pl.BlockSpec((1, HGRN_HEADS, HGRN_DV, t_len), b4), pl.BlockSpec((1, HGRN_HEADS, HGRN_DV, t_len), b4),
                  pl.BlockSpec((1, HGRN_HEADS, HGRN_DK, HGRN_DV), b4),
                  pl.BlockSpec((lb_logits.shape[0], HGRN_KW), lambda b: (0, 0)),
                  pl.BlockSpec((HGRN_DV, 1), lambda b: (0, 0))],
        out_specs=[pl.BlockSpec((1, HGRN_HEADS, HGRN_DV, t_len), b4),
                   pl.BlockSpec((1, HGRN_HEADS, HGRN_DK, HGRN_DV), b4)],
        out_shape=[jax.ShapeDtypeStruct((b_sz, HGRN_HEADS, HGRN_DV, t_len), F32),
                   jax.ShapeDtypeStruct((b_sz, HGRN_HEADS, HGRN_DK, HGRN_DV), F32)],
        compiler_params=_cparams(("parallel",)),
    )(z3, z3, vt, hgt, s0, lb_logits, gnorm.reshape(HGRN_DV, 1))
    o_b = ot.transpose(0, 3, 1, 2).reshape(b_sz * t_len, HGRN_VW)
    return o_b.astype(BF16), s_new


def kernel(x_prompt, x_sample, cache_cmp_kv, cache_slc_kv, cache_win_kv, state_hgrn, page_table, norm1, w_in, cmp_pe,
           cmp_w1, cmp_b1, cmp_w2, cmp_b2, hgrn_lb_logits, hgrn_gnorm, w_proj_a, w_proj_b, w_out, norm2,
           w_router_group, b_router_group, w_router_expert, b_router_expert, w_exp_gate, w_exp_up, w_exp_down,
           final_norm):
    assert w_in.shape[0] == 1, "one layer"
    b_sz, s_len, _ = x_prompt.shape
    d_sz, t_len, _ = x_sample.shape
    past = page_table.shape[1] * PAGE_SIZE
    n_p = b_sz * s_len
    n_s = d_sz * t_len
    kv_row = (2, NSA_KV_HEADS, HEAD_DIM)

    offs = np.cumsum((0,) + IN_SPLITS)
    w_att = _attn_weight(w_in[0])
    w_h = w_in[0][:, offs[5]:offs[9]].astype(BF16)
    w_m = w_in[0][:, offs[9]:offs[11]].astype(BF16)
    g1 = norm1[0][None]
    cw = _compress_weights(cmp_pe[0], cmp_w1[0], cmp_b1[0], cmp_w2[0], cmp_b2[0])
    perm = _pair_perm()
    wpa = w_proj_a[0][perm].astype(BF16)
    wpb = w_proj_b[0].astype(BF16)
    wo = w_out[0].astype(BF16)
    wrt, br = _router_weights(w_router_group[0], b_router_group[0], w_router_expert[0], b_router_expert[0])
    gn = hgrn_gnorm[0][None]

    xp = x_prompt.reshape(n_p, D_MODEL)
    cos, sin = _rope_tables(jnp.arange(s_len))
    qu, qr, kvc_p, kvs_p, kvw_p, kvs_b, kvw_b, gates = _attn_proj(xp, g1, w_att, cos, sin)
    zh = _norm_proj(xp, g1, w_h)
    zm = _norm_proj(xp, g1, w_m)
    ckv = _compress_prompt(kvc_p, b_sz, s_len, cw)
    o_a = _nsa_prompt(qu, qr, ckv, kvs_b, kvw_b, gates, b_sz, s_len)
    o_b, hg_p = _hgrn_prompt(zh, hgrn_lb_logits, gn, b_sz, s_len)
    x1_p, h2_p, lg_p = _post_mixer(xp, o_a, o_b, zm, wpa, wpb, wo, norm2[0][None], wrt, br)

    xs = x_sample.reshape(n_s, D_MODEL)
    cos, sin = _rope_tables(past + jnp.tile(jnp.arange(t_len), d_sz))
    qu, qr, kvc_s, kvs_s, kvw_s, _, _, gates = _attn_proj(xs, g1, w_att, cos, sin)
    zh = _norm_proj(xs, g1, w_h)
    zm = _norm_proj(xs, g1, w_m)
    o_a = _nsa_sample(qu, qr, kvs_s, kvw_s, gates, cache_cmp_kv[0], cache_slc_kv[0], cache_win_kv[0], page_table, cw,
                      t_len)
    o_b, hg_s = _hgrn_sample(zh, state_hgrn[0], hgrn_lb_logits, hgrn_gnorm[0], d_sz, t_len)
    x1_s, h2_s, lg_s = _post_mixer(xs, o_a, o_b, zm, wpa, wpb, wo, norm2[0][None], wrt, br)

    w_exp = (w_exp_gate[0].astype(BF16), w_exp_up[0].astype(BF16), w_exp_down[0].astype(BF16))
    y_p = _moe_and_final(x1_p, h2_p, lg_p, *w_exp, final_norm[None]).reshape(b_sz, s_len, D_MODEL)
    y_s = _moe_and_final(x1_s, h2_s, lg_s, *w_exp, final_norm[None]).reshape(d_sz, t_len, D_MODEL)

    win_p = kvw_p.reshape((b_sz, s_len) + kv_row)[:, s_len - min(WINDOW, s_len):]
    win_rows = jnp.concatenate([cache_win_kv[0], kvw_s.reshape((d_sz, t_len) + kv_row)], axis=1)
    win_s = win_rows[:, win_rows.shape[1] - min(WINDOW, past + t_len):]
    return (y_p, y_s,
            kvc_p.reshape((1, b_sz, s_len) + kv_row), kvc_s.reshape((1, d_sz, t_len) + kv_row),
            kvs_p.reshape((1, b_sz, s_len) + kv_row), kvs_s.reshape((1, d_sz, t_len) + kv_row),
            win_p[None], win_s[None], hg_p[None], hg_s[None])
```

```python
import functools
import math

import numpy as np
import jax
import jax.numpy as jnp
from jax import lax
from jax.experimental import pallas as pl
from jax.experimental.pallas import tpu as pltpu

D_MODEL = 1024
PAGE_SIZE = 128
NSA_HEADS = 8
NSA_KV_HEADS = 2
NSA_GROUP = NSA_HEADS // NSA_KV_HEADS
HEAD_DIM = 64
CMP_BLOCK = 32
CMP_STRIDE = 16
CMP_HIDDEN = 2 * HEAD_DIM
SEL_BLOCK = 64
N_SEL = 16
WINDOW = 512
ROPE_THETA = 10000.0
HGRN_HEADS = 4
HGRN_DK = 128
HGRN_DV = 128
N_GROUPS = 4
EXPERTS_PER_GROUP = 8
N_EXPERTS = N_GROUPS * EXPERTS_PER_GROUP
TOP_K_EXPERTS = 2
D_EXPERT = 512
NSA_WIDTH = NSA_HEADS * HEAD_DIM
KV_WIDTH = 2 * NSA_KV_HEADS * HEAD_DIM
HGRN_KW = HGRN_HEADS * HGRN_DK
HGRN_VW = HGRN_HEADS * HGRN_DV
IN_SPLITS = (NSA_WIDTH, KV_WIDTH, KV_WIDTH, KV_WIDTH, 3 * NSA_HEADS, HGRN_KW, HGRN_KW, HGRN_VW, HGRN_VW,
             D_MODEL, D_MODEL)
NORM_EPS = 1e-6
NEG = -1e9
BIG = 1e9

LANES = 128
VMEM_LIMIT = 56 * 1024 * 1024
TM_PROJ = 512
TQ = 256
TK_SLC = 512
HG_CHUNK = 64
HG_TS = 256
HG_SEQS = 2
HG_UNROLL = 2
TM_POST = 256
TR_ROUTE = 512
MOE_BM = 256
T_DISP = 256

F32 = jnp.float32
BF16 = jnp.bfloat16
HIGHEST = lax.Precision.HIGHEST


def _dot(a, b, precision=None):
    return jnp.dot(a, b, preferred_element_type=F32, precision=precision)


def _dot_nt(a, b, precision=None):
    return lax.dot_general(a, b, (((1,), (1,)), ((), ())), preferred_element_type=F32, precision=precision)


def _tile(n, target, align=8):
    if n <= target:
        return n
    t = target - target % align
    while n % t:
        t -= align
    return t


def _pallas(body, **kw):
    fn = getattr(body, "func", body)
    return pl.pallas_call(body, name=fn.__name__.strip("_").removesuffix("_kernel"), **kw)


def _cparams(sem):
    return pltpu.CompilerParams(dimension_semantics=sem, vmem_limit_bytes=VMEM_LIMIT)


def _masked_softmax(s, ok):
    m = jnp.max(jnp.where(ok, s, NEG), axis=-1, keepdims=True)
    e = jnp.exp2(jnp.where(ok, s - m, NEG))
    return e / jnp.maximum(jnp.sum(e, axis=-1, keepdims=True), 1e-30)


def _rms(x, g):
    return x * lax.rsqrt(jnp.mean(x * x, axis=-1, keepdims=True) + NORM_EPS) * g


_A_Q, _A_QR, _A_KC, _A_KS, _A_KSR, _A_KW, _A_KWR, _A_G, _A_END = 0, 512, 1024, 1280, 1536, 1664, 1920, 2048, 2176


def _attn_proj_kernel(x_ref, g_ref, w_ref, cos_ref, sin_ref,
                      qu_ref, qr_ref, kvc_ref, kvs_ref, kvw_ref, kvsb_ref, kvwb_ref, gate_ref, *kvct_ref, kv_t):
    h = _rms(x_ref[...], g_ref[...]).astype(BF16)
    z = _dot(h, w_ref[...])
    cos = cos_ref[...]
    sin = sin_ref[...]
    q = z[:, _A_Q:_A_QR]
    qu_ref[...] = q.astype(BF16)
    qr_ref[...] = (q * cos + z[:, _A_QR:_A_KC] * sin).astype(BF16)
    kvc = z[:, _A_KC:_A_KS]
    kvc_ref[...] = kvc
    ck = cos[:, :LANES]
    sk = sin[:, :LANES]
    ks = z[:, _A_KS:_A_KS + 128] * ck + z[:, _A_KSR:_A_KW] * sk
    vs = z[:, _A_KS + 128:_A_KSR]
    kvsb_ref[:, :128] = ks.astype(BF16)
    kvsb_ref[:, 128:] = vs.astype(BF16)
    kw = z[:, _A_KW:_A_KW + 128] * ck + z[:, _A_KWR:_A_G] * sk
    vw = z[:, _A_KW + 128:_A_KWR]
    kvwb_ref[:, :128] = kw.astype(BF16)
    kvwb_ref[:, 128:] = vw.astype(BF16)
    if kv_t:
        kvct_ref[0][0] = kvc.T
        kvs_ref[0, :128, :] = ks.T
        kvs_ref[0, 128:, :] = vs.T
        kvw_ref[0, :128, :] = kw.T
        kvw_ref[0, 128:, :] = vw.T
    else:
        kvs_ref[:, :128] = ks
        kvs_ref[:, 128:] = vs
        kvw_ref[:, :128] = kw
        kvw_ref[:, 128:] = vw
    gate_ref[...] = jax.nn.sigmoid(z[:, _A_G:_A_END])


def _rot_cols(w):
    d, n = w.shape
    w4 = w.reshape(d, n // HEAD_DIM, 2, HEAD_DIM // 2)
    return jnp.stack([-w4[:, :, 1], w4[:, :, 0]], axis=2).reshape(d, n)


def _pair_perm():
    idx = []
    for g in range(NSA_GROUP):
        for kvh in range(NSA_KV_HEADS):
            h = kvh * NSA_GROUP + g
            idx.extend(range(h * HEAD_DIM, (h + 1) * HEAD_DIM))
    return np.asarray(idx, np.int32)


def _attn_weight(w_in):
    offs = np.cumsum((0,) + IN_SPLITS)
    scale = HEAD_DIM ** -0.5 * math.log2(math.e)
    wq = w_in[:, offs[0]:offs[1]][:, _pair_perm()] * scale
    wkc = w_in[:, offs[1]:offs[2]]
    wks = w_in[:, offs[2]:offs[3]]
    wkw = w_in[:, offs[3]:offs[4]]
    wg = w_in[:, offs[4]:offs[5]]
    wg = jnp.pad(wg, ((0, 0), (0, LANES - wg.shape[1])))
    half = KV_WIDTH // 2
    w = jnp.concatenate([wq, _rot_cols(wq), wkc, wks, _rot_cols(wks[:, :half]), wkw, _rot_cols(wkw[:, :half]), wg],
                        axis=1)
    return w.astype(BF16)


def _rope_tables(pos):
    half = HEAD_DIM // 2
    inv = ROPE_THETA ** (-jnp.arange(half, dtype=F32) / half)
    ang = pos.astype(F32)[:, None] * inv[None, :]
    cos = jnp.tile(jnp.cos(ang), (1, 2 * NSA_HEADS))
    sin = jnp.tile(jnp.sin(ang), (1, 2 * NSA_HEADS))
    return cos, sin


def _attn_proj(x2, g, w_att, cos, sin, seq_len=None):
    n = x2.shape[0]
    tm = _tile(n, TM_PROJ)
    n_tab = cos.shape[0] // tm
    row = lambda i: (i, 0)
    tab = lambda i: (i % n_tab, 0)
    const = lambda i: (0, 0)
    row_major = lambda width, dt: (jax.ShapeDtypeStruct((n, width), dt), pl.BlockSpec((tm, width), row))
    outs = [row_major(NSA_WIDTH, BF16), row_major(NSA_WIDTH, BF16), row_major(KV_WIDTH, F32), row_major(KV_WIDTH, F32),
            row_major(KV_WIDTH, F32), row_major(KV_WIDTH, BF16), row_major(KV_WIDTH, BF16), row_major(LANES, F32)]
    if seq_len is not None:
        assert seq_len % tm == 0 and tm % LANES == 0
        per_seq = seq_len // tm
        pos_minor = (jax.ShapeDtypeStruct((n // seq_len, KV_WIDTH, seq_len), F32),
                     pl.BlockSpec((1, KV_WIDTH, tm), lambda i: (i // per_seq, 0, i % per_seq)))
        outs[3] = outs[4] = pos_minor
        outs.append(pos_minor)
    return _pallas(
        functools.partial(_attn_proj_kernel, kv_t=seq_len is not None),
        grid=(n // tm,),
        in_specs=[pl.BlockSpec((tm, D_MODEL), row), pl.BlockSpec((1, D_MODEL), const),
                  pl.BlockSpec((D_MODEL, _A_END), const),
                  pl.BlockSpec((tm, NSA_WIDTH), tab), pl.BlockSpec((tm, NSA_WIDTH), tab)],
        out_specs=[spec for _, spec in outs],
        out_shape=[shape for shape, _ in outs],
        compiler_params=_cparams(("parallel",)),
    )(x2, g, w_att, cos, sin)


def _norm_proj_kernel(x_ref, g_ref, w_ref, o_ref):
    h = _rms(x_ref[...], g_ref[...]).astype(BF16)
    o_ref[...] = _dot(h, w_ref[...])


def _norm_proj(x2, g, w):
    n = x2.shape[0]
    tm = _tile(n, TM_PROJ)
    width = w.shape[1]
    return _pallas(
        _norm_proj_kernel,
        grid=(n // tm,),
        in_specs=[pl.BlockSpec((tm, D_MODEL), lambda i: (i, 0)), pl.BlockSpec((1, D_MODEL), lambda i: (0, 0)),
                  pl.BlockSpec((D_MODEL, width), lambda i: (0, 0))],
        out_specs=pl.BlockSpec((tm, width), lambda i: (i, 0)),
        out_shape=jax.ShapeDtypeStruct((n, width), F32),
        compiler_params=_cparams(("parallel",)),
    )(x2, g, w)


def _compress_weights(cmp_pe, cmp_w1, cmp_b1, cmp_w2, cmp_b2):
    w1r = cmp_w1.reshape(2, 2, CMP_STRIDE, HEAD_DIM, CMP_HIDDEN)
    eye = jnp.eye(NSA_KV_HEADS, dtype=F32)
    w1bd = jnp.einsum('casdh,kj->cskdajh', w1r, eye).reshape(2, CMP_STRIDE * NSA_KV_HEADS * HEAD_DIM,
                                                             2 * NSA_KV_HEADS * CMP_HIDDEN)
    w2bd = jnp.einsum('chd,kj->ckhjd', cmp_w2, eye).reshape(2, NSA_KV_HEADS * CMP_HIDDEN, NSA_KV_HEADS * HEAD_DIM)
    pe8 = jnp.broadcast_to(cmp_pe.reshape(2, 1, CMP_BLOCK * HEAD_DIM), (2, 8, CMP_BLOCK * HEAD_DIM))
    b1t = jnp.tile(cmp_b1[:, None, :], (1, 1, NSA_KV_HEADS))
    b2t = jnp.tile(cmp_b2[:, None, :], (1, 1, NSA_KV_HEADS))
    c0 = _pallas(
        _compress_bias_kernel,
        out_shape=jax.ShapeDtypeStruct((2, 1, NSA_KV_HEADS * CMP_HIDDEN), F32),
    )(pe8, cmp_w1, b1t)
    return w1bd.astype(BF16), w2bd.astype(BF16), c0, b2t


def _compress_bias_kernel(pe8_ref, w1_ref, b1_ref, c0_ref):
    for c in range(2):
        c0 = _dot(pe8_ref[c], w1_ref[c], precision=HIGHEST)[0:1, :]
        c0_ref[c] = jnp.concatenate([c0] * NSA_KV_HEADS, axis=1) + b1_ref[c]


def _compress_rows(rows_refs, n_ch, a_ref, w1bd_ref, w2bd_ref, c0_ref, b2_ref, out_ref):
    for c in range(2):
        for s in range(CMP_STRIDE):
            a_ref[:, s * LANES:(s + 1) * LANES] = rows_refs[c][pl.ds(s, n_ch, stride=CMP_STRIDE), :].astype(BF16)
        _compress_mlp(c, a_ref, n_ch, w1bd_ref, w2bd_ref, c0_ref, b2_ref, out_ref)


def _compress_mlp(c, a_ref, n_ch, w1bd_ref, w2bd_ref, c0_ref, b2_ref, out_ref):
    hcat = _dot(a_ref[...], w1bd_ref[c])
    h_b = pltpu.roll(hcat[:, 2 * LANES:], n_ch - 1, 0)
    hid = jax.nn.gelu(hcat[:, :2 * LANES] + h_b + c0_ref[c])
    out_ref[:, c * LANES:(c + 1) * LANES] = _dot(hid.astype(BF16), w2bd_ref[c]) + b2_ref[c]


def _compress_prompt_kernel(k_ref, v_ref, w1bd_ref, w2bd_ref, c0_ref, b2_ref, out_ref, a_ref, *, n_ch):
    _compress_rows((k_ref, v_ref), n_ch, a_ref, w1bd_ref, w2bd_ref, c0_ref, b2_ref, out_ref)


def _cw_specs():
    z3 = lambda *a: (0, 0, 0)
    return [pl.BlockSpec((2, CMP_STRIDE * LANES, 4 * LANES), z3), pl.BlockSpec((2, 2 * LANES, LANES), z3),
            pl.BlockSpec((2, 1, 2 * LANES), z3), pl.BlockSpec((2, 1, LANES), z3)]


def _compress_prompt(kvc, b_sz, s_len, cw):
    n_ch = s_len // CMP_STRIDE
    return _pallas(
        functools.partial(_compress_prompt_kernel, n_ch=n_ch),
        grid=(b_sz,),
        in_specs=[pl.BlockSpec((s_len, LANES), lambda b: (b, 0)), pl.BlockSpec((s_len, LANES), lambda b: (b, 1))]
        + _cw_specs(),
        out_specs=pl.BlockSpec((n_ch, KV_WIDTH), lambda b: (b, 0)),
        out_shape=jax.ShapeDtypeStruct((b_sz * n_ch, KV_WIDTH), F32),
        scratch_shapes=[pltpu.VMEM((n_ch, CMP_STRIDE * LANES), BF16)],
        compiler_params=_cparams(("parallel",)),
    )(kvc, kvc, *cw)


def _cover_t(n_cmp_pad, n_cmp, n_sel, n_sel_pad):
    i = np.arange(n_cmp_pad)[None, :]
    j = np.arange(n_sel_pad)[:, None]
    start = i * CMP_STRIDE
    m = (start < (j + 1) * SEL_BLOCK) & (start + CMP_BLOCK > j * SEL_BLOCK) & (i < n_cmp) & (j < n_sel)
    return jnp.asarray(m.astype(np.float32))


def _select_t(imp_t, tp, n_sel):
    j = lax.broadcasted_iota(jnp.int32, imp_t.shape, 0)
    cur = tp // SEL_BLOCK
    valid = j * SEL_BLOCK <= tp
    forced = (j == 0) | (j == cur) | (j == cur - 1)
    score = jnp.where(valid, jnp.where(forced, BIG, imp_t), NEG)
    cnt = jnp.zeros(imp_t.shape, F32)
    for jp in range(n_sel):
        row = score[jp:jp + 1, :]
        cnt = cnt + jnp.where(j > jp, jnp.where(row >= score, 1.0, 0.0), jnp.where(row > score, 1.0, 0.0))
    return jnp.where((cnt < min(N_SEL, n_sel)) & valid, 1.0, 0.0)


def _online_step(s, ok, v, m, l, acc, v_t=False):
    if ok is None:
        m_new = jnp.maximum(m, jnp.max(s, axis=-1, keepdims=True))
        e = jnp.exp2(s - m_new)
    else:
        m_new = jnp.maximum(m, jnp.max(jnp.where(ok, s, NEG), axis=-1, keepdims=True))
        e = jnp.where(ok, jnp.exp2(s - m_new), 0.0)
    alpha = jnp.exp2(m - m_new)
    l = alpha * l + jnp.sum(e, axis=-1, keepdims=True)
    acc = alpha * acc + (_dot_nt if v_t else _dot)(e.astype(BF16), v)
    return m_new, l, acc


WIN_TILES = ((0, WINDOW + TQ),)
CHAIN_ROWS = 512


def _attend(state, s, v_t, mask):
    m_old, l_old, acc_old = state
    rows, tk = s.shape
    if mask is not None:
        s = jnp.where(mask[None], s.reshape(rows // TQ, TQ, tk), NEG).reshape(rows, tk)
    m_new = jnp.maximum(m_old, jnp.broadcast_to(jnp.max(s, axis=-1, keepdims=True), (rows, LANES)))
    e = jnp.concatenate([jnp.exp2(s[:, c:c + LANES] - m_new) for c in range(0, tk, LANES)], axis=1)
    ones = jnp.ones((tk, LANES), BF16)
    pv = _dot(e.astype(BF16), jnp.concatenate([v_t, ones], axis=1))
    alpha = jnp.exp2(m_old - m_new)
    return m_new, alpha * l_old + pv[:, LANES:], alpha * acc_old + pv[:, :LANES]


def _nsa_prompt_kernel(qu_ref, qr_ref, ckv_ref, ks_ref, vs_ref, kw_ref, vw_ref, gate_ref, covt_ref, eb_ref, ge_ref,
                       o_ref, qu_s, qa_s, m_s, l_s, acc_s, owin_s, *, n_sel):
    qi = pl.program_id(1)
    t0 = qi * TQ
    r = NSA_HEADS * TQ
    lane = lax.broadcasted_iota(jnp.int32, (TQ, LANES), 1)
    own = [lane < HEAD_DIM, lane >= HEAD_DIM]
    rpos = lax.broadcasted_iota(jnp.int32, (TQ, 1), 0)
    for g in range(NSA_GROUP):
        qug = qu_ref[:, g * LANES:(g + 1) * LANES]
        qrg = qr_ref[:, g * LANES:(g + 1) * LANES]
        for kvh in range(NSA_KV_HEADS):
            rows = pl.ds((g * NSA_KV_HEADS + kvh) * TQ, TQ)
            qu_s[rows, :] = jnp.where(own[kvh], qug, jnp.zeros_like(qug))
            qa_s[rows, :LANES] = jnp.where(own[kvh], qrg, jnp.zeros_like(qrg))

    for c0 in range(0, r, CHAIN_ROWS):
        cs = pl.ds(c0, CHAIN_ROWS)
        state = (jnp.full((CHAIN_ROWS, LANES), NEG, F32), jnp.zeros((CHAIN_ROWS, LANES), F32),
                 jnp.zeros((CHAIN_ROWS, LANES), F32))
        for off, tk in WIN_TILES:
            rows = pl.ds(pl.multiple_of(t0 + off, TQ), tk)
            kpos = t0 + off - WINDOW + lax.broadcasted_iota(jnp.int32, (1, tk), 1)
            mask = (kpos >= 0) & (kpos <= t0 + rpos) & (kpos > t0 + rpos - WINDOW)
            state = _attend(state, _dot_nt(qa_s[cs, :LANES], kw_ref[rows, :]), vw_ref[rows, :], mask)
        owin_s[cs, :] = state[2] / jnp.maximum(state[1], 1e-30)

    ckv = ckv_ref[...]
    n_cmp_pad = ckv.shape[0]
    c_ok = (lax.broadcasted_iota(jnp.int32, (1, n_cmp_pad), 1) * CMP_STRIDE + (CMP_BLOCK - 1)) <= (t0 + rpos)
    s = _dot_nt(qu_s[...], ckv[:, :LANES].astype(BF16)).reshape(NSA_HEADS, TQ, n_cmp_pad)
    p = _masked_softmax(s, c_ok[None])
    o_cmp = _dot(p.reshape(r, n_cmp_pad).astype(BF16), ckv[:, LANES:].astype(BF16))
    p4 = p.reshape(NSA_GROUP, NSA_KV_HEADS * TQ, n_cmp_pad)
    psum = (p4[0] + p4[1]) + (p4[2] + p4[3])
    imp_t = _dot_nt(covt_ref[...], psum, precision=HIGHEST)
    tp = t0 + (lax.broadcasted_iota(jnp.int32, (n_sel, NSA_KV_HEADS * TQ), 1) % TQ)
    sel_t = _select_t(imp_t[:n_sel], tp, n_sel)
    unsel_t = jnp.concatenate([1.0 - sel_t, jnp.zeros((LANES - n_sel, NSA_KV_HEADS * TQ), F32)], axis=0)
    unsel = unsel_t.T.astype(BF16)
    for g in range(NSA_GROUP):
        qa_s[pl.ds(g * NSA_KV_HEADS * TQ, NSA_KV_HEADS * TQ), LANES:] = unsel

    fresh = (jnp.full((r, LANES), NEG, F32), jnp.zeros((r, LANES), F32), jnp.zeros((r, LANES), F32))

    del fresh
    m_s[...] = jnp.full(m_s.shape, NEG, F32)
    l_s[...] = jnp.zeros(l_s.shape, F32)
    acc_s[...] = jnp.zeros(acc_s.shape, F32)

    def slc_tile(kt, causal):
        rows = pl.ds(pl.multiple_of(kt * TK_SLC, TK_SLC), TK_SLC)
        k_aug = jnp.concatenate([ks_ref[rows, :], eb_ref[rows, :]], axis=1)
        v_t = vs_ref[rows, :]
        mask = None
        if causal:
            mask = (kt * TK_SLC + lax.broadcasted_iota(jnp.int32, (1, TK_SLC), 1)) <= (t0 + rpos)
        chain = CHAIN_ROWS if causal else r
        for c0 in range(0, r, chain):
            cs = pl.ds(c0, chain)
            state = _attend((m_s[cs, :], l_s[cs, :], acc_s[cs, :]), _dot_nt(qa_s[cs, :], k_aug), v_t, mask)
            m_s[cs, :], l_s[cs, :], acc_s[cs, :] = state

    n_full = t0 // TK_SLC

    def slc_body(kt, c):
        slc_tile(kt, False)
        return c

    lax.fori_loop(0, n_full, slc_body, 0)
    slc_tile(n_full, True)
    o_slc = acc_s[...] / jnp.maximum(l_s[...], 1e-30)

    gt = gate_ref[...]
    g_hi = gt.astype(BF16)
    g_lo = (gt - g_hi.astype(F32)).astype(BF16)
    gexp = _dot(jnp.concatenate([g_hi, g_lo], axis=1), ge_ref[...])
    o_win = owin_s[...]
    for g in range(NSA_GROUP):
        lo = slice((g * NSA_KV_HEADS) * TQ, (g * NSA_KV_HEADS + 1) * TQ)
        hi = slice((g * NSA_KV_HEADS + 1) * TQ, (g * NSA_KV_HEADS + 2) * TQ)
        out = None
        for i, o_i in enumerate((o_cmp, o_slc, o_win)):
            col = (i * NSA_GROUP + g) * LANES
            term = gexp[:, col:col + LANES] * jnp.where(own[0], o_i[lo], o_i[hi])
            out = term if out is None else out + term
        o_ref[:, g * LANES:(g + 1) * LANES] = out.astype(BF16)


def _gate_expand():
    ge = np.zeros((LANES, 3, NSA_GROUP, NSA_KV_HEADS, HEAD_DIM), np.float32)
    for kvh in range(NSA_KV_HEADS):
        for g in range(NSA_GROUP):
            for i in range(3):
                ge[3 * (kvh * NSA_GROUP + g) + i, i, g, kvh, :] = 1.0
    ge = ge.reshape(LANES, 3 * NSA_WIDTH)
    return jnp.asarray(np.concatenate([ge, ge], axis=0)).astype(BF16)


def _block_neg(n_keys):
    k = np.arange(n_keys)[:, None]
    j = np.arange(LANES)[None, :]
    return jnp.asarray(np.where(k // SEL_BLOCK == j, NEG, 0.0).astype(np.float32)).astype(BF16)


def _nsa_prompt(qu, qr, ckv, kvs_b, kvw_b, gates, b_sz, s_len):
    n_cmp_pad = s_len // CMP_STRIDE
    n_sel = s_len // SEL_BLOCK
    assert n_sel <= LANES and n_cmp_pad % LANES == 0 and s_len % TK_SLC == 0
    covt = _cover_t(n_cmp_pad, n_cmp_pad - 1, n_sel, LANES)
    eb = _block_neg(s_len)
    kvw_pad = jnp.pad(kvw_b.reshape(b_sz, s_len, KV_WIDTH), ((0, 0), (WINDOW, 0), (0, 0)))
    kvw_pad = kvw_pad.reshape(b_sz * (s_len + WINDOW), KV_WIDTH)
    nq = s_len // TQ
    r = NSA_HEADS * TQ
    tile = lambda b, q: (b * nq + q, 0)
    const = lambda b, q: (0, 0)
    return _pallas(
        functools.partial(_nsa_prompt_kernel, n_sel=n_sel),
        grid=(b_sz, nq),
        in_specs=[pl.BlockSpec((TQ, NSA_WIDTH), tile), pl.BlockSpec((TQ, NSA_WIDTH), tile),
                  pl.BlockSpec((n_cmp_pad, KV_WIDTH), lambda b, q: (b, 0)),
                  pl.BlockSpec((s_len, LANES), lambda b, q: (b, 0)), pl.BlockSpec((s_len, LANES), lambda b, q: (b, 1)),
                  pl.BlockSpec((s_len + WINDOW, LANES), lambda b, q: (b, 0)),
                  pl.BlockSpec((s_len + WINDOW, LANES), lambda b, q: (b, 1)),
                  pl.BlockSpec((TQ, LANES), tile),
                  pl.BlockSpec((LANES, n_cmp_pad), const), pl.BlockSpec((s_len, LANES), const),
                  pl.BlockSpec((2 * LANES, 3 * NSA_WIDTH), const)],
        out_specs=pl.BlockSpec((TQ, NSA_WIDTH), tile),
        out_shape=jax.ShapeDtypeStruct((b_sz * s_len, NSA_WIDTH), BF16),
        scratch_shapes=[pltpu.VMEM((r, LANES), BF16), pltpu.VMEM((r, 2 * LANES), BF16),
                        pltpu.VMEM((r, LANES), F32), pltpu.VMEM((r, LANES), F32), pltpu.VMEM((r, LANES), F32),
                        pltpu.VMEM((r, LANES), F32)],
        compiler_params=_cparams(("parallel", "arbitrary")),
    )(qu, qr, ckv, kvs_b, kvs_b, kvw_pad, kvw_pad, gates, covt, eb, _gate_expand())


def _block_expand(n_sel_pad, n_keys):
    j = np.arange(n_sel_pad)[:, None]
    k = np.arange(n_keys)[None, :]
    return jnp.asarray((k // SEL_BLOCK == j).astype(np.float32)).astype(BF16)


def _hgrn_consts(c):
    n_lvl = int(math.log2(c))
    hc = HGRN_HEADS * c
    t = np.arange(hc) % c
    head = np.arange(hc) // c
    same_head = head[:, None] == head[None, :]
    u = t[None, :]
    mall = np.zeros(((1 + n_lvl) * hc, hc), np.float32)
    masks = np.zeros((n_lvl, hc, hc), np.float32)
    mall[:hc] = same_head & (u <= t[:, None])
    for li in range(n_lvl):
        n = c >> (li + 1)
        blk = t // (2 * n)
        up_start = blk * 2 * n + n
        upper = t >= up_start
        m_up = upper[:, None] & (u >= up_start[:, None]) & (u <= t[:, None])
        m_lo = (~upper)[:, None] & (u > t[:, None]) & (u < up_start[:, None])
        mall[(1 + li) * hc:(2 + li) * hc] = same_head & (m_up | m_lo)
        masks[li] = same_head & upper[:, None] & (~upper)[None, :] & (blk[:, None] == blk[None, :])
    return jnp.asarray(mall).astype(BF16), jnp.asarray(masks)


def _split2(x):
    a = x.astype(BF16)
    return a, (x - a.astype(F32)).astype(BF16)


def _lower_bound(lbl):
    e = jnp.exp(lbl - jnp.max(lbl, axis=0, keepdims=True))
    return e[0:1] / jnp.sum(e, axis=0, keepdims=True)


def _hgrn_prompt_kernel(hq_ref, hf_ref, hi_ref, hg_ref, lbl_ref, gn_ref, mall_ref, mask_ref, o_ref, s_ref, st_ref,
                        *, c, n_lvl, ts, n_seq):
    si = pl.program_id(1)

    @pl.when(si == 0)
    def _():
        st_ref[...] = jnp.zeros_like(st_ref)

    hc = HGRN_HEADS * c
    lb_all = _lower_bound(lbl_ref[...])
    lb = jnp.concatenate([jnp.broadcast_to(lb_all[:, h * LANES:(h + 1) * LANES], (c, LANES))
                          for h in range(HGRN_HEADS)], axis=0)
    gn = gn_ref[...]
    rowi = lax.broadcasted_iota(jnp.int32, (hc, 1), 0)

    def chunk(ci, carry):
        for sq in range(n_seq):
            seq_chunk(sq, pl.multiple_of(ci * c, c))
        return carry

    def seq_chunk(sq, r0):
        def stacked(ref):
            return jnp.concatenate([ref[sq, pl.ds(r0, c), h * LANES:(h + 1) * LANES] for h in range(HGRN_HEADS)],
                                   axis=0)

        q = jax.nn.silu(stacked(hq_ref)) * HGRN_DK ** -0.5
        f = lb + (1.0 - lb) * jax.nn.sigmoid(stacked(hf_ref))
        k = 1.0 - f
        v = stacked(hi_ref)
        a, b = _split2(jnp.log(f))
        sums = _dot(mall_ref[...], jnp.concatenate([a, b], axis=1))
        sums = sums[:, :LANES] + sums[:, LANES:]
        cum = sums[:hc]
        att = jnp.zeros((hc, hc), F32)
        for li in range(n_lvl):
            n = c >> (li + 1)
            e = jnp.exp(sums[(1 + li) * hc:(2 + li) * hc])
            zz = (jnp.where((rowi & n) != 0, q, k) * e).astype(BF16)
            att = att + _dot_nt(zz, zz) * mask_ref[li]
        o = _dot(att.astype(BF16), v.astype(BF16)) + jnp.sum(q * k, axis=-1, keepdims=True) * v
        qe = (q * jnp.exp(cum)).astype(BF16)
        o_heads = []
        for h in range(HGRN_HEADS):
            rows = slice(h * c, (h + 1) * c)
            st = st_ref[sq, h]
            o_heads.append(o[rows] + _dot_nt(qe[rows], st.astype(BF16)))
            last = cum[(h + 1) * c - 1:(h + 1) * c, :]
            kd = (k[rows] * jnp.exp(last - cum[rows])).astype(BF16)
            st_ref[sq, h] = st * jnp.exp(last) + _dot(v[rows].T.astype(BF16), kd)
        o = jnp.concatenate(o_heads, axis=0)
        y = o * lax.rsqrt(jnp.mean(o * o, axis=-1, keepdims=True) + NORM_EPS) * gn * jax.nn.silu(stacked(hg_ref))
        for h in range(HGRN_HEADS):
            o_ref[sq, pl.ds(r0, c), h * LANES:(h + 1) * LANES] = y[h * c:(h + 1) * c].astype(BF16)

    lax.fori_loop(0, ts // c, chunk, 0, unroll=HG_UNROLL)

    @pl.when(si == pl.num_programs(1) - 1)
    def _():
        for sq in range(n_seq):
            for h in range(HGRN_HEADS):
                s_ref[sq, h] = st_ref[sq, h].T


def _hgrn_prompt(zh, lb_logits, gnorm, b_sz, s_len):
    c = HG_CHUNK
    ts = min(HG_TS, s_len)
    ns = s_len // ts
    n_lvl = int(math.log2(c))
    mall, masks = _hgrn_consts(c)
    n_seq = HG_SEQS if b_sz % HG_SEQS == 0 else 1
    z3 = zh.reshape(b_sz, s_len, zh.shape[1])
    col = lambda k: (lambda b, s: (b, s, k))
    o_b, s_fin = _pallas(
        functools.partial(_hgrn_prompt_kernel, c=c, n_lvl=n_lvl, ts=ts, n_seq=n_seq),
        grid=(b_sz // n_seq, ns),
        in_specs=[pl.BlockSpec((n_seq, ts, HGRN_KW), col(0)), pl.BlockSpec((n_seq, ts, HGRN_KW), col(1)),
                  pl.BlockSpec((n_seq, ts, HGRN_VW), col(2)), pl.BlockSpec((n_seq, ts, HGRN_VW), col(3)),
                  pl.BlockSpec((lb_logits.shape[0], HGRN_KW), lambda b, s: (0, 0)),
                  pl.BlockSpec((1, LANES), lambda b, s: (0, 0)),
                  pl.BlockSpec(mall.shape, lambda b, s: (0, 0)),
                  pl.BlockSpec(masks.shape, lambda b, s: (0, 0, 0))],
        out_specs=[pl.BlockSpec((n_seq, ts, HGRN_VW), lambda b, s: (b, s, 0)),
                   pl.BlockSpec((n_seq, HGRN_HEADS, HGRN_DK, HGRN_DV), lambda b, s: (b, 0, 0, 0))],
        out_shape=[jax.ShapeDtypeStruct((b_sz, s_len, HGRN_VW), BF16),
                   jax.ShapeDtypeStruct((b_sz, HGRN_HEADS, HGRN_DK, HGRN_DV), F32)],
        scratch_shapes=[pltpu.VMEM((n_seq, HGRN_HEADS, HGRN_DV, HGRN_DK), F32)],
        compiler_params=_cparams(("parallel", "arbitrary")),
    )(z3, z3, z3, z3, lb_logits, gnorm, mall, masks)
    return o_b.reshape(b_sz * s_len, HGRN_VW), s_fin


_R_EXP0 = 8


def _post_kernel(x_ref, oa_ref, ob_ref, ma_ref, mb_ref, wpa_ref, wpb_ref, wo_ref, g2_ref, wr_ref, br_ref,
                 x1_ref, h2_ref, lg_ref):
    pa = _dot(oa_ref[...], wpa_ref[...])
    pb = _dot(ob_ref[...], wpb_ref[...])
    merged = jax.nn.sigmoid(ma_ref[...]) * pa + jax.nn.sigmoid(mb_ref[...]) * pb
    x1 = x_ref[...] + _dot(merged.astype(BF16), wo_ref[...])
    h2 = _rms(x1, g2_ref[...])
    x1_ref[...] = x1
    h2_ref[...] = h2
    h_hi = h2.astype(BF16)
    h_lo = (h2 - h_hi.astype(F32)).astype(BF16)
    lg = _dot(jnp.concatenate([h_hi, h_hi, h_lo], axis=1), wr_ref[...]) + br_ref[...]
    lg_ref[...] = lg.T


def _router_weights(w_rg, b_rg, w_re, b_re):
    wr = jnp.zeros((D_MODEL, LANES), F32)
    wr = wr.at[:, :N_GROUPS].set(w_rg)
    wr = wr.at[:, _R_EXP0:_R_EXP0 + N_EXPERTS].set(jnp.transpose(w_re, (1, 0, 2)).reshape(D_MODEL, N_EXPERTS))
    w_hi = wr.astype(BF16)
    w_lo = (wr - w_hi.astype(F32)).astype(BF16)
    br = jnp.zeros((1, LANES), F32)
    br = br.at[0, :N_GROUPS].set(b_rg)
    br = br.at[0, _R_EXP0:_R_EXP0 + N_EXPERTS].set(b_re.reshape(N_EXPERTS))
    return jnp.concatenate([w_hi, w_lo, w_hi], axis=0), br


def _post_mixer(x2, o_a, o_b, zm, wpa, wpb, wo, g2, wrt, br):
    n = x2.shape[0]
    tm = _tile(n, TM_POST)
    row = lambda i: (i, 0)
    const = lambda i: (0, 0)
    return _pallas(
        _post_kernel,
        grid=(n // tm,),
        in_specs=[pl.BlockSpec((tm, D_MODEL), row), pl.BlockSpec((tm, NSA_WIDTH), row),
                  pl.BlockSpec((tm, HGRN_VW), row),
                  pl.BlockSpec((tm, D_MODEL), lambda i: (i, 0)), pl.BlockSpec((tm, D_MODEL), lambda i: (i, 1)),
                  pl.BlockSpec((NSA_WIDTH, D_MODEL), const), pl.BlockSpec((HGRN_VW, D_MODEL), const),
                  pl.BlockSpec((D_MODEL, D_MODEL), const), pl.BlockSpec((1, D_MODEL), const),
                  pl.BlockSpec((3 * D_MODEL, LANES), const), pl.BlockSpec((1, LANES), const)],
        out_specs=[pl.BlockSpec((tm, D_MODEL), row), pl.BlockSpec((tm, D_MODEL), row),
                   pl.BlockSpec((LANES, tm), lambda i: (0, i))],
        out_shape=[jax.ShapeDtypeStruct((n, D_MODEL), F32), jax.ShapeDtypeStruct((n, D_MODEL), F32),
                   jax.ShapeDtypeStruct((LANES, n), F32)],
        compiler_params=_cparams(("parallel",)),
    )(x2, o_a, o_b, zm, zm, wpa, wpb, wo, g2, wrt, br)


def _lowest_argmax(vals, top):
    idx = jnp.full(top.shape, len(vals) - 1, jnp.int32)
    for i in range(len(vals) - 2, -1, -1):
        idx = jnp.where(vals[i] == top, i, idx)
    return idx


def _route_kernel(lg_ref, u_ref, eid_ref, gate_ref, rank_ref, cnt_ref, carry_ref):
    @pl.when(pl.program_id(0) == 0)
    def _():
        carry_ref[...] = jnp.zeros_like(carry_ref)

    lg = lg_ref[...]
    tr = lg.shape[1]
    grp = [lg[g:g + 1] for g in range(N_GROUPS)]
    mx = functools.reduce(jnp.maximum, grp)
    ex = [jnp.exp(v - mx) for v in grp]
    den = functools.reduce(lambda a, b: a + b, ex)
    pr = [e / den for e in ex]
    pg = functools.reduce(jnp.maximum, pr)
    gtop = _lowest_argmax(pr, pg)
    le = []
    for j in range(EXPERTS_PER_GROUP):
        v = lg[_R_EXP0 + j:_R_EXP0 + j + 1]
        for g in range(1, N_GROUPS):
            r = _R_EXP0 + g * EXPERTS_PER_GROUP + j
            v = jnp.where(gtop == g, lg[r:r + 1], v)
        le.append(v)
    m1 = functools.reduce(jnp.maximum, le)
    i1 = _lowest_argmax(le, m1)
    le2 = [jnp.where(i1 == j, -jnp.inf, le[j]) for j in range(EXPERTS_PER_GROUP)]
    m2 = functools.reduce(jnp.maximum, le2)
    i2 = _lowest_argmax(le2, m2)
    e2 = jnp.exp(m2 - m1)
    den2 = 1.0 + e2
    gate1 = (1.0 / den2) * pg
    gate2 = (e2 / den2) * pg
    eid1 = gtop * EXPERTS_PER_GROUP + i1
    eid2 = gtop * EXPERTS_PER_GROUP + i2

    eio = lax.broadcasted_iota(jnp.int32, (N_EXPERTS, tr), 0)
    oh1 = jnp.where(eio == eid1, 1.0, 0.0)
    oh2 = jnp.where(eio == eid2, 1.0, 0.0)
    both = (oh1 + oh2).astype(BF16)
    carry = carry_ref[...]
    before = _dot(both, u_ref[...]) + jnp.concatenate([carry] * (tr // LANES), axis=1)
    rank1 = jnp.sum(oh1 * before, axis=0, keepdims=True).astype(jnp.int32)
    rank2 = jnp.sum(oh2 * before, axis=0, keepdims=True).astype(jnp.int32)
    carry = carry + _dot(both, jnp.ones((tr, LANES), BF16))
    carry_ref[...] = carry
    cnt_ref[...] = carry
    zi = jnp.zeros((6, tr), jnp.int32)
    eid_ref[...] = jnp.concatenate([eid1, eid2, zi], axis=0)
    rank_ref[...] = jnp.concatenate([rank1, rank2, zi], axis=0)
    gate_ref[...] = jnp.concatenate([gate1, gate2, jnp.zeros((6, tr), F32)], axis=0)


def _route(lg):
    n = lg.shape[1]
    tr = _tile(n, TR_ROUTE, LANES)
    u = jnp.asarray(np.triu(np.ones((tr, tr), np.float32), 1)).astype(BF16)
    col = lambda i: (0, i)
    return _pallas(
        _route_kernel,
        grid=(n // tr,),
        in_specs=[pl.BlockSpec((LANES, tr), col), pl.BlockSpec((tr, tr), lambda i: (0, 0))],
        out_specs=[pl.BlockSpec((8, tr), col), pl.BlockSpec((8, tr), col), pl.BlockSpec((8, tr), col),
                   pl.BlockSpec((N_EXPERTS, LANES), lambda i: (0, 0))],
        out_shape=[jax.ShapeDtypeStruct((8, n), jnp.int32), jax.ShapeDtypeStruct((8, n), F32),
                   jax.ShapeDtypeStruct((8, n), jnp.int32), jax.ShapeDtypeStruct((N_EXPERTS, LANES), F32)],
        scratch_shapes=[pltpu.VMEM((N_EXPERTS, LANES), F32)],
        compiler_params=_cparams(("arbitrary",)),
    )(lg, u)


def _dest_kernel(ps_ref, eid_ref, rank_ref, dest_ref):
    eid = eid_ref[...]
    dest = rank_ref[...]
    for e in range(N_EXPERTS):
        dest = dest + jnp.where(eid == e, ps_ref[e], 0)
    dest_ref[...] = dest


def _dest_rows(pad_start, eid, rank):
    n = eid.shape[1]
    tr = _tile(n, 4096, LANES)
    col = lambda i: (0, i)
    return _pallas(
        _dest_kernel,
        grid=(n // tr,),
        in_specs=[pl.BlockSpec(memory_space=pltpu.SMEM), pl.BlockSpec((8, tr), col), pl.BlockSpec((8, tr), col)],
        out_specs=pl.BlockSpec((8, tr), col),
        out_shape=jax.ShapeDtypeStruct((8, n), jnp.int32),
        compiler_params=_cparams(("parallel",)),
    )(pad_start, eid, rank)


DMA_UNROLL = 8


def _issue_rows(t_n, row_copy):
    def issue(t, c):
        for k in range(TOP_K_EXPERTS):
            row_copy(t, k).start()
        return c

    lax.fori_loop(0, t_n, issue, 0, unroll=DMA_UNROLL)


def _wait_rows(t_n, row_copy):
    def wait(t, c):
        for k in range(TOP_K_EXPERTS):
            row_copy(t, k).wait()
        return c

    lax.fori_loop(0, t_n, wait, 0, unroll=DMA_UNROLL)


def _row_loops(t_n, row_copy):
    _issue_rows(t_n, row_copy)
    _wait_rows(t_n, row_copy)


def _dispatch_kernel(dest_ref, h_ref, xb_in_ref, xb_ref, sem):
    del xb_in_ref

    def row_copy(t, k):
        return pltpu.make_async_copy(h_ref.at[pl.ds(t, 1), :], xb_ref.at[pl.ds(dest_ref[k, t], 1), :], sem)

    _row_loops(h_ref.shape[0], row_copy)


def _smem_cols(t):
    return pl.BlockSpec((8, t), lambda i: (0, i), memory_space=pltpu.SMEM)


def _dispatch(dest, h2, rows_total):
    n = h2.shape[0]
    t = _tile(n, T_DISP, LANES)
    xb0 = jnp.zeros((rows_total, D_MODEL), F32)
    return _pallas(
        _dispatch_kernel,
        grid=(n // t,),
        in_specs=[_smem_cols(t), pl.BlockSpec((t, D_MODEL), lambda i: (i, 0)), pl.BlockSpec(memory_space=pl.ANY)],
        out_specs=pl.BlockSpec(memory_space=pl.ANY),
        out_shape=jax.ShapeDtypeStruct((rows_total, D_MODEL), F32),
        scratch_shapes=[pltpu.SemaphoreType.DMA(())],
        input_output_aliases={2: 0},
        compiler_params=_cparams(("arbitrary",)),
    )(dest, h2, xb0)


def _expert_kernel(blk_e_ref, x_ref, wg_ref, wu_ref, wd_ref, y_ref):
    del blk_e_ref
    x = x_ref[...].astype(BF16)
    hid = jax.nn.silu(_dot(x, wg_ref[0])) * _dot(x, wu_ref[0])
    y_ref[...] = _dot(hid.astype(BF16), wd_ref[0])


def _experts(blk_e, xb, wg, wu, wd):
    rows_total = xb.shape[0]
    wsel = lambda i, e: (e[i], 0, 0)
    grid_spec = pltpu.PrefetchScalarGridSpec(
        num_scalar_prefetch=1,
        grid=(rows_total // MOE_BM,),
        in_specs=[pl.BlockSpec((MOE_BM, D_MODEL), lambda i, e: (i, 0)),
                  pl.BlockSpec((1, D_MODEL, D_EXPERT), wsel), pl.BlockSpec((1, D_MODEL, D_EXPERT), wsel),
                  pl.BlockSpec((1, D_EXPERT, D_MODEL), wsel)],
        out_specs=pl.BlockSpec((MOE_BM, D_MODEL), lambda i, e: (i, 0)),
    )
    return _pallas(
        _expert_kernel,
        grid_spec=grid_spec,
        out_shape=jax.ShapeDtypeStruct((rows_total, D_MODEL), F32),
        compiler_params=_cparams(("arbitrary",)),
    )(blk_e, xb, wg, wu, wd)


def _combine_kernel(dest_ref, next_ref, x1_ref, gate_ref, fn_ref, yb_ref, y_ref, buf_ref, sem):
    i = pl.program_id(0)
    slot = i % 2
    t_n = x1_ref.shape[0]

    def row_copy(idx_ref, sl):
        def copy(t, k):
            return pltpu.make_async_copy(yb_ref.at[pl.ds(idx_ref[k, t], 1), :], buf_ref.at[sl, k, pl.ds(t, 1), :],
                                         sem.at[sl])
        return copy

    @pl.when(i == 0)
    def _():
        _issue_rows(t_n, row_copy(dest_ref, 0))

    @pl.when(i + 1 < pl.num_programs(0))
    def _():
        _issue_rows(t_n, row_copy(next_ref, 1 - slot))

    _wait_rows(t_n, row_copy(dest_ref, slot))
    g = gate_ref[...]
    out = x1_ref[...] + g[:, 0:1] * buf_ref[slot, 0] + g[:, 1:2] * buf_ref[slot, 1]
    y_ref[...] = _rms(out, fn_ref[...])


def _combine(dest, x1, gate_t, fnorm, yb):
    n = x1.shape[0]
    t = _tile(n, T_DISP, LANES)
    return _pallas(
        _combine_kernel,
        grid=(n // t,),
        in_specs=[_smem_cols(t),
                  pl.BlockSpec((8, t), lambda i: (0, jnp.minimum(i + 1, n // t - 1)), memory_space=pltpu.SMEM),
                  pl.BlockSpec((t, D_MODEL), lambda i: (i, 0)), pl.BlockSpec((t, 8), lambda i: (i, 0)),
                  pl.BlockSpec((1, D_MODEL), lambda i: (0, 0)), pl.BlockSpec(memory_space=pl.ANY)],
        out_specs=pl.BlockSpec((t, D_MODEL), lambda i: (i, 0)),
        out_shape=jax.ShapeDtypeStruct((n, D_MODEL), F32),
        scratch_shapes=[pltpu.VMEM((2, TOP_K_EXPERTS, t, D_MODEL), F32), pltpu.SemaphoreType.DMA((2,))],
        compiler_params=_cparams(("arbitrary",)),
    )(dest, dest, x1, gate_t, fnorm, yb)


def _moe_and_final(x1, h2, lg, wg, wu, wd, fnorm):
    n = x1.shape[0]
    eid, gate, rank, cnt = _route(lg)
    counts = cnt[:, 0].astype(jnp.int32)
    padded = (counts + MOE_BM - 1) // MOE_BM * MOE_BM
    pad_end = jnp.cumsum(padded)
    pad_start = (pad_end - padded).astype(jnp.int32)
    n_blocks = -(-(n * TOP_K_EXPERTS + N_EXPERTS * (MOE_BM - 1)) // MOE_BM)
    blk_start = jnp.arange(n_blocks, dtype=jnp.int32) * MOE_BM
    blk_e = jnp.minimum(jnp.sum(pad_end[None, :] <= blk_start[:, None], axis=1), N_EXPERTS - 1).astype(jnp.int32)
    dest = _dest_rows(pad_start, eid, rank)
    xb = _dispatch(dest, h2, n_blocks * MOE_BM)
    yb = _experts(blk_e, xb, wg, wu, wd)
    return _combine(dest, x1, gate.T, fnorm, yb)


def _page_stream(pt_ref, pool_ref, buf_ref, sem, n_pages):
    b = pl.program_id(0)
    slot = b % 2

    def page_copy(bb, sl, p):
        col = pl.ds(pl.multiple_of(p * PAGE_SIZE, PAGE_SIZE), PAGE_SIZE)
        return pltpu.make_async_copy(pool_ref.at[pt_ref[bb, p]], buf_ref.at[sl, :, col], sem.at[sl])

    def issue(bb, sl):
        def body(p, c):
            page_copy(bb, sl, p).start()
            return c
        lax.fori_loop(0, n_pages, body, 0, unroll=DMA_UNROLL)

    def wait():
        def body(p, c):
            page_copy(b, slot, p).wait()
            return c
        lax.fori_loop(0, n_pages, body, 0, unroll=DMA_UNROLL)

    @pl.when(b == 0)
    def _():
        issue(0, 0)

    @pl.when(b + 1 < pl.num_programs(0))
    def _():
        issue(b + 1, 1 - slot)

    return slot, wait


TR_BLOCK = 256


def _chunk_perm():
    n_loc = TR_BLOCK // CMP_STRIDE
    r = np.arange(TR_BLOCK)
    perm = np.zeros((TR_BLOCK, TR_BLOCK), np.float32)
    perm[r, (r % n_loc) * CMP_STRIDE + r // n_loc] = 1.0
    return jnp.asarray(perm).astype(BF16)


def _compress_sample_kernel(pt_ref, pool_ref, perm_ref, w1bd_ref, w2bd_ref, c0_ref, b2_ref, out_ref,
                            buf_ref, ak_ref, av_ref, sem, *, n_pages, n_ch):
    slot, wait = _page_stream(pt_ref, pool_ref, buf_ref, sem, n_pages)
    wait()
    n_loc = TR_BLOCK // CMP_STRIDE

    def block(j, carry):
        xt = buf_ref[slot, :, pl.ds(pl.multiple_of(j * TR_BLOCK, TR_BLOCK), TR_BLOCK)].astype(BF16)
        rows = _dot_nt(perm_ref[...], xt)
        dst = pl.ds(pl.multiple_of(j * n_loc, n_loc), n_loc)
        for s in range(CMP_STRIDE):
            piece = rows[s * n_loc:(s + 1) * n_loc]
            ak_ref[dst, s * LANES:(s + 1) * LANES] = piece[:, :LANES].astype(BF16)
            av_ref[dst, s * LANES:(s + 1) * LANES] = piece[:, LANES:].astype(BF16)
        return carry

    lax.fori_loop(0, n_pages * PAGE_SIZE // TR_BLOCK, block, 0, unroll=4)
    for c, a_ref in ((0, ak_ref), (1, av_ref)):
        _compress_mlp(c, a_ref, n_ch, w1bd_ref, w2bd_ref, c0_ref, b2_ref, out_ref)


def _compress_sample(page_table, pool, cw):
    b_sz, n_pages = page_table.shape
    past = n_pages * PAGE_SIZE
    n_ch = past // CMP_STRIDE
    z3 = lambda b, pt: (0, 0, 0)
    specs = [pl.BlockSpec((2, CMP_STRIDE * LANES, 4 * LANES), z3), pl.BlockSpec((2, 2 * LANES, LANES), z3),
             pl.BlockSpec((2, 1, 2 * LANES), z3), pl.BlockSpec((2, 1, LANES), z3)]
    grid_spec = pltpu.PrefetchScalarGridSpec(
        num_scalar_prefetch=1,
        grid=(b_sz,),
        in_specs=[pl.BlockSpec(memory_space=pl.ANY), pl.BlockSpec((TR_BLOCK, TR_BLOCK), lambda b, pt: (0, 0))] + specs,
        out_specs=pl.BlockSpec((n_ch, KV_WIDTH), lambda b, pt: (b, 0)),
        scratch_shapes=[pltpu.VMEM((2, KV_WIDTH, past), F32), pltpu.VMEM((n_ch, CMP_STRIDE * LANES), BF16),
                        pltpu.VMEM((n_ch, CMP_STRIDE * LANES), BF16), pltpu.SemaphoreType.DMA((2,))],
    )
    assert past % TR_BLOCK == 0
    return _pallas(
        functools.partial(_compress_sample_kernel, n_pages=n_pages, n_ch=n_ch),
        grid_spec=grid_spec,
        out_shape=jax.ShapeDtypeStruct((b_sz * n_ch, KV_WIDTH), F32),
        compiler_params=_cparams(("arbitrary",)),
    )(page_table, pool, _chunk_perm(), *cw)


def _cmp_sample_kernel(qu_ref, ckv_ref, covt_ref, ocmp_ref, selt_ref, ps_s, *, past, t_len, n_sel, n_grp, n_cmp_pad):
    r = qu_ref.shape[1]
    rows_kt = NSA_KV_HEADS * t_len
    tpos = past + (lax.broadcasted_iota(jnp.int32, (r, 1), 0) % t_len)
    c_end = lax.broadcasted_iota(jnp.int32, (1, n_cmp_pad), 1) * CMP_STRIDE + (CMP_BLOCK - 1)

    def one_seq(sq, carry):
        ckv = ckv_ref[pl.ds(pl.multiple_of(sq * n_cmp_pad, n_cmp_pad), n_cmp_pad), :]
        p = _masked_softmax(_dot_nt(qu_ref[sq], ckv[:, :LANES].astype(BF16)), c_end <= tpos)
        ocmp_ref[sq] = _dot(p.astype(BF16), ckv[:, LANES:].astype(BF16))
        psum = []
        for kvh in range(NSA_KV_HEADS):
            acc = None
            for g in range(NSA_GROUP):
                blk = g * NSA_KV_HEADS + kvh
                pg = p[blk * t_len:(blk + 1) * t_len]
                acc = pg if acc is None else acc + pg
            psum.append(acc)
        ps_s[pl.ds(pl.multiple_of(sq * rows_kt, rows_kt), rows_kt), :] = jnp.concatenate(psum, axis=0)
        return carry

    lax.fori_loop(0, n_grp, one_seq, 0)
    imp_t = _dot_nt(covt_ref[...], ps_s[...], precision=HIGHEST)
    tp = past + (lax.broadcasted_iota(jnp.int32, imp_t.shape, 1) % t_len)
    selt_ref[0] = _select_t(imp_t, tp, n_sel)


def _cmp_sample(qu32, ckv, b_sz, past, t_len, n_sel, n_sel_pad):
    n_cmp_pad = past // CMP_STRIDE
    covt = _cover_t(n_cmp_pad, n_cmp_pad - 1, n_sel, n_sel_pad)
    r = qu32.shape[1]
    rows_kt = NSA_KV_HEADS * t_len
    n_grp = math.gcd(b_sz, LANES // rows_kt)
    assert rows_kt % 8 == 0
    ocmp, sel_t = _pallas(
        functools.partial(_cmp_sample_kernel, past=past, t_len=t_len, n_sel=n_sel, n_grp=n_grp, n_cmp_pad=n_cmp_pad),
        grid=(b_sz // n_grp,),
        in_specs=[pl.BlockSpec((n_grp, r, LANES), lambda b: (b, 0, 0)),
                  pl.BlockSpec((n_grp * n_cmp_pad, KV_WIDTH), lambda b: (b, 0)),
                  pl.BlockSpec((n_sel_pad, n_cmp_pad), lambda b: (0, 0))],
        out_specs=[pl.BlockSpec((n_grp, r, LANES), lambda b: (b, 0, 0)),
                   pl.BlockSpec((1, n_sel_pad, n_grp * rows_kt), lambda b: (b, 0, 0))],
        out_shape=[jax.ShapeDtypeStruct((b_sz, r, LANES), F32),
                   jax.ShapeDtypeStruct((b_sz // n_grp, n_sel_pad, n_grp * rows_kt), F32)],
        scratch_shapes=[pltpu.VMEM((n_grp * rows_kt, n_cmp_pad), F32)],
        compiler_params=_cparams(("parallel",)),
    )(qu32, ckv, covt)
    sel = sel_t.reshape(b_sz // n_grp, n_sel_pad, n_grp, rows_kt).transpose(0, 2, 3, 1)
    return ocmp, sel.reshape(b_sz, rows_kt, n_sel_pad)


TK_SAMPLE = 1024


def _slc_win_sample_kernel(pt_ref, qr_ref, qa_ref, sel_ref, e_ref, pool_ref, new_s_ref, win_ref, new_w_ref, ocmp_ref,
                           gate_ref, o_ref, buf_ref, sem, *, n_pages, past, t_len, win_buf):
    slot, wait_pages = _page_stream(pt_ref, pool_ref, buf_ref, sem, n_pages)
    qr = qr_ref[0]
    sel = sel_ref[0]
    r = qr.shape[0]
    tpos = past + (lax.broadcasted_iota(jnp.int32, (r, 1), 0) % t_len)
    init = (jnp.full((r, 1), NEG, F32), jnp.zeros((r, 1), F32), jnp.zeros((r, LANES), F32))
    new_i = lax.broadcasted_iota(jnp.int32, (1, LANES), 1)
    new_ok = ((past + new_i) <= tpos) & (new_i < t_len)

    wt = win_ref[0]
    wpos = past - win_buf + lax.broadcasted_iota(jnp.int32, (1, win_buf), 1)
    ok = (wpos >= 0) & (wpos <= tpos) & (wpos > tpos - WINDOW)
    carry = _online_step(_dot(qr, wt[:LANES].astype(BF16)), ok, wt[LANES:].astype(BF16), *init, v_t=True)
    rows = new_w_ref[0]
    ok = new_ok & ((past + new_i) > tpos - WINDOW)
    m, l, acc = _online_step(_dot_nt(qr, rows[:, :LANES].astype(BF16)), ok, rows[:, LANES:].astype(BF16), *carry)
    o_win = acc / jnp.maximum(l, 1e-30)

    wait_pages()

    qa = qa_ref[0]

    def slc_body(kt, carry):
        col = pl.ds(pl.multiple_of(kt * TK_SAMPLE, TK_SAMPLE), TK_SAMPLE)
        k_aug = jnp.concatenate([buf_ref[slot, :LANES, col].astype(BF16), e_ref[:, col]], axis=0)
        return _online_step(_dot(qa, k_aug), None, buf_ref[slot, LANES:, col].astype(BF16), *carry, v_t=True)

    carry = lax.fori_loop(0, past // TK_SAMPLE, slc_body, init, unroll=2)
    rows = new_s_ref[0]
    n_past_blk = past // SEL_BLOCK
    ok = new_ok & (sel[:, n_past_blk:n_past_blk + 1] > 0.5)
    m, l, acc = _online_step(_dot_nt(qr, rows[:, :LANES].astype(BF16)), ok, rows[:, LANES:].astype(BF16), *carry)
    o_slc = acc / jnp.maximum(l, 1e-30)

    g = gate_ref[0]
    o = g[:, 0:1] * ocmp_ref[0] + g[:, 1:2] * o_slc + g[:, 2:3] * o_win
    row = lax.broadcasted_iota(jnp.int32, (r, LANES), 0)
    lane = lax.broadcasted_iota(jnp.int32, (r, LANES), 1)
    o_ref[0] = jnp.where((((row // t_len) % 2) == 0) == (lane < HEAD_DIM), o, 0.0)


def _slc_win_sample(page_table, qr32, sel32, pool, new_s, win_t, new_w, ocmp32, gate32, past, t_len, n_sel_pad):
    b_sz, n_pages = page_table.shape
    win_buf = win_t.shape[2]
    r = qr32.shape[1]
    n_past_blk = past // SEL_BLOCK
    assert n_past_blk <= LANES
    unsel = jnp.pad(1.0 - sel32[:, :, :n_past_blk].astype(F32), ((0, 0), (0, 0), (0, LANES - n_past_blk)))
    qa32 = jnp.concatenate([qr32, unsel.astype(BF16)], axis=2)
    e_mat = _block_neg(past).T
    b3 = lambda b, pt: (b, 0, 0)
    grid_spec = pltpu.PrefetchScalarGridSpec(
        num_scalar_prefetch=1,
        grid=(b_sz,),
        in_specs=[pl.BlockSpec((1, r, LANES), b3), pl.BlockSpec((1, r, 2 * LANES), b3),
                  pl.BlockSpec((1, r, n_sel_pad), b3),
                  pl.BlockSpec((LANES, past), lambda b, pt: (0, 0)),
                  pl.BlockSpec(memory_space=pl.ANY),
                  pl.BlockSpec((1, LANES, KV_WIDTH), b3),
                  pl.BlockSpec((1, KV_WIDTH, win_buf), b3),
                  pl.BlockSpec((1, LANES, KV_WIDTH), b3),
                  pl.BlockSpec((1, r, LANES), b3), pl.BlockSpec((1, r, 3), b3)],
        out_specs=pl.BlockSpec((1, r, LANES), b3),
        scratch_shapes=[pltpu.VMEM((2, KV_WIDTH, past), F32), pltpu.SemaphoreType.DMA((2,))],
    )
    return _pallas(
        functools.partial(_slc_win_sample_kernel, n_pages=n_pages, past=past, t_len=t_len, win_buf=win_buf),
        grid_spec=grid_spec,
        out_shape=jax.ShapeDtypeStruct((b_sz, r, LANES), F32),
        compiler_params=_cparams(("arbitrary",)),
    )(page_table, qr32, qa32, sel32, e_mat, pool, new_s, win_t, new_w, ocmp32, gate32)


def _rows32(q, b_sz, t_len):
    q5 = q.reshape(b_sz, t_len, NSA_GROUP, NSA_KV_HEADS, HEAD_DIM).transpose(0, 2, 3, 1, 4)
    eye = jnp.eye(NSA_KV_HEADS, dtype=q.dtype)
    return jnp.einsum('bgktd,kj->bgktjd', q5, eye).reshape(b_sz, NSA_HEADS * t_len, LANES)


def _from_rows32(o32, b_sz, t_len):
    o6 = o32.reshape(b_sz, NSA_GROUP, NSA_KV_HEADS, t_len, NSA_KV_HEADS, HEAD_DIM)
    o5 = jnp.stack([o6[:, :, k, :, k] for k in range(NSA_KV_HEADS)], axis=2)
    return o5.transpose(0, 3, 1, 2, 4).reshape(b_sz * t_len, NSA_WIDTH)


def _nsa_sample(qu, qr, kvs_new, kvw_new, gates, cache_cmp, cache_slc, cache_win, page_table, cw, t_len):
    b_sz, n_pages = page_table.shape
    past = n_pages * PAGE_SIZE
    total = past + t_len
    assert t_len < CMP_STRIDE and t_len <= SEL_BLOCK and past % TK_SAMPLE == 0
    n_sel = -(-total // SEL_BLOCK)
    n_sel_pad = -(-n_sel // 8) * 8
    pool_c = cache_cmp.reshape(-1, PAGE_SIZE, KV_WIDTH).transpose(0, 2, 1)
    pool_s = cache_slc.reshape(-1, PAGE_SIZE, KV_WIDTH).transpose(0, 2, 1)
    win_t = cache_win.reshape(b_sz, cache_win.shape[1], KV_WIDTH).transpose(0, 2, 1)
    ckv = _compress_sample(page_table, pool_c, cw)
    ocmp32, sel = _cmp_sample(_rows32(qu, b_sz, t_len), ckv, b_sz, past, t_len, n_sel, n_sel_pad)
    sel = sel.reshape(b_sz, 1, NSA_KV_HEADS * t_len, n_sel_pad)
    sel32 = jnp.broadcast_to(sel, (b_sz, NSA_GROUP, NSA_KV_HEADS * t_len, n_sel_pad))
    sel32 = sel32.reshape(b_sz, NSA_HEADS * t_len, n_sel_pad).astype(BF16)
    pad_rows = lambda a: jnp.pad(a.reshape(b_sz, t_len, KV_WIDTH), ((0, 0), (0, LANES - t_len), (0, 0)))
    g4 = gates[:, :3 * NSA_HEADS].reshape(b_sz, t_len, NSA_KV_HEADS, NSA_GROUP, 3)
    gate32 = g4.transpose(0, 3, 2, 1, 4).reshape(b_sz, NSA_HEADS * t_len, 3)
    o32 = _slc_win_sample(page_table, _rows32(qr, b_sz, t_len), sel32, pool_s, pad_rows(kvs_new), win_t,
                          pad_rows(kvw_new), ocmp32, gate32, past, t_len, n_sel_pad)
    return _from_rows32(o32, b_sz, t_len).astype(BF16)


def _hgrn_sample_kernel(hq_ref, hf_ref, vt_ref, hgt_ref, s0_ref, lbl_ref, gnt_ref, ot_ref, s_ref, *, t_len):
    lb = _lower_bound(lbl_ref[...])
    q_all = jax.nn.silu(hq_ref[0]) * HGRN_DK ** -0.5
    f_all = lb + (1.0 - lb) * jax.nn.sigmoid(hf_ref[0])
    for h in range(HGRN_HEADS):
        hs = slice(h * LANES, (h + 1) * LANES)
        q = q_all[:, hs]
        f = f_all[:, hs]
        k = 1.0 - f
        vt = vt_ref[0, h]
        st = s0_ref[0, h].T
        cols = []
        for t in range(t_len):
            st = st * f[t:t + 1] + vt[:, t:t + 1] * k[t:t + 1]
            cols.append(jnp.sum(st * q[t:t + 1], axis=1, keepdims=True))
        ot = jnp.concatenate(cols, axis=1)
        y = ot * lax.rsqrt(jnp.mean(ot * ot, axis=0, keepdims=True) + NORM_EPS) * gnt_ref[...]
        ot_ref[0, h] = y * jax.nn.silu(hgt_ref[0, h])
        s_ref[0, h] = st.T


def _hgrn_sample(zh, s0, lb_logits, gnorm, b_sz, t_len):
    z3 = zh.reshape(b_sz, t_len, 4 * HGRN_KW)
    to_t = lambda a: a.reshape(b_sz, t_len, HGRN_HEADS, HGRN_DV).transpose(0, 2, 3, 1)
    vt = to_t(z3[:, :, 2 * HGRN_KW:3 * HGRN_KW])
    hgt = to_t(z3[:, :, 3 * HGRN_KW:])
    col = lambda k: (lambda b: (b, 0, k))
    b4 = lambda b: (b, 0, 0, 0)
    ot, s_new = _pallas(
        functools.partial(_hgrn_sample_kernel, t_len=t_len),
        grid=(b_sz,),
        in_specs=[pl.BlockSpec((1, t_len, HGRN_KW), col(0)), pl.BlockSpec((1, t_len, HGRN_KW), col(1)),
                  pl.BlockSpec((1, HGRN_HEADS, HGRN_DV, t_len), b4), pl.BlockSpec((1, HGRN_HEADS, HGRN_DV, t_len), b4),
                  pl.BlockSpec((1, HGRN_HEADS, HGRN_DK, HGRN_DV), b4),
                  pl.BlockSpec((lb_logits.shape[0], HGRN_KW), lambda b: (0, 0)),
                  pl.BlockSpec((HGRN_DV, 1), lambda b: (0, 0))],
        out_specs=[pl.BlockSpec((1, HGRN_HEADS, HGRN_DV, t_len), b4),
                   pl.BlockSpec((1, HGRN_HEADS, HGRN_DK, HGRN_DV), b4)],
        out_shape=[jax.ShapeDtypeStruct((b_sz, HGRN_HEADS, HGRN_DV, t_len), F32),
                   jax.ShapeDtypeStruct((b_sz, HGRN_HEADS, HGRN_DK, HGRN_DV), F32)],
        compiler_params=_cparams(("parallel",)),
    )(z3, z3, vt, hgt, s0, lb_logits, gnorm.reshape(HGRN_DV, 1))
    o_b = ot.transpose(0, 3, 1, 2).reshape(b_sz * t_len, HGRN_VW)
    return o_b.astype(BF16), s_new


def kernel(x_prompt, x_sample, cache_cmp_kv, cache_slc_kv, cache_win_kv, state_hgrn, page_table, norm1, w_in, cmp_pe,
           cmp_w1, cmp_b1, cmp_w2, cmp_b2, hgrn_lb_logits, hgrn_gnorm, w_proj_a, w_proj_b, w_out, norm2,
           w_router_group, b_router_group, w_router_expert, b_router_expert, w_exp_gate, w_exp_up, w_exp_down,
           final_norm):
    assert w_in.shape[0] == 1, "one layer"
    b_sz, s_len, _ = x_prompt.shape
    d_sz, t_len, _ = x_sample.shape
    past = page_table.shape[1] * PAGE_SIZE
    n_p = b_sz * s_len
    n_s = d_sz * t_len
    kv_row = (2, NSA_KV_HEADS, HEAD_DIM)

    offs = np.cumsum((0,) + IN_SPLITS)
    w_att = _attn_weight(w_in[0])
    w_h = w_in[0][:, offs[5]:offs[9]].astype(BF16)
    w_m = w_in[0][:, offs[9]:offs[11]].astype(BF16)
    g1 = norm1[0][None]
    cw = _compress_weights(cmp_pe[0], cmp_w1[0], cmp_b1[0], cmp_w2[0], cmp_b2[0])
    perm = _pair_perm()
    wpa = w_proj_a[0][perm].astype(BF16)
    wpb = w_proj_b[0].astype(BF16)
    wo = w_out[0].astype(BF16)
    wrt, br = _router_weights(w_router_group[0], b_router_group[0], w_router_expert[0], b_router_expert[0])
    gn = hgrn_gnorm[0][None]

    xp = x_prompt.reshape(n_p, D_MODEL)
    cos, sin = _rope_tables(jnp.arange(s_len))
    qu, qr, kvc_p, kvs_t, kvw_t, kvs_b, kvw_b, gates, kvc_t = _attn_proj(xp, g1, w_att, cos, sin, seq_len=s_len)
    zh = _norm_proj(xp, g1, w_h)
    zm = _norm_proj(xp, g1, w_m)
    ckv = _compress_prompt(kvc_p, b_sz, s_len, cw)
    o_a = _nsa_prompt(qu, qr, ckv, kvs_b, kvw_b, gates, b_sz, s_len)
    o_b, hg_p = _hgrn_prompt(zh, hgrn_lb_logits, gn, b_sz, s_len)
    x1_p, h2_p, lg_p = _post_mixer(xp, o_a, o_b, zm, wpa, wpb, wo, norm2[0][None], wrt, br)

    xs = x_sample.reshape(n_s, D_MODEL)
    cos, sin = _rope_tables(past + jnp.tile(jnp.arange(t_len), d_sz))
    qu, qr, kvc_s, kvs_s, kvw_s, _, _, gates = _attn_proj(xs, g1, w_att, cos, sin)
    zh = _norm_proj(xs, g1, w_h)
    zm = _norm_proj(xs, g1, w_m)
    o_a = _nsa_sample(qu, qr, kvs_s, kvw_s, gates, cache_cmp_kv[0], cache_slc_kv[0], cache_win_kv[0], page_table, cw,
                      t_len)
    o_b, hg_s = _hgrn_sample(zh, state_hgrn[0], hgrn_lb_logits, hgrn_gnorm[0], d_sz, t_len)
    x1_s, h2_s, lg_s = _post_mixer(xs, o_a, o_b, zm, wpa, wpb, wo, norm2[0][None], wrt, br)

    w_exp = (w_exp_gate[0].astype(BF16), w_exp_up[0].astype(BF16), w_exp_down[0].astype(BF16))
    y_p = _moe_and_final(x1_p, h2_p, lg_p, *w_exp, final_norm[None]).reshape(b_sz, s_len, D_MODEL)
    y_s = _moe_and_final(x1_s, h2_s, lg_s, *w_exp, final_norm[None]).reshape(d_sz, t_len, D_MODEL)

    def rows6(a_t):
        return a_t.reshape((a_t.shape[0],) + kv_row + (a_t.shape[2],)).transpose(0, 4, 1, 2, 3)[None]

    win_p = kvw_t[:, :, s_len - min(WINDOW, s_len):]
    win_rows = jnp.concatenate([cache_win_kv[0], kvw_s.reshape((d_sz, t_len) + kv_row)], axis=1)
    win_s = win_rows[:, win_rows.shape[1] - min(WINDOW, past + t_len):]
    return (y_p, y_s,
            rows6(kvc_t), kvc_s.reshape((1, d_sz, t_len) + kv_row),
            rows6(kvs_t), kvs_s.reshape((1, d_sz, t_len) + kv_row),
            rows6(win_p), win_s[None], hg_p[None], hg_s[None])
```

```python
import functools
import math

import numpy as np
import jax
import jax.numpy as jnp
from jax import lax
from jax.experimental import pallas as pl
from jax.experimental.pallas import tpu as pltpu

D_MODEL = 1024
PAGE_SIZE = 128
NSA_HEADS = 8
NSA_KV_HEADS = 2
NSA_GROUP = NSA_HEADS // NSA_KV_HEADS
HEAD_DIM = 64
CMP_BLOCK = 32
CMP_STRIDE = 16
CMP_HIDDEN = 2 * HEAD_DIM
SEL_BLOCK = 64
N_SEL = 16
WINDOW = 512
ROPE_THETA = 10000.0
HGRN_HEADS = 4
HGRN_DK = 128
HGRN_DV = 128
N_GROUPS = 4
EXPERTS_PER_GROUP = 8
N_EXPERTS = N_GROUPS * EXPERTS_PER_GROUP
TOP_K_EXPERTS = 2
D_EXPERT = 512
NSA_WIDTH = NSA_HEADS * HEAD_DIM
KV_WIDTH = 2 * NSA_KV_HEADS * HEAD_DIM
HGRN_KW = HGRN_HEADS * HGRN_DK
HGRN_VW = HGRN_HEADS * HGRN_DV
IN_SPLITS = (NSA_WIDTH, KV_WIDTH, KV_WIDTH, KV_WIDTH, 3 * NSA_HEADS, HGRN_KW, HGRN_KW, HGRN_VW, HGRN_VW,
             D_MODEL, D_MODEL)
NORM_EPS = 1e-6
NEG = -1e9
BIG = 1e9

LANES = 128
VMEM_LIMIT = 56 * 1024 * 1024
TM_PROJ = 512
TQ = 256
TK_SLC = 512
HG_CHUNK = 64
HG_TS = 256
HG_SEQS = 2
HG_UNROLL = 2
TM_POST = 256
TR_ROUTE = 512
MOE_BM = 256
T_DISP = 256

F32 = jnp.float32
BF16 = jnp.bfloat16
HIGHEST = lax.Precision.HIGHEST


def _dot(a, b, precision=None):
    return jnp.dot(a, b, preferred_element_type=F32, precision=precision)


def _dot_nt(a, b, precision=None):
    return lax.dot_general(a, b, (((1,), (1,)), ((), ())), preferred_element_type=F32, precision=precision)


def _tile(n, target, align=8):
    if n <= target:
        return n
    t = target - target % align
    while n % t:
        t -= align
    return t


def _pallas(body, **kw):
    fn = getattr(body, "func", body)
    return pl.pallas_call(body, name=fn.__name__.strip("_").removesuffix("_kernel"), **kw)


def _cparams(sem):
    return pltpu.CompilerParams(dimension_semantics=sem, vmem_limit_bytes=VMEM_LIMIT)


def _masked_softmax(s, ok):
    m = jnp.max(jnp.where(ok, s, NEG), axis=-1, keepdims=True)
    e = jnp.exp2(jnp.where(ok, s - m, NEG))
    return e / jnp.maximum(jnp.sum(e, axis=-1, keepdims=True), 1e-30)


def _rms(x, g):
    return x * lax.rsqrt(jnp.mean(x * x, axis=-1, keepdims=True) + NORM_EPS) * g


_A_Q, _A_QR, _A_KC, _A_KS, _A_KSR, _A_KW, _A_KWR, _A_G, _A_END = 0, 512, 1024, 1280, 1536, 1664, 1920, 2048, 2176


def _attn_proj_kernel(x_ref, g_ref, w_ref, cos_ref, sin_ref,
                      qu_ref, qr_ref, kvc_ref, kvs_ref, kvw_ref, kvsb_ref, kvwb_ref, gate_ref, *kvct_ref, kv_t):
    h = _rms(x_ref[...], g_ref[...]).astype(BF16)
    z = _dot(h, w_ref[...])
    cos = cos_ref[...]
    sin = sin_ref[...]
    q = z[:, _A_Q:_A_QR]
    qu_ref[...] = q.astype(BF16)
    qr_ref[...] = (q * cos + z[:, _A_QR:_A_KC] * sin).astype(BF16)
    kvc = z[:, _A_KC:_A_KS]
    kvc_ref[...] = kvc
    ck = cos[:, :LANES]
    sk = sin[:, :LANES]
    ks = z[:, _A_KS:_A_KS + 128] * ck + z[:, _A_KSR:_A_KW] * sk
    vs = z[:, _A_KS + 128:_A_KSR]
    kvsb_ref[:, :128] = ks.astype(BF16)
    kvsb_ref[:, 128:] = vs.astype(BF16)
    kw = z[:, _A_KW:_A_KW + 128] * ck + z[:, _A_KWR:_A_G] * sk
    vw = z[:, _A_KW + 128:_A_KWR]
    kvwb_ref[:, :128] = kw.astype(BF16)
    kvwb_ref[:, 128:] = vw.astype(BF16)
    if kv_t:
        kvct_ref[0][0] = kvc.T
        kvs_ref[0, :128, :] = ks.T
        kvs_ref[0, 128:, :] = vs.T
        kvw_ref[0, :128, :] = kw.T
        kvw_ref[0, 128:, :] = vw.T
    else:
        kvs_ref[:, :128] = ks
        kvs_ref[:, 128:] = vs
        kvw_ref[:, :128] = kw
        kvw_ref[:, 128:] = vw
    gate_ref[...] = jax.nn.sigmoid(z[:, _A_G:_A_END])


def _rot_cols(w):
    d, n = w.shape
    w4 = w.reshape(d, n // HEAD_DIM, 2, HEAD_DIM // 2)
    return jnp.stack([-w4[:, :, 1], w4[:, :, 0]], axis=2).reshape(d, n)


def _pair_perm():
    idx = []
    for g in range(NSA_GROUP):
        for kvh in range(NSA_KV_HEADS):
            h = kvh * NSA_GROUP + g
            idx.extend(range(h * HEAD_DIM, (h + 1) * HEAD_DIM))
    return np.asarray(idx, np.int32)


def _attn_weight(w_in):
    offs = np.cumsum((0,) + IN_SPLITS)
    scale = HEAD_DIM ** -0.5 * math.log2(math.e)
    wq = w_in[:, offs[0]:offs[1]][:, _pair_perm()] * scale
    wkc = w_in[:, offs[1]:offs[2]]
    wks = w_in[:, offs[2]:offs[3]]
    wkw = w_in[:, offs[3]:offs[4]]
    wg = w_in[:, offs[4]:offs[5]]
    wg = jnp.pad(wg, ((0, 0), (0, LANES - wg.shape[1])))
    half = KV_WIDTH // 2
    w = jnp.concatenate([wq, _rot_cols(wq), wkc, wks, _rot_cols(wks[:, :half]), wkw, _rot_cols(wkw[:, :half]), wg],
                        axis=1)
    return w.astype(BF16)


def _rope_tables(pos):
    half = HEAD_DIM // 2
    inv = ROPE_THETA ** (-jnp.arange(half, dtype=F32) / half)
    ang = pos.astype(F32)[:, None] * inv[None, :]
    cos = jnp.tile(jnp.cos(ang), (1, 2 * NSA_HEADS))
    sin = jnp.tile(jnp.sin(ang), (1, 2 * NSA_HEADS))
    return cos, sin


def _attn_proj(x2, g, w_att, cos, sin, seq_len=None):
    n = x2.shape[0]
    tm = _tile(n, TM_PROJ)
    n_tab = cos.shape[0] // tm
    row = lambda i: (i, 0)
    tab = lambda i: (i % n_tab, 0)
    const = lambda i: (0, 0)
    row_major = lambda width, dt: (jax.ShapeDtypeStruct((n, width), dt), pl.BlockSpec((tm, width), row))
    outs = [row_major(NSA_WIDTH, BF16), row_major(NSA_WIDTH, BF16), row_major(KV_WIDTH, F32), row_major(KV_WIDTH, F32),
            row_major(KV_WIDTH, F32), row_major(KV_WIDTH, BF16), row_major(KV_WIDTH, BF16), row_major(LANES, F32)]
    if seq_len is not None:
        assert seq_len % tm == 0 and tm % LANES == 0
        per_seq = seq_len // tm
        pos_minor = (jax.ShapeDtypeStruct((n // seq_len, KV_WIDTH, seq_len), F32),
                     pl.BlockSpec((1, KV_WIDTH, tm), lambda i: (i // per_seq, 0, i % per_seq)))
        outs[3] = outs[4] = pos_minor
        outs.append(pos_minor)
    return _pallas(
        functools.partial(_attn_proj_kernel, kv_t=seq_len is not None),
        grid=(n // tm,),
        in_specs=[pl.BlockSpec((tm, D_MODEL), row), pl.BlockSpec((1, D_MODEL), const),
                  pl.BlockSpec((D_MODEL, _A_END), const),
                  pl.BlockSpec((tm, NSA_WIDTH), tab), pl.BlockSpec((tm, NSA_WIDTH), tab)],
        out_specs=[spec for _, spec in outs],
        out_shape=[shape for shape, _ in outs],
        compiler_params=_cparams(("parallel",)),
    )(x2, g, w_att, cos, sin)


def _norm_proj_kernel(x_ref, g_ref, w_ref, o_ref):
    h = _rms(x_ref[...], g_ref[...]).astype(BF16)
    o_ref[...] = _dot(h, w_ref[...])


def _norm_proj(x2, g, w):
    n = x2.shape[0]
    tm = _tile(n, TM_PROJ)
    width = w.shape[1]
    return _pallas(
        _norm_proj_kernel,
        grid=(n // tm,),
        in_specs=[pl.BlockSpec((tm, D_MODEL), lambda i: (i, 0)), pl.BlockSpec((1, D_MODEL), lambda i: (0, 0)),
                  pl.BlockSpec((D_MODEL, width), lambda i: (0, 0))],
        out_specs=pl.BlockSpec((tm, width), lambda i: (i, 0)),
        out_shape=jax.ShapeDtypeStruct((n, width), F32),
        compiler_params=_cparams(("parallel",)),
    )(x2, g, w)


def _compress_weights(cmp_pe, cmp_w1, cmp_b1, cmp_w2, cmp_b2):
    w1r = cmp_w1.reshape(2, 2, CMP_STRIDE, HEAD_DIM, CMP_HIDDEN)
    eye = jnp.eye(NSA_KV_HEADS, dtype=F32)
    w1bd = jnp.einsum('casdh,kj->cskdajh', w1r, eye).reshape(2, CMP_STRIDE * NSA_KV_HEADS * HEAD_DIM,
                                                             2 * NSA_KV_HEADS * CMP_HIDDEN)
    w2bd = jnp.einsum('chd,kj->ckhjd', cmp_w2, eye).reshape(2, NSA_KV_HEADS * CMP_HIDDEN, NSA_KV_HEADS * HEAD_DIM)
    pe8 = jnp.broadcast_to(cmp_pe.reshape(2, 1, CMP_BLOCK * HEAD_DIM), (2, 8, CMP_BLOCK * HEAD_DIM))
    b1t = jnp.tile(cmp_b1[:, None, :], (1, 1, NSA_KV_HEADS))
    b2t = jnp.tile(cmp_b2[:, None, :], (1, 1, NSA_KV_HEADS))
    c0 = _pallas(
        _compress_bias_kernel,
        out_shape=jax.ShapeDtypeStruct((2, 1, NSA_KV_HEADS * CMP_HIDDEN), F32),
    )(pe8, cmp_w1, b1t)
    return w1bd.astype(BF16), w2bd.astype(BF16), c0, b2t


def _compress_bias_kernel(pe8_ref, w1_ref, b1_ref, c0_ref):
    for c in range(2):
        c0 = _dot(pe8_ref[c], w1_ref[c], precision=HIGHEST)[0:1, :]
        c0_ref[c] = jnp.concatenate([c0] * NSA_KV_HEADS, axis=1) + b1_ref[c]


def _compress_rows(rows_refs, n_ch, a_ref, w1bd_ref, w2bd_ref, c0_ref, b2_ref, out_ref):
    for c in range(2):
        for s in range(CMP_STRIDE):
            a_ref[:, s * LANES:(s + 1) * LANES] = rows_refs[c][pl.ds(s, n_ch, stride=CMP_STRIDE), :].astype(BF16)
        _compress_mlp(c, a_ref, n_ch, w1bd_ref, w2bd_ref, c0_ref, b2_ref, out_ref)


def _compress_mlp(c, a_ref, n_ch, w1bd_ref, w2bd_ref, c0_ref, b2_ref, out_ref):
    hcat = _dot(a_ref[...], w1bd_ref[c])
    h_b = pltpu.roll(hcat[:, 2 * LANES:], n_ch - 1, 0)
    hid = jax.nn.gelu(hcat[:, :2 * LANES] + h_b + c0_ref[c])
    out_ref[:, c * LANES:(c + 1) * LANES] = _dot(hid.astype(BF16), w2bd_ref[c]) + b2_ref[c]


def _compress_prompt_kernel(k_ref, v_ref, w1bd_ref, w2bd_ref, c0_ref, b2_ref, out_ref, a_ref, *, n_ch):
    _compress_rows((k_ref, v_ref), n_ch, a_ref, w1bd_ref, w2bd_ref, c0_ref, b2_ref, out_ref)


def _cw_specs():
    z3 = lambda *a: (0, 0, 0)
    return [pl.BlockSpec((2, CMP_STRIDE * LANES, 4 * LANES), z3), pl.BlockSpec((2, 2 * LANES, LANES), z3),
            pl.BlockSpec((2, 1, 2 * LANES), z3), pl.BlockSpec((2, 1, LANES), z3)]


def _compress_prompt(kvc, b_sz, s_len, cw):
    n_ch = s_len // CMP_STRIDE
    return _pallas(
        functools.partial(_compress_prompt_kernel, n_ch=n_ch),
        grid=(b_sz,),
        in_specs=[pl.BlockSpec((s_len, LANES), lambda b: (b, 0)), pl.BlockSpec((s_len, LANES), lambda b: (b, 1))]
        + _cw_specs(),
        out_specs=pl.BlockSpec((n_ch, KV_WIDTH), lambda b: (b, 0)),
        out_shape=jax.ShapeDtypeStruct((b_sz * n_ch, KV_WIDTH), F32),
        scratch_shapes=[pltpu.VMEM((n_ch, CMP_STRIDE * LANES), BF16)],
        compiler_params=_cparams(("parallel",)),
    )(kvc, kvc, *cw)


def _cover_t(n_cmp_pad, n_cmp, n_sel, n_sel_pad):
    i = np.arange(n_cmp_pad)[None, :]
    j = np.arange(n_sel_pad)[:, None]
    start = i * CMP_STRIDE
    m = (start < (j + 1) * SEL_BLOCK) & (start + CMP_BLOCK > j * SEL_BLOCK) & (i < n_cmp) & (j < n_sel)
    return jnp.asarray(m.astype(np.float32))


def _select_t(imp_t, tp, n_sel):
    j = lax.broadcasted_iota(jnp.int32, imp_t.shape, 0)
    cur = tp // SEL_BLOCK
    valid = j * SEL_BLOCK <= tp
    forced = (j == 0) | (j == cur) | (j == cur - 1)
    score = jnp.where(valid, jnp.where(forced, BIG, imp_t), NEG)
    cnt = jnp.zeros(imp_t.shape, F32)
    for jp in range(n_sel):
        row = score[jp:jp + 1, :]
        cnt = cnt + jnp.where(j > jp, jnp.where(row >= score, 1.0, 0.0), jnp.where(row > score, 1.0, 0.0))
    return jnp.where((cnt < min(N_SEL, n_sel)) & valid, 1.0, 0.0)


def _online_step(s, ok, v, m, l, acc, v_t=False):
    if ok is None:
        m_new = jnp.maximum(m, jnp.max(s, axis=-1, keepdims=True))
        e = jnp.exp2(s - m_new)
    else:
        m_new = jnp.maximum(m, jnp.max(jnp.where(ok, s, NEG), axis=-1, keepdims=True))
        e = jnp.where(ok, jnp.exp2(s - m_new), 0.0)
    alpha = jnp.exp2(m - m_new)
    l = alpha * l + jnp.sum(e, axis=-1, keepdims=True)
    acc = alpha * acc + (_dot_nt if v_t else _dot)(e.astype(BF16), v)
    return m_new, l, acc


WIN_TILES = ((0, WINDOW + TQ),)
CHAIN_ROWS = 512


def _attend(state, s, v_t, mask):
    m_old, l_old, acc_old = state
    rows, tk = s.shape
    if mask is not None:
        s = jnp.where(mask[None], s.reshape(rows // TQ, TQ, tk), NEG).reshape(rows, tk)
    m_new = jnp.maximum(m_old, jnp.broadcast_to(jnp.max(s, axis=-1, keepdims=True), (rows, LANES)))
    e = jnp.concatenate([jnp.exp2(s[:, c:c + LANES] - m_new) for c in range(0, tk, LANES)], axis=1)
    ones = jnp.ones((tk, LANES), BF16)
    pv = _dot(e.astype(BF16), jnp.concatenate([v_t, ones], axis=1))
    alpha = jnp.exp2(m_old - m_new)
    return m_new, alpha * l_old + pv[:, LANES:], alpha * acc_old + pv[:, :LANES]


def _nsa_prompt_kernel(qu_ref, qr_ref, ckv_ref, ks_ref, vs_ref, kw_ref, vw_ref, gate_ref, covt_ref, eb_ref, ge_ref,
                       o_ref, qu_s, qa_s, m_s, l_s, acc_s, owin_s, *, n_sel):
    qi = pl.program_id(1)
    t0 = qi * TQ
    r = NSA_HEADS * TQ
    lane = lax.broadcasted_iota(jnp.int32, (TQ, LANES), 1)
    own = [lane < HEAD_DIM, lane >= HEAD_DIM]
    rpos = lax.broadcasted_iota(jnp.int32, (TQ, 1), 0)
    for g in range(NSA_GROUP):
        qug = qu_ref[:, g * LANES:(g + 1) * LANES]
        qrg = qr_ref[:, g * LANES:(g + 1) * LANES]
        for kvh in range(NSA_KV_HEADS):
            rows = pl.ds((g * NSA_KV_HEADS + kvh) * TQ, TQ)
            qu_s[rows, :] = jnp.where(own[kvh], qug, jnp.zeros_like(qug))
            qa_s[rows, :LANES] = jnp.where(own[kvh], qrg, jnp.zeros_like(qrg))

    for c0 in range(0, r, CHAIN_ROWS):
        cs = pl.ds(c0, CHAIN_ROWS)
        state = (jnp.full((CHAIN_ROWS, LANES), NEG, F32), jnp.zeros((CHAIN_ROWS, LANES), F32),
                 jnp.zeros((CHAIN_ROWS, LANES), F32))
        for off, tk in WIN_TILES:
            rows = pl.ds(pl.multiple_of(t0 + off, TQ), tk)
            kpos = t0 + off - WINDOW + lax.broadcasted_iota(jnp.int32, (1, tk), 1)
            mask = (kpos >= 0) & (kpos <= t0 + rpos) & (kpos > t0 + rpos - WINDOW)
            state = _attend(state, _dot_nt(qa_s[cs, :LANES], kw_ref[rows, :]), vw_ref[rows, :], mask)
        owin_s[cs, :] = state[2] / jnp.maximum(state[1], 1e-30)

    ckv = ckv_ref[...]
    n_cmp_pad = ckv.shape[0]
    c_ok = (lax.broadcasted_iota(jnp.int32, (1, n_cmp_pad), 1) * CMP_STRIDE + (CMP_BLOCK - 1)) <= (t0 + rpos)
    s = _dot_nt(qu_s[...], ckv[:, :LANES].astype(BF16)).reshape(NSA_HEADS, TQ, n_cmp_pad)
    p = _masked_softmax(s, c_ok[None])
    o_cmp = _dot(p.reshape(r, n_cmp_pad).astype(BF16), ckv[:, LANES:].astype(BF16))
    p4 = p.reshape(NSA_GROUP, NSA_KV_HEADS * TQ, n_cmp_pad)
    psum = (p4[0] + p4[1]) + (p4[2] + p4[3])
    imp_t = _dot_nt(covt_ref[...], psum, precision=HIGHEST)
    tp = t0 + (lax.broadcasted_iota(jnp.int32, (n_sel, NSA_KV_HEADS * TQ), 1) % TQ)
    sel_t = _select_t(imp_t[:n_sel], tp, n_sel)
    unsel_t = jnp.concatenate([1.0 - sel_t, jnp.zeros((LANES - n_sel, NSA_KV_HEADS * TQ), F32)], axis=0)
    unsel = unsel_t.T.astype(BF16)
    for g in range(NSA_GROUP):
        qa_s[pl.ds(g * NSA_KV_HEADS * TQ, NSA_KV_HEADS * TQ), LANES:] = unsel

    fresh = (jnp.full((r, LANES), NEG, F32), jnp.zeros((r, LANES), F32), jnp.zeros((r, LANES), F32))

    del fresh
    m_s[...] = jnp.full(m_s.shape, NEG, F32)
    l_s[...] = jnp.zeros(l_s.shape, F32)
    acc_s[...] = jnp.zeros(acc_s.shape, F32)

    def slc_tile(kt, causal):
        rows = pl.ds(pl.multiple_of(kt * TK_SLC, TK_SLC), TK_SLC)
        k_aug = jnp.concatenate([ks_ref[rows, :], eb_ref[rows, :]], axis=1)
        v_t = vs_ref[rows, :]
        mask = None
        if causal:
            mask = (kt * TK_SLC + lax.broadcasted_iota(jnp.int32, (1, TK_SLC), 1)) <= (t0 + rpos)
        chain = CHAIN_ROWS if causal else r
        for c0 in range(0, r, chain):
            cs = pl.ds(c0, chain)
            state = _attend((m_s[cs, :], l_s[cs, :], acc_s[cs, :]), _dot_nt(qa_s[cs, :], k_aug), v_t, mask)
            m_s[cs, :], l_s[cs, :], acc_s[cs, :] = state

    n_full = t0 // TK_SLC

    def slc_body(kt, c):
        slc_tile(kt, False)
        return c

    lax.fori_loop(0, n_full, slc_body, 0)
    slc_tile(n_full, True)
    o_slc = acc_s[...] / jnp.maximum(l_s[...], 1e-30)

    gt = gate_ref[...]
    g_hi = gt.astype(BF16)
    g_lo = (gt - g_hi.astype(F32)).astype(BF16)
    gexp = _dot(jnp.concatenate([g_hi, g_lo], axis=1), ge_ref[...])
    o_win = owin_s[...]
    for g in range(NSA_GROUP):
        lo = slice((g * NSA_KV_HEADS) * TQ, (g * NSA_KV_HEADS + 1) * TQ)
        hi = slice((g * NSA_KV_HEADS + 1) * TQ, (g * NSA_KV_HEADS + 2) * TQ)
        out = None
        for i, o_i in enumerate((o_cmp, o_slc, o_win)):
            col = (i * NSA_GROUP + g) * LANES
            term = gexp[:, col:col + LANES] * jnp.where(own[0], o_i[lo], o_i[hi])
            out = term if out is None else out + term
        o_ref[:, g * LANES:(g + 1) * LANES] = out.astype(BF16)


def _gate_expand():
    ge = np.zeros((LANES, 3, NSA_GROUP, NSA_KV_HEADS, HEAD_DIM), np.float32)
    for kvh in range(NSA_KV_HEADS):
        for g in range(NSA_GROUP):
            for i in range(3):
                ge[3 * (kvh * NSA_GROUP + g) + i, i, g, kvh, :] = 1.0
    ge = ge.reshape(LANES, 3 * NSA_WIDTH)
    return jnp.asarray(np.concatenate([ge, ge], axis=0)).astype(BF16)


def _block_neg(n_keys):
    k = np.arange(n_keys)[:, None]
    j = np.arange(LANES)[None, :]
    return jnp.asarray(np.where(k // SEL_BLOCK == j, NEG, 0.0).astype(np.float32)).astype(BF16)


def _nsa_prompt(qu, qr, ckv, kvs_b, kvw_b, gates, b_sz, s_len):
    n_cmp_pad = s_len // CMP_STRIDE
    n_sel = s_len // SEL_BLOCK
    assert n_sel <= LANES and n_cmp_pad % LANES == 0 and s_len % TK_SLC == 0
    covt = _cover_t(n_cmp_pad, n_cmp_pad - 1, n_sel, LANES)
    eb = _block_neg(s_len)
    kvw_pad = jnp.pad(kvw_b.reshape(b_sz, s_len, KV_WIDTH), ((0, 0), (WINDOW, 0), (0, 0)))
    kvw_pad = kvw_pad.reshape(b_sz * (s_len + WINDOW), KV_WIDTH)
    nq = s_len // TQ
    r = NSA_HEADS * TQ
    tile = lambda b, q: (b * nq + q, 0)
    const = lambda b, q: (0, 0)
    return _pallas(
        functools.partial(_nsa_prompt_kernel, n_sel=n_sel),
        grid=(b_sz, nq),
        in_specs=[pl.BlockSpec((TQ, NSA_WIDTH), tile), pl.BlockSpec((TQ, NSA_WIDTH), tile),
                  pl.BlockSpec((n_cmp_pad, KV_WIDTH), lambda b, q: (b, 0)),
                  pl.BlockSpec((s_len, LANES), lambda b, q: (b, 0)), pl.BlockSpec((s_len, LANES), lambda b, q: (b, 1)),
                  pl.BlockSpec((s_len + WINDOW, LANES), lambda b, q: (b, 0)),
                  pl.BlockSpec((s_len + WINDOW, LANES), lambda b, q: (b, 1)),
                  pl.BlockSpec((TQ, LANES), tile),
                  pl.BlockSpec((LANES, n_cmp_pad), const), pl.BlockSpec((s_len, LANES), const),
                  pl.BlockSpec((2 * LANES, 3 * NSA_WIDTH), const)],
        out_specs=pl.BlockSpec((TQ, NSA_WIDTH), tile),
        out_shape=jax.ShapeDtypeStruct((b_sz * s_len, NSA_WIDTH), BF16),
        scratch_shapes=[pltpu.VMEM((r, LANES), BF16), pltpu.VMEM((r, 2 * LANES), BF16),
                        pltpu.VMEM((r, LANES), F32), pltpu.VMEM((r, LANES), F32), pltpu.VMEM((r, LANES), F32),
                        pltpu.VMEM((r, LANES), F32)],
        compiler_params=_cparams(("parallel", "arbitrary")),
    )(qu, qr, ckv, kvs_b, kvs_b, kvw_pad, kvw_pad, gates, covt, eb, _gate_expand())


def _block_expand(n_sel_pad, n_keys):
    j = np.arange(n_sel_pad)[:, None]
    k = np.arange(n_keys)[None, :]
    return jnp.asarray((k // SEL_BLOCK == j).astype(np.float32)).astype(BF16)


def _hgrn_consts(c):
    n_lvl = int(math.log2(c))
    hc = HGRN_HEADS * c
    t = np.arange(hc) % c
    head = np.arange(hc) // c
    same_head = head[:, None] == head[None, :]
    u = t[None, :]
    mall = np.zeros(((1 + n_lvl) * hc, hc), np.float32)
    masks = np.zeros((n_lvl, hc, hc), np.float32)
    mall[:hc] = same_head & (u <= t[:, None])
    for li in range(n_lvl):
        n = c >> (li + 1)
        blk = t // (2 * n)
        up_start = blk * 2 * n + n
        upper = t >= up_start
        m_up = upper[:, None] & (u >= up_start[:, None]) & (u <= t[:, None])
        m_lo = (~upper)[:, None] & (u > t[:, None]) & (u < up_start[:, None])
        mall[(1 + li) * hc:(2 + li) * hc] = same_head & (m_up | m_lo)
        masks[li] = same_head & upper[:, None] & (~upper)[None, :] & (blk[:, None] == blk[None, :])
    return jnp.asarray(mall).astype(BF16), jnp.asarray(masks)


def _split2(x):
    a = x.astype(BF16)
    return a, (x - a.astype(F32)).astype(BF16)


def _lower_bound(lbl):
    e = jnp.exp(lbl - jnp.max(lbl, axis=0, keepdims=True))
    return e[0:1] / jnp.sum(e, axis=0, keepdims=True)


def _hgrn_prompt_kernel(hq_ref, hf_ref, hi_ref, hg_ref, lbl_ref, gn_ref, mall_ref, mask_ref, o_ref, s_ref, st_ref,
                        *, c, n_lvl, ts, n_seq):
    si = pl.program_id(1)

    @pl.when(si == 0)
    def _():
        st_ref[...] = jnp.zeros_like(st_ref)

    hc = HGRN_HEADS * c
    lb_all = _lower_bound(lbl_ref[...])
    lb = jnp.concatenate([jnp.broadcast_to(lb_all[:, h * LANES:(h + 1) * LANES], (c, LANES))
                          for h in range(HGRN_HEADS)], axis=0)
    gn = gn_ref[...]
    rowi = lax.broadcasted_iota(jnp.int32, (hc, 1), 0)

    def chunk(ci, carry):
        for sq in range(n_seq):
            seq_chunk(sq, pl.multiple_of(ci * c, c))
        return carry

    def seq_chunk(sq, r0):
        def stacked(ref):
            return jnp.concatenate([ref[sq, pl.ds(r0, c), h * LANES:(h + 1) * LANES] for h in range(HGRN_HEADS)],
                                   axis=0)

        q = jax.nn.silu(stacked(hq_ref)) * HGRN_DK ** -0.5
        f = lb + (1.0 - lb) * jax.nn.sigmoid(stacked(hf_ref))
        k = 1.0 - f
        v = stacked(hi_ref)
        a, b = _split2(jnp.log(f))
        sums = _dot(mall_ref[...], jnp.concatenate([a, b], axis=1))
        sums = sums[:, :LANES] + sums[:, LANES:]
        cum = sums[:hc]
        att = jnp.zeros((hc, hc), F32)
        for li in range(n_lvl):
            n = c >> (li + 1)
            e = jnp.exp(sums[(1 + li) * hc:(2 + li) * hc])
            zz = (jnp.where((rowi & n) != 0, q, k) * e).astype(BF16)
            att = att + _dot_nt(zz, zz) * mask_ref[li]
        o = _dot(att.astype(BF16), v.astype(BF16)) + jnp.sum(q * k, axis=-1, keepdims=True) * v
        qe = (q * jnp.exp(cum)).astype(BF16)
        o_heads = []
        for h in range(HGRN_HEADS):
            rows = slice(h * c, (h + 1) * c)
            st = st_ref[sq, h]
            o_heads.append(o[rows] + _dot_nt(qe[rows], st.astype(BF16)))
            last = cum[(h + 1) * c - 1:(h + 1) * c, :]
            kd = (k[rows] * jnp.exp(last - cum[rows])).astype(BF16)
            st_ref[sq, h] = st * jnp.exp(last) + _dot(v[rows].T.astype(BF16), kd)
        o = jnp.concatenate(o_heads, axis=0)
        y = o * lax.rsqrt(jnp.mean(o * o, axis=-1, keepdims=True) + NORM_EPS) * gn * jax.nn.silu(stacked(hg_ref))
        for h in range(HGRN_HEADS):
            o_ref[sq, pl.ds(r0, c), h * LANES:(h + 1) * LANES] = y[h * c:(h + 1) * c].astype(BF16)

    lax.fori_loop(0, ts // c, chunk, 0, unroll=HG_UNROLL)

    @pl.when(si == pl.num_programs(1) - 1)
    def _():
        for sq in range(n_seq):
            for h in range(HGRN_HEADS):
                s_ref[sq, h] = st_ref[sq, h].T


def _hgrn_prompt(zh, lb_logits, gnorm, b_sz, s_len):
    c = HG_CHUNK
    ts = min(HG_TS, s_len)
    ns = s_len // ts
    n_lvl = int(math.log2(c))
    mall, masks = _hgrn_consts(c)
    n_seq = HG_SEQS if b_sz % HG_SEQS == 0 else 1
    z3 = zh.reshape(b_sz, s_len, zh.shape[1])
    col = lambda k: (lambda b, s: (b, s, k))
    o_b, s_fin = _pallas(
        functools.partial(_hgrn_prompt_kernel, c=c, n_lvl=n_lvl, ts=ts, n_seq=n_seq),
        grid=(b_sz // n_seq, ns),
        in_specs=[pl.BlockSpec((n_seq, ts, HGRN_KW), col(0)), pl.BlockSpec((n_seq, ts, HGRN_KW), col(1)),
                  pl.BlockSpec((n_seq, ts, HGRN_VW), col(2)), pl.BlockSpec((n_seq, ts, HGRN_VW), col(3)),
                  pl.BlockSpec((lb_logits.shape[0], HGRN_KW), lambda b, s: (0, 0)),
                  pl.BlockSpec((1, LANES), lambda b, s: (0, 0)),
                  pl.BlockSpec(mall.shape, lambda b, s: (0, 0)),
                  pl.BlockSpec(masks.shape, lambda b, s: (0, 0, 0))],
        out_specs=[pl.BlockSpec((n_seq, ts, HGRN_VW), lambda b, s: (b, s, 0)),
                   pl.BlockSpec((n_seq, HGRN_HEADS, HGRN_DK, HGRN_DV), lambda b, s: (b, 0, 0, 0))],
        out_shape=[jax.ShapeDtypeStruct((b_sz, s_len, HGRN_VW), BF16),
                   jax.ShapeDtypeStruct((b_sz, HGRN_HEADS, HGRN_DK, HGRN_DV), F32)],
        scratch_shapes=[pltpu.VMEM((n_seq, HGRN_HEADS, HGRN_DV, HGRN_DK), F32)],
        compiler_params=_cparams(("parallel", "arbitrary")),
    )(z3, z3, z3, z3, lb_logits, gnorm, mall, masks)
    return o_b.reshape(b_sz * s_len, HGRN_VW), s_fin


_R_EXP0 = 8


def _post_kernel(x_ref, oa_ref, ob_ref, ma_ref, mb_ref, wpa_ref, wpb_ref, wo_ref, g2_ref, wr_ref, br_ref,
                 x1_ref, h2_ref, lg_ref):
    pa = _dot(oa_ref[...], wpa_ref[...])
    pb = _dot(ob_ref[...], wpb_ref[...])
    merged = jax.nn.sigmoid(ma_ref[...]) * pa + jax.nn.sigmoid(mb_ref[...]) * pb
    x1 = x_ref[...] + _dot(merged.astype(BF16), wo_ref[...])
    h2 = _rms(x1, g2_ref[...])
    x1_ref[...] = x1
    h2_ref[...] = h2
    h_hi = h2.astype(BF16)
    h_lo = (h2 - h_hi.astype(F32)).astype(BF16)
    lg = _dot(jnp.concatenate([h_hi, h_hi, h_lo], axis=1), wr_ref[...]) + br_ref[...]
    lg_ref[...] = lg.T


def _router_weights(w_rg, b_rg, w_re, b_re):
    wr = jnp.zeros((D_MODEL, LANES), F32)
    wr = wr.at[:, :N_GROUPS].set(w_rg)
    wr = wr.at[:, _R_EXP0:_R_EXP0 + N_EXPERTS].set(jnp.transpose(w_re, (1, 0, 2)).reshape(D_MODEL, N_EXPERTS))
    w_hi = wr.astype(BF16)
    w_lo = (wr - w_hi.astype(F32)).astype(BF16)
    br = jnp.zeros((1, LANES), F32)
    br = br.at[0, :N_GROUPS].set(b_rg)
    br = br.at[0, _R_EXP0:_R_EXP0 + N_EXPERTS].set(b_re.reshape(N_EXPERTS))
    return jnp.concatenate([w_hi, w_lo, w_hi], axis=0), br


def _post_mixer(x2, o_a, o_b, zm, wpa, wpb, wo, g2, wrt, br):
    n = x2.shape[0]
    tm = _tile(n, TM_POST)
    row = lambda i: (i, 0)
    const = lambda i: (0, 0)
    return _pallas(
        _post_kernel,
        grid=(n // tm,),
        in_specs=[pl.BlockSpec((tm, D_MODEL), row), pl.BlockSpec((tm, NSA_WIDTH), row),
                  pl.BlockSpec((tm, HGRN_VW), row),
                  pl.BlockSpec((tm, D_MODEL), lambda i: (i, 0)), pl.BlockSpec((tm, D_MODEL), lambda i: (i, 1)),
                  pl.BlockSpec((NSA_WIDTH, D_MODEL), const), pl.BlockSpec((HGRN_VW, D_MODEL), const),
                  pl.BlockSpec((D_MODEL, D_MODEL), const), pl.BlockSpec((1, D_MODEL), const),
                  pl.BlockSpec((3 * D_MODEL, LANES), const), pl.BlockSpec((1, LANES), const)],
        out_specs=[pl.BlockSpec((tm, D_MODEL), row), pl.BlockSpec((tm, D_MODEL), row),
                   pl.BlockSpec((LANES, tm), lambda i: (0, i))],
        out_shape=[jax.ShapeDtypeStruct((n, D_MODEL), F32), jax.ShapeDtypeStruct((n, D_MODEL), F32),
                   jax.ShapeDtypeStruct((LANES, n), F32)],
        compiler_params=_cparams(("parallel",)),
    )(x2, o_a, o_b, zm, zm, wpa, wpb, wo, g2, wrt, br)


def _lowest_argmax(vals, top):
    idx = jnp.full(top.shape, len(vals) - 1, jnp.int32)
    for i in range(len(vals) - 2, -1, -1):
        idx = jnp.where(vals[i] == top, i, idx)
    return idx


def _route_kernel(lg_ref, u_ref, eid_ref, gate_ref, rank_ref, cnt_ref, carry_ref):
    @pl.when(pl.program_id(0) == 0)
    def _():
        carry_ref[...] = jnp.zeros_like(carry_ref)

    lg = lg_ref[...]
    tr = lg.shape[1]
    grp = [lg[g:g + 1] for g in range(N_GROUPS)]
    mx = functools.reduce(jnp.maximum, grp)
    ex = [jnp.exp(v - mx) for v in grp]
    den = functools.reduce(lambda a, b: a + b, ex)
    pr = [e / den for e in ex]
    pg = functools.reduce(jnp.maximum, pr)
    gtop = _lowest_argmax(pr, pg)
    le = []
    for j in range(EXPERTS_PER_GROUP):
        v = lg[_R_EXP0 + j:_R_EXP0 + j + 1]
        for g in range(1, N_GROUPS):
            r = _R_EXP0 + g * EXPERTS_PER_GROUP + j
            v = jnp.where(gtop == g, lg[r:r + 1], v)
        le.append(v)
    m1 = functools.reduce(jnp.maximum, le)
    i1 = _lowest_argmax(le, m1)
    le2 = [jnp.where(i1 == j, -jnp.inf, le[j]) for j in range(EXPERTS_PER_GROUP)]
    m2 = functools.reduce(jnp.maximum, le2)
    i2 = _lowest_argmax(le2, m2)
    e2 = jnp.exp(m2 - m1)
    den2 = 1.0 + e2
    gate1 = (1.0 / den2) * pg
    gate2 = (e2 / den2) * pg
    eid1 = gtop * EXPERTS_PER_GROUP + i1
    eid2 = gtop * EXPERTS_PER_GROUP + i2

    eio = lax.broadcasted_iota(jnp.int32, (N_EXPERTS, tr), 0)
    oh1 = jnp.where(eio == eid1, 1.0, 0.0)
    oh2 = jnp.where(eio == eid2, 1.0, 0.0)
    both = (oh1 + oh2).astype(BF16)
    carry = carry_ref[...]
    before = _dot(both, u_ref[...]) + jnp.concatenate([carry] * (tr // LANES), axis=1)
    rank1 = jnp.sum(oh1 * before, axis=0, keepdims=True).astype(jnp.int32)
    rank2 = jnp.sum(oh2 * before, axis=0, keepdims=True).astype(jnp.int32)
    carry = carry + _dot(both, jnp.ones((tr, LANES), BF16))
    carry_ref[...] = carry
    cnt_ref[...] = carry
    zi = jnp.zeros((6, tr), jnp.int32)
    eid_ref[...] = jnp.concatenate([eid1, eid2, zi], axis=0)
    rank_ref[...] = jnp.concatenate([rank1, rank2, zi], axis=0)
    gate_ref[...] = jnp.concatenate([gate1, gate2, jnp.zeros((6, tr), F32)], axis=0)


def _route(lg):
    n = lg.shape[1]
    tr = _tile(n, TR_ROUTE, LANES)
    u = jnp.asarray(np.triu(np.ones((tr, tr), np.float32), 1)).astype(BF16)
    col = lambda i: (0, i)
    return _pallas(
        _route_kernel,
        grid=(n // tr,),
        in_specs=[pl.BlockSpec((LANES, tr), col), pl.BlockSpec((tr, tr), lambda i: (0, 0))],
        out_specs=[pl.BlockSpec((8, tr), col), pl.BlockSpec((8, tr), col), pl.BlockSpec((8, tr), col),
                   pl.BlockSpec((N_EXPERTS, LANES), lambda i: (0, 0))],
        out_shape=[jax.ShapeDtypeStruct((8, n), jnp.int32), jax.ShapeDtypeStruct((8, n), F32),
                   jax.ShapeDtypeStruct((8, n), jnp.int32), jax.ShapeDtypeStruct((N_EXPERTS, LANES), F32)],
        scratch_shapes=[pltpu.VMEM((N_EXPERTS, LANES), F32)],
        compiler_params=_cparams(("arbitrary",)),
    )(lg, u)


def _dest_kernel(ps_ref, eid_ref, rank_ref, dest_ref):
    eid = eid_ref[...]
    dest = rank_ref[...]
    for e in range(N_EXPERTS):
        dest = dest + jnp.where(eid == e, ps_ref[e], 0)
    dest_ref[...] = dest


def _dest_rows(pad_start, eid, rank):
    n = eid.shape[1]
    tr = _tile(n, 4096, LANES)
    col = lambda i: (0, i)
    return _pallas(
        _dest_kernel,
        grid=(n // tr,),
        in_specs=[pl.BlockSpec(memory_space=pltpu.SMEM), pl.BlockSpec((8, tr), col), pl.BlockSpec((8, tr), col)],
        out_specs=pl.BlockSpec((8, tr), col),
        out_shape=jax.ShapeDtypeStruct((8, n), jnp.int32),
        compiler_params=_cparams(("parallel",)),
    )(pad_start, eid, rank)


DMA_UNROLL = 8


def _issue_rows(t_n, row_copy):
    for t in range(t_n):
        for k in range(TOP_K_EXPERTS):
            row_copy(t, k).start()


def _wait_rows(t_n, row_copy):
    def wait(t, c):
        for k in range(TOP_K_EXPERTS):
            row_copy(t, k).wait()
        return c

    lax.fori_loop(0, t_n, wait, 0, unroll=DMA_UNROLL)


def _row_loops(t_n, row_copy):
    _issue_rows(t_n, row_copy)
    _wait_rows(t_n, row_copy)


def _dispatch_kernel(dest_ref, h_ref, xb_in_ref, xb_ref, sem):
    del xb_in_ref

    def row_copy(t, k):
        return pltpu.make_async_copy(h_ref.at[pl.ds(t, 1), :], xb_ref.at[pl.ds(dest_ref[k, t], 1), :], sem)

    _row_loops(h_ref.shape[0], row_copy)


def _smem_cols(t):
    return pl.BlockSpec((8, t), lambda i: (0, i), memory_space=pltpu.SMEM)


def _dispatch(dest, h2, rows_total):
    n = h2.shape[0]
    t = _tile(n, T_DISP, LANES)
    xb0 = jnp.zeros((rows_total, D_MODEL), F32)
    return _pallas(
        _dispatch_kernel,
        grid=(n // t,),
        in_specs=[_smem_cols(t), pl.BlockSpec((t, D_MODEL), lambda i: (i, 0)), pl.BlockSpec(memory_space=pl.ANY)],
        out_specs=pl.BlockSpec(memory_space=pl.ANY),
        out_shape=jax.ShapeDtypeStruct((rows_total, D_MODEL), F32),
        scratch_shapes=[pltpu.SemaphoreType.DMA(())],
        input_output_aliases={2: 0},
        compiler_params=_cparams(("arbitrary",)),
    )(dest, h2, xb0)


def _expert_kernel(blk_e_ref, x_ref, wg_ref, wu_ref, wd_ref, y_ref):
    del blk_e_ref
    x = x_ref[...].astype(BF16)
    hid = jax.nn.silu(_dot(x, wg_ref[0])) * _dot(x, wu_ref[0])
    y_ref[...] = _dot(hid.astype(BF16), wd_ref[0])


def _experts(blk_e, xb, wg, wu, wd):
    rows_total = xb.shape[0]
    wsel = lambda i, e: (e[i], 0, 0)
    grid_spec = pltpu.PrefetchScalarGridSpec(
        num_scalar_prefetch=1,
        grid=(rows_total // MOE_BM,),
        in_specs=[pl.BlockSpec((MOE_BM, D_MODEL), lambda i, e: (i, 0)),
                  pl.BlockSpec((1, D_MODEL, D_EXPERT), wsel), pl.BlockSpec((1, D_MODEL, D_EXPERT), wsel),
                  pl.BlockSpec((1, D_EXPERT, D_MODEL), wsel)],
        out_specs=pl.BlockSpec((MOE_BM, D_MODEL), lambda i, e: (i, 0)),
    )
    return _pallas(
        _expert_kernel,
        grid_spec=grid_spec,
        out_shape=jax.ShapeDtypeStruct((rows_total, D_MODEL), F32),
        compiler_params=_cparams(("arbitrary",)),
    )(blk_e, xb, wg, wu, wd)


def _combine_kernel(dest_ref, next_ref, x1_ref, gate_ref, fn_ref, yb_ref, y_ref, buf_ref, sem):
    i = pl.program_id(0)
    slot = i % 2
    t_n = x1_ref.shape[0]

    def row_copy(idx_ref, sl):
        def copy(t, k):
            return pltpu.make_async_copy(yb_ref.at[pl.ds(idx_ref[k, t], 1), :], buf_ref.at[sl, k, pl.ds(t, 1), :],
                                         sem.at[sl])
        return copy

    @pl.when(i == 0)
    def _():
        _issue_rows(t_n, row_copy(dest_ref, 0))

    @pl.when(i + 1 < pl.num_programs(0))
    def _():
        _issue_rows(t_n, row_copy(next_ref, 1 - slot))

    _wait_rows(t_n, row_copy(dest_ref, slot))
    g = gate_ref[...]
    out = x1_ref[...] + g[:, 0:1] * buf_ref[slot, 0] + g[:, 1:2] * buf_ref[slot, 1]
    y_ref[...] = _rms(out, fn_ref[...])


def _combine(dest, x1, gate_t, fnorm, yb):
    n = x1.shape[0]
    t = _tile(n, T_DISP, LANES)
    return _pallas(
        _combine_kernel,
        grid=(n // t,),
        in_specs=[_smem_cols(t),
                  pl.BlockSpec((8, t), lambda i: (0, jnp.minimum(i + 1, n // t - 1)), memory_space=pltpu.SMEM),
                  pl.BlockSpec((t, D_MODEL), lambda i: (i, 0)), pl.BlockSpec((t, 8), lambda i: (i, 0)),
                  pl.BlockSpec((1, D_MODEL), lambda i: (0, 0)), pl.BlockSpec(memory_space=pl.ANY)],
        out_specs=pl.BlockSpec((t, D_MODEL), lambda i: (i, 0)),
        out_shape=jax.ShapeDtypeStruct((n, D_MODEL), F32),
        scratch_shapes=[pltpu.VMEM((2, TOP_K_EXPERTS, t, D_MODEL), F32), pltpu.SemaphoreType.DMA((2,))],
        compiler_params=_cparams(("arbitrary",)),
    )(dest, dest, x1, gate_t, fnorm, yb)


def _moe_and_final(x1, h2, lg, wg, wu, wd, fnorm):
    n = x1.shape[0]
    eid, gate, rank, cnt = _route(lg)
    counts = cnt[:, 0].astype(jnp.int32)
    padded = (counts + MOE_BM - 1) // MOE_BM * MOE_BM
    pad_end = jnp.cumsum(padded)
    pad_start = (pad_end - padded).astype(jnp.int32)
    n_blocks = -(-(n * TOP_K_EXPERTS + N_EXPERTS * (MOE_BM - 1)) // MOE_BM)
    blk_start = jnp.arange(n_blocks, dtype=jnp.int32) * MOE_BM
    blk_e = jnp.minimum(jnp.sum(pad_end[None, :] <= blk_start[:, None], axis=1), N_EXPERTS - 1).astype(jnp.int32)
    dest = _dest_rows(pad_start, eid, rank)
    xb = _dispatch(dest, h2, n_blocks * MOE_BM)
    yb = _experts(blk_e, xb, wg, wu, wd)
    return _combine(dest, x1, gate.T, fnorm, yb)


def _page_stream(pt_ref, pool_ref, buf_ref, sem, n_pages):
    b = pl.program_id(0)
    slot = b % 2

    def page_copy(bb, sl, p):
        col = pl.ds(pl.multiple_of(p * PAGE_SIZE, PAGE_SIZE), PAGE_SIZE)
        return pltpu.make_async_copy(pool_ref.at[pt_ref[bb, p]], buf_ref.at[sl, :, col], sem.at[sl])

    def issue(bb, sl):
        def body(p, c):
            page_copy(bb, sl, p).start()
            return c
        lax.fori_loop(0, n_pages, body, 0, unroll=DMA_UNROLL)

    def wait():
        def body(p, c):
            page_copy(b, slot, p).wait()
            return c
        lax.fori_loop(0, n_pages, body, 0, unroll=DMA_UNROLL)

    @pl.when(b == 0)
    def _():
        issue(0, 0)

    @pl.when(b + 1 < pl.num_programs(0))
    def _():
        issue(b + 1, 1 - slot)

    return slot, wait


TR_BLOCK = 256


def _chunk_perm():
    n_loc = TR_BLOCK // CMP_STRIDE
    r = np.arange(TR_BLOCK)
    perm = np.zeros((TR_BLOCK, TR_BLOCK), np.float32)
    perm[r, (r % n_loc) * CMP_STRIDE + r // n_loc] = 1.0
    return jnp.asarray(perm).astype(BF16)


def _compress_sample_kernel(pt_ref, pool_ref, perm_ref, w1bd_ref, w2bd_ref, c0_ref, b2_ref, out_ref,
                            buf_ref, ak_ref, av_ref, sem, *, n_pages, n_ch):
    slot, wait = _page_stream(pt_ref, pool_ref, buf_ref, sem, n_pages)
    wait()
    n_loc = TR_BLOCK // CMP_STRIDE

    def block(j, carry):
        xt = buf_ref[slot, :, pl.ds(pl.multiple_of(j * TR_BLOCK, TR_BLOCK), TR_BLOCK)].astype(BF16)
        rows = _dot_nt(perm_ref[...], xt)
        dst = pl.ds(pl.multiple_of(j * n_loc, n_loc), n_loc)
        for s in range(CMP_STRIDE):
            piece = rows[s * n_loc:(s + 1) * n_loc]
            ak_ref[dst, s * LANES:(s + 1) * LANES] = piece[:, :LANES].astype(BF16)
            av_ref[dst, s * LANES:(s + 1) * LANES] = piece[:, LANES:].astype(BF16)
        return carry

    lax.fori_loop(0, n_pages * PAGE_SIZE // TR_BLOCK, block, 0, unroll=4)
    for c, a_ref in ((0, ak_ref), (1, av_ref)):
        _compress_mlp(c, a_ref, n_ch, w1bd_ref, w2bd_ref, c0_ref, b2_ref, out_ref)


def _compress_sample(page_table, pool, cw):
    b_sz, n_pages = page_table.shape
    past = n_pages * PAGE_SIZE
    n_ch = past // CMP_STRIDE
    z3 = lambda b, pt: (0, 0, 0)
    specs = [pl.BlockSpec((2, CMP_STRIDE * LANES, 4 * LANES), z3), pl.BlockSpec((2, 2 * LANES, LANES), z3),
             pl.BlockSpec((2, 1, 2 * LANES), z3), pl.BlockSpec((2, 1, LANES), z3)]
    grid_spec = pltpu.PrefetchScalarGridSpec(
        num_scalar_prefetch=1,
        grid=(b_sz,),
        in_specs=[pl.BlockSpec(memory_space=pl.ANY), pl.BlockSpec((TR_BLOCK, TR_BLOCK), lambda b, pt: (0, 0))] + specs,
        out_specs=pl.BlockSpec((n_ch, KV_WIDTH), lambda b, pt: (b, 0)),
        scratch_shapes=[pltpu.VMEM((2, KV_WIDTH, past), F32), pltpu.VMEM((n_ch, CMP_STRIDE * LANES), BF16),
                        pltpu.VMEM((n_ch, CMP_STRIDE * LANES), BF16), pltpu.SemaphoreType.DMA((2,))],
    )
    assert past % TR_BLOCK == 0
    return _pallas(
        functools.partial(_compress_sample_kernel, n_pages=n_pages, n_ch=n_ch),
        grid_spec=grid_spec,
        out_shape=jax.ShapeDtypeStruct((b_sz * n_ch, KV_WIDTH), F32),
        compiler_params=_cparams(("arbitrary",)),
    )(page_table, pool, _chunk_perm(), *cw)


def _cmp_sample_kernel(qu_ref, ckv_ref, covt_ref, ocmp_ref, selt_ref, ps_s, *, past, t_len, n_sel, n_grp, n_cmp_pad):
    r = qu_ref.shape[1]
    rows_kt = NSA_KV_HEADS * t_len
    tpos = past + (lax.broadcasted_iota(jnp.int32, (r, 1), 0) % t_len)
    c_end = lax.broadcasted_iota(jnp.int32, (1, n_cmp_pad), 1) * CMP_STRIDE + (CMP_BLOCK - 1)

    def one_seq(sq, carry):
        ckv = ckv_ref[pl.ds(pl.multiple_of(sq * n_cmp_pad, n_cmp_pad), n_cmp_pad), :]
        p = _masked_softmax(_dot_nt(qu_ref[sq], ckv[:, :LANES].astype(BF16)), c_end <= tpos)
        ocmp_ref[sq] = _dot(p.astype(BF16), ckv[:, LANES:].astype(BF16))
        psum = []
        for kvh in range(NSA_KV_HEADS):
            acc = None
            for g in range(NSA_GROUP):
                blk = g * NSA_KV_HEADS + kvh
                pg = p[blk * t_len:(blk + 1) * t_len]
                acc = pg if acc is None else acc + pg
            psum.append(acc)
        ps_s[pl.ds(pl.multiple_of(sq * rows_kt, rows_kt), rows_kt), :] = jnp.concatenate(psum, axis=0)
        return carry

    lax.fori_loop(0, n_grp, one_seq, 0)
    imp_t = _dot_nt(covt_ref[...], ps_s[...], precision=HIGHEST)
    tp = past + (lax.broadcasted_iota(jnp.int32, imp_t.shape, 1) % t_len)
    selt_ref[0] = _select_t(imp_t, tp, n_sel)


def _cmp_sample(qu32, ckv, b_sz, past, t_len, n_sel, n_sel_pad):
    n_cmp_pad = past // CMP_STRIDE
    covt = _cover_t(n_cmp_pad, n_cmp_pad - 1, n_sel, n_sel_pad)
    r = qu32.shape[1]
    rows_kt = NSA_KV_HEADS * t_len
    n_grp = math.gcd(b_sz, LANES // rows_kt)
    assert rows_kt % 8 == 0
    ocmp, sel_t = _pallas(
        functools.partial(_cmp_sample_kernel, past=past, t_len=t_len, n_sel=n_sel, n_grp=n_grp, n_cmp_pad=n_cmp_pad),
        grid=(b_sz // n_grp,),
        in_specs=[pl.BlockSpec((n_grp, r, LANES), lambda b: (b, 0, 0)),
                  pl.BlockSpec((n_grp * n_cmp_pad, KV_WIDTH), lambda b: (b, 0)),
                  pl.BlockSpec((n_sel_pad, n_cmp_pad), lambda b: (0, 0))],
        out_specs=[pl.BlockSpec((n_grp, r, LANES), lambda b: (b, 0, 0)),
                   pl.BlockSpec((1, n_sel_pad, n_grp * rows_kt), lambda b: (b, 0, 0))],
        out_shape=[jax.ShapeDtypeStruct((b_sz, r, LANES), F32),
                   jax.ShapeDtypeStruct((b_sz // n_grp, n_sel_pad, n_grp * rows_kt), F32)],
        scratch_shapes=[pltpu.VMEM((n_grp * rows_kt, n_cmp_pad), F32)],
        compiler_params=_cparams(("parallel",)),
    )(qu32, ckv, covt)
    sel = sel_t.reshape(b_sz // n_grp, n_sel_pad, n_grp, rows_kt).transpose(0, 2, 3, 1)
    return ocmp, sel.reshape(b_sz, rows_kt, n_sel_pad)


TK_SAMPLE = 1024


def _slc_win_sample_kernel(pt_ref, qr_ref, qa_ref, sel_ref, e_ref, pool_ref, new_s_ref, win_ref, new_w_ref, ocmp_ref,
                           gate_ref, o_ref, buf_ref, sem, *, n_pages, past, t_len, win_buf):
    slot, wait_pages = _page_stream(pt_ref, pool_ref, buf_ref, sem, n_pages)
    qr = qr_ref[0]
    sel = sel_ref[0]
    r = qr.shape[0]
    tpos = past + (lax.broadcasted_iota(jnp.int32, (r, 1), 0) % t_len)
    init = (jnp.full((r, 1), NEG, F32), jnp.zeros((r, 1), F32), jnp.zeros((r, LANES), F32))
    new_i = lax.broadcasted_iota(jnp.int32, (1, LANES), 1)
    new_ok = ((past + new_i) <= tpos) & (new_i < t_len)

    wt = win_ref[0]
    wpos = past - win_buf + lax.broadcasted_iota(jnp.int32, (1, win_buf), 1)
    ok = (wpos >= 0) & (wpos <= tpos) & (wpos > tpos - WINDOW)
    carry = _online_step(_dot(qr, wt[:LANES].astype(BF16)), ok, wt[LANES:].astype(BF16), *init, v_t=True)
    rows = new_w_ref[0]
    ok = new_ok & ((past + new_i) > tpos - WINDOW)
    m, l, acc = _online_step(_dot_nt(qr, rows[:, :LANES].astype(BF16)), ok, rows[:, LANES:].astype(BF16), *carry)
    o_win = acc / jnp.maximum(l, 1e-30)

    wait_pages()

    qa = qa_ref[0]

    def slc_body(kt, carry):
        col = pl.ds(pl.multiple_of(kt * TK_SAMPLE, TK_SAMPLE), TK_SAMPLE)
        k_aug = jnp.concatenate([buf_ref[slot, :LANES, col].astype(BF16), e_ref[:, col]], axis=0)
        return _online_step(_dot(qa, k_aug), None, buf_ref[slot, LANES:, col].astype(BF16), *carry, v_t=True)

    carry = lax.fori_loop(0, past // TK_SAMPLE, slc_body, init, unroll=2)
    rows = new_s_ref[0]
    n_past_blk = past // SEL_BLOCK
    ok = new_ok & (sel[:, n_past_blk:n_past_blk + 1] > 0.5)
    m, l, acc = _online_step(_dot_nt(qr, rows[:, :LANES].astype(BF16)), ok, rows[:, LANES:].astype(BF16), *carry)
    o_slc = acc / jnp.maximum(l, 1e-30)

    g = gate_ref[0]
    o = g[:, 0:1] * ocmp_ref[0] + g[:, 1:2] * o_slc + g[:, 2:3] * o_win
    row = lax.broadcasted_iota(jnp.int32, (r, LANES), 0)
    lane = lax.broadcasted_iota(jnp.int32, (r, LANES), 1)
    o_ref[0] = jnp.where((((row // t_len) % 2) == 0) == (lane < HEAD_DIM), o, 0.0)


def _slc_win_sample(page_table, qr32, sel32, pool, new_s, win_t, new_w, ocmp32, gate32, past, t_len, n_sel_pad):
    b_sz, n_pages = page_table.shape
    win_buf = win_t.shape[2]
    r = qr32.shape[1]
    n_past_blk = past // SEL_BLOCK
    assert n_past_blk <= LANES
    unsel = jnp.pad(1.0 - sel32[:, :, :n_past_blk].astype(F32), ((0, 0), (0, 0), (0, LANES - n_past_blk)))
    qa32 = jnp.concatenate([qr32, unsel.astype(BF16)], axis=2)
    e_mat = _block_neg(past).T
    b3 = lambda b, pt: (b, 0, 0)
    grid_spec = pltpu.PrefetchScalarGridSpec(
        num_scalar_prefetch=1,
        grid=(b_sz,),
        in_specs=[pl.BlockSpec((1, r, LANES), b3), pl.BlockSpec((1, r, 2 * LANES), b3),
                  pl.BlockSpec((1, r, n_sel_pad), b3),
                  pl.BlockSpec((LANES, past), lambda b, pt: (0, 0)),
                  pl.BlockSpec(memory_space=pl.ANY),
                  pl.BlockSpec((1, LANES, KV_WIDTH), b3),
                  pl.BlockSpec((1, KV_WIDTH, win_buf), b3),
                  pl.BlockSpec((1, LANES, KV_WIDTH), b3),
                  pl.BlockSpec((1, r, LANES), b3), pl.BlockSpec((1, r, 3), b3)],
        out_specs=pl.BlockSpec((1, r, LANES), b3),
        scratch_shapes=[pltpu.VMEM((2, KV_WIDTH, past), F32), pltpu.SemaphoreType.DMA((2,))],
    )
    return _pallas(
        functools.partial(_slc_win_sample_kernel, n_pages=n_pages, past=past, t_len=t_len, win_buf=win_buf),
        grid_spec=grid_spec,
        out_shape=jax.ShapeDtypeStruct((b_sz, r, LANES), F32),
        compiler_params=_cparams(("arbitrary",)),
    )(page_table, qr32, qa32, sel32, e_mat, pool, new_s, win_t, new_w, ocmp32, gate32)


def _rows32(q, b_sz, t_len):
    q5 = q.reshape(b_sz, t_len, NSA_GROUP, NSA_KV_HEADS, HEAD_DIM).transpose(0, 2, 3, 1, 4)
    eye = jnp.eye(NSA_KV_HEADS, dtype=q.dtype)
    return jnp.einsum('bgktd,kj->bgktjd', q5, eye).reshape(b_sz, NSA_HEADS * t_len, LANES)


def _from_rows32(o32, b_sz, t_len):
    o6 = o32.reshape(b_sz, NSA_GROUP, NSA_KV_HEADS, t_len, NSA_KV_HEADS, HEAD_DIM)
    o5 = jnp.stack([o6[:, :, k, :, k] for k in range(NSA_KV_HEADS)], axis=2)
    return o5.transpose(0, 3, 1, 2, 4).reshape(b_sz * t_len, NSA_WIDTH)


def _nsa_sample(qu, qr, kvs_new, kvw_new, gates, cache_cmp, cache_slc, cache_win, page_table, cw, t_len):
    b_sz, n_pages = page_table.shape
    past = n_pages * PAGE_SIZE
    total = past + t_len
    assert t_len < CMP_STRIDE and t_len <= SEL_BLOCK and past % TK_SAMPLE == 0
    n_sel = -(-total // SEL_BLOCK)
    n_sel_pad = -(-n_sel // 8) * 8
    pool_c = cache_cmp.reshape(-1, PAGE_SIZE, KV_WIDTH).transpose(0, 2, 1)
    pool_s = cache_slc.reshape(-1, PAGE_SIZE, KV_WIDTH).transpose(0, 2, 1)
    win_t = cache_win.reshape(b_sz, cache_win.shape[1], KV_WIDTH).transpose(0, 2, 1)
    ckv = _compress_sample(page_table, pool_c, cw)
    ocmp32, sel = _cmp_sample(_rows32(qu, b_sz, t_len), ckv, b_sz, past, t_len, n_sel, n_sel_pad)
    sel = sel.reshape(b_sz, 1, NSA_KV_HEADS * t_len, n_sel_pad)
    sel32 = jnp.broadcast_to(sel, (b_sz, NSA_GROUP, NSA_KV_HEADS * t_len, n_sel_pad))
    sel32 = sel32.reshape(b_sz, NSA_HEADS * t_len, n_sel_pad).astype(BF16)
    pad_rows = lambda a: jnp.pad(a.reshape(b_sz, t_len, KV_WIDTH), ((0, 0), (0, LANES - t_len), (0, 0)))
    g4 = gates[:, :3 * NSA_HEADS].reshape(b_sz, t_len, NSA_KV_HEADS, NSA_GROUP, 3)
    gate32 = g4.transpose(0, 3, 2, 1, 4).reshape(b_sz, NSA_HEADS * t_len, 3)
    o32 = _slc_win_sample(page_table, _rows32(qr, b_sz, t_len), sel32, pool_s, pad_rows(kvs_new), win_t,
                          pad_rows(kvw_new), ocmp32, gate32, past, t_len, n_sel_pad)
    return _from_rows32(o32, b_sz, t_len).astype(BF16)


def _hgrn_sample_kernel(hq_ref, hf_ref, vt_ref, hgt_ref, s0_ref, lbl_ref, gnt_ref, ot_ref, s_ref, *, t_len):
    lb = _lower_bound(lbl_ref[...])
    q_all = jax.nn.silu(hq_ref[0]) * HGRN_DK ** -0.5
    f_all = lb + (1.0 - lb) * jax.nn.sigmoid(hf_ref[0])
    for h in range(HGRN_HEADS):
        hs = slice(h * LANES, (h + 1) * LANES)
        q = q_all[:, hs]
        f = f_all[:, hs]
        k = 1.0 - f
        vt = vt_ref[0, h]
        st = s0_ref[0, h].T
        cols = []
        for t in range(t_len):
            st = st * f[t:t + 1] + vt[:, t:t + 1] * k[t:t + 1]
            cols.append(jnp.sum(st * q[t:t + 1], axis=1, keepdims=True))
        ot = jnp.concatenate(cols, axis=1)
        y = ot * lax.rsqrt(jnp.mean(ot * ot, axis=0, keepdims=True) + NORM_EPS) * gnt_ref[...]
        ot_ref[0, h] = y * jax.nn.silu(hgt_ref[0, h])
        s_ref[0, h] = st.T


def _hgrn_sample(zh, s0, lb_logits, gnorm, b_sz, t_len):
    z3 = zh.reshape(b_sz, t_len, 4 * HGRN_KW)
    to_t = lambda a: a.reshape(b_sz, t_len, HGRN_HEADS, HGRN_DV).transpose(0, 2, 3, 1)
    vt = to_t(z3[:, :, 2 * HGRN_KW:3 * HGRN_KW])
    hgt = to_t(z3[:, :, 3 * HGRN_KW:])
    col = lambda k: (lambda b: (b, 0, k))
    b4 = lambda b: (b, 0, 0, 0)
    ot, s_new = _pallas(
        functools.partial(_hgrn_sample_kernel, t_len=t_len),
        grid=(b_sz,),
        in_specs=[pl.BlockSpec((1, t_len, HGRN_KW), col(0)), pl.BlockSpec((1, t_len, HGRN_KW), col(1)),
                  pl.BlockSpec((1, HGRN_HEADS, HGRN_DV, t_len), b4), pl.BlockSpec((1, HGRN_HEADS, HGRN_DV, t_len), b4),
                  pl.BlockSpec((1, HGRN_HEADS, HGRN_DK, HGRN_DV), b4),
                  pl.BlockSpec((lb_logits.shape[0], HGRN_KW), lambda b: (0, 0)),
                  pl.BlockSpec((HGRN_DV, 1), lambda b: (0, 0))],
        out_specs=[pl.BlockSpec((1, HGRN_HEADS, HGRN_DV, t_len), b4),
                   pl.BlockSpec((1, HGRN_HEADS, HGRN_DK, HGRN_DV), b4)],
        out_shape=[jax.ShapeDtypeStruct((b_sz, HGRN_HEADS, HGRN_DV, t_len), F32),
                   jax.ShapeDtypeStruct((b_sz, HGRN_HEADS, HGRN_DK, HGRN_DV), F32)],
        compiler_params=_cparams(("parallel",)),
    )(z3, z3, vt, hgt, s0, lb_logits, gnorm.reshape(HGRN_DV, 1))
    o_b = ot.transpose(0, 3, 1, 2).reshape(b_sz * t_len, HGRN_VW)
    return o_b.astype(BF16), s_new


def kernel(x_prompt, x_sample, cache_cmp_kv, cache_slc_kv, cache_win_kv, state_hgrn, page_table, norm1, w_in, cmp_pe,
           cmp_w1, cmp_b1, cmp_w2, cmp_b2, hgrn_lb_logits, hgrn_gnorm, w_proj_a, w_proj_b, w_out, norm2,
           w_router_group, b_router_group, w_router_expert, b_router_expert, w_exp_gate, w_exp_up, w_exp_down,
           final_norm):
    assert w_in.shape[0] == 1, "one layer"
    b_sz, s_len, _ = x_prompt.shape
    d_sz, t_len, _ = x_sample.shape
    past = page_table.shape[1] * PAGE_SIZE
    n_p = b_sz * s_len
    n_s = d_sz * t_len
    kv_row = (2, NSA_KV_HEADS, HEAD_DIM)

    offs = np.cumsum((0,) + IN_SPLITS)
    w_att = _attn_weight(w_in[0])
    w_h = w_in[0][:, offs[5]:offs[9]].astype(BF16)
    w_m = w_in[0][:, offs[9]:offs[11]].astype(BF16)
    g1 = norm1[0][None]
    cw = _compress_weights(cmp_pe[0], cmp_w1[0], cmp_b1[0], cmp_w2[0], cmp_b2[0])
    perm = _pair_perm()
    wpa = w_proj_a[0][perm].astype(BF16)
    wpb = w_proj_b[0].astype(BF16)
    wo = w_out[0].astype(BF16)
    wrt, br = _router_weights(w_router_group[0], b_router_group[0], w_router_expert[0], b_router_expert[0])
    gn = hgrn_gnorm[0][None]

    xp = x_prompt.reshape(n_p, D_MODEL)
    cos, sin = _rope_tables(jnp.arange(s_len))
    qu, qr, kvc_p, kvs_t, kvw_t, kvs_b, kvw_b, gates, kvc_t = _attn_proj(xp, g1, w_att, cos, sin, seq_len=s_len)
    zh = _norm_proj(xp, g1, w_h)
    zm = _norm_proj(xp, g1, w_m)
    ckv = _compress_prompt(kvc_p, b_sz, s_len, cw)
    o_a = _nsa_prompt(qu, qr, ckv, kvs_b, kvw_b, gates, b_sz, s_len)
    o_b, hg_p = _hgrn_prompt(zh, hgrn_lb_logits, gn, b_sz, s_len)
    x1_p, h2_p, lg_p = _post_mixer(xp, o_a, o_b, zm, wpa, wpb, wo, norm2[0][None], wrt, br)

    xs = x_sample.reshape(n_s, D_MODEL)
    cos, sin = _rope_tables(past + jnp.tile(jnp.arange(t_len), d_sz))
    qu, qr, kvc_s, kvs_s, kvw_s, _, _, gates = _attn_proj(xs, g1, w_att, cos, sin)
    zh = _norm_proj(xs, g1, w_h)
    zm = _norm_proj(xs, g1, w_m)
    o_a = _nsa_sample(qu, qr, kvs_s, kvw_s, gates, cache_cmp_kv[0], cache_slc_kv[0], cache_win_kv[0], page_table, cw,
                      t_len)
    o_b, hg_s = _hgrn_sample(zh, state_hgrn[0], hgrn_lb_logits, hgrn_gnorm[0], d_sz, t_len)
    x1_s, h2_s, lg_s = _post_mixer(xs, o_a, o_b, zm, wpa, wpb, wo, norm2[0][None], wrt, br)

    w_exp = (w_exp_gate[0].astype(BF16), w_exp_up[0].astype(BF16), w_exp_down[0].astype(BF16))
    y_p = _moe_and_final(x1_p, h2_p, lg_p, *w_exp, final_norm[None]).reshape(b_sz, s_len, D_MODEL)
    y_s = _moe_and_final(x1_s, h2_s, lg_s, *w_exp, final_norm[None]).reshape(d_sz, t_len, D_MODEL)

    def rows6(a_t):
        return a_t.reshape((a_t.shape[0],) + kv_row + (a_t.shape[2],)).transpose(0, 4, 1, 2, 3)[None]

    win_p = kvw_t[:, :, s_len - min(WINDOW, s_len):]
    win_rows = jnp.concatenate([cache_win_kv[0], kvw_s.reshape((d_sz, t_len) + kv_row)], axis=1)
    win_s = win_rows[:, win_rows.shape[1] - min(WINDOW, past + t_len):]
    return (y_p, y_s,
            rows6(kvc_t), kvc_s.reshape((1, d_sz, t_len) + kv_row),
            rows6(kvs_t), kvs_s.reshape((1, d_sz, t_len) + kv_row),
            rows6(win_p), win_s[None], hg_p[None], hg_s[None])
```

```python
import functools
import math

import numpy as np
import jax
import jax.numpy as jnp
from jax import lax
from jax.experimental import pallas as pl
from jax.experimental.pallas import tpu as pltpu

D_MODEL = 1024
PAGE_SIZE = 128
NSA_HEADS = 8
NSA_KV_HEADS = 2
NSA_GROUP = NSA_HEADS // NSA_KV_HEADS
HEAD_DIM = 64
CMP_BLOCK = 32
CMP_STRIDE = 16
CMP_HIDDEN = 2 * HEAD_DIM
SEL_BLOCK = 64
N_SEL = 16
WINDOW = 512
ROPE_THETA = 10000.0
HGRN_HEADS = 4
HGRN_DK = 128
HGRN_DV = 128
N_GROUPS = 4
EXPERTS_PER_GROUP = 8
N_EXPERTS = N_GROUPS * EXPERTS_PER_GROUP
TOP_K_EXPERTS = 2
D_EXPERT = 512
NSA_WIDTH = NSA_HEADS * HEAD_DIM
KV_WIDTH = 2 * NSA_KV_HEADS * HEAD_DIM
HGRN_KW = HGRN_HEADS * HGRN_DK
HGRN_VW = HGRN_HEADS * HGRN_DV
IN_SPLITS = (NSA_WIDTH, KV_WIDTH, KV_WIDTH, KV_WIDTH, 3 * NSA_HEADS, HGRN_KW, HGRN_KW, HGRN_VW, HGRN_VW,
             D_MODEL, D_MODEL)
NORM_EPS = 1e-6
NEG = -1e9
BIG = 1e9

LANES = 128
VMEM_LIMIT = 56 * 1024 * 1024
TM_PROJ = 512
TQ = 256
TK_SLC = 512
HG_CHUNK = 64
HG_TS = 256
HG_SEQS = 2
HG_UNROLL = 2
TM_POST = 256
TR_ROUTE = 512
MOE_BM = 256
T_DISP = 256

F32 = jnp.float32
BF16 = jnp.bfloat16
HIGHEST = lax.Precision.HIGHEST


def _dot(a, b, precision=None):
    return jnp.dot(a, b, preferred_element_type=F32, precision=precision)


def _dot_nt(a, b, precision=None):
    return lax.dot_general(a, b, (((1,), (1,)), ((), ())), preferred_element_type=F32, precision=precision)


def _tile(n, target, align=8):
    if n <= target:
        return n
    t = target - target % align
    while n % t:
        t -= align
    return t


def _pallas(body, **kw):
    fn = getattr(body, "func", body)
    return pl.pallas_call(body, name=fn.__name__.strip("_").removesuffix("_kernel"), **kw)


def _cparams(sem):
    return pltpu.CompilerParams(dimension_semantics=sem, vmem_limit_bytes=VMEM_LIMIT)


def _masked_softmax(s, ok):
    m = jnp.max(jnp.where(ok, s, NEG), axis=-1, keepdims=True)
    e = jnp.exp2(jnp.where(ok, s - m, NEG))
    return e / jnp.maximum(jnp.sum(e, axis=-1, keepdims=True), 1e-30)


def _rms(x, g):
    return x * lax.rsqrt(jnp.mean(x * x, axis=-1, keepdims=True) + NORM_EPS) * g


_A_Q, _A_QR, _A_KC, _A_KS, _A_KSR, _A_KW, _A_KWR, _A_G, _A_END = 0, 512, 1024, 1280, 1536, 1664, 1920, 2048, 2176


def _attn_proj_kernel(x_ref, g_ref, w_ref, cos_ref, sin_ref,
                      qu_ref, qr_ref, kvc_ref, kvs_ref, kvw_ref, kvsb_ref, kvwb_ref, gate_ref, *kvct_ref, kv_t):
    h = _rms(x_ref[...], g_ref[...]).astype(BF16)
    z = _dot(h, w_ref[...])
    cos = cos_ref[...]
    sin = sin_ref[...]
    q = z[:, _A_Q:_A_QR]
    qu_ref[...] = q.astype(BF16)
    qr_ref[...] = (q * cos + z[:, _A_QR:_A_KC] * sin).astype(BF16)
    kvc = z[:, _A_KC:_A_KS]
    kvc_ref[...] = kvc
    ck = cos[:, :LANES]
    sk = sin[:, :LANES]
    ks = z[:, _A_KS:_A_KS + 128] * ck + z[:, _A_KSR:_A_KW] * sk
    vs = z[:, _A_KS + 128:_A_KSR]
    kvsb_ref[:, :128] = ks.astype(BF16)
    kvsb_ref[:, 128:] = vs.astype(BF16)
    kw = z[:, _A_KW:_A_KW + 128] * ck + z[:, _A_KWR:_A_G] * sk
    vw = z[:, _A_KW + 128:_A_KWR]
    kvwb_ref[:, :128] = kw.astype(BF16)
    kvwb_ref[:, 128:] = vw.astype(BF16)
    if kv_t:
        kvct_ref[0][0] = kvc.T
        kvs_ref[0, :128, :] = ks.T
        kvs_ref[0, 128:, :] = vs.T
        kvw_ref[0, :128, :] = kw.T
        kvw_ref[0, 128:, :] = vw.T
    else:
        kvs_ref[:, :128] = ks
        kvs_ref[:, 128:] = vs
        kvw_ref[:, :128] = kw
        kvw_ref[:, 128:] = vw
    gate_ref[...] = jax.nn.sigmoid(z[:, _A_G:_A_END])


def _rot_cols(w):
    d, n = w.shape
    w4 = w.reshape(d, n // HEAD_DIM, 2, HEAD_DIM // 2)
    return jnp.stack([-w4[:, :, 1], w4[:, :, 0]], axis=2).reshape(d, n)


def _pair_perm():
    idx = []
    for g in range(NSA_GROUP):
        for kvh in range(NSA_KV_HEADS):
            h = kvh * NSA_GROUP + g
            idx.extend(range(h * HEAD_DIM, (h + 1) * HEAD_DIM))
    return np.asarray(idx, np.int32)


def _attn_weight(w_in):
    offs = np.cumsum((0,) + IN_SPLITS)
    scale = HEAD_DIM ** -0.5 * math.log2(math.e)
    wq = w_in[:, offs[0]:offs[1]][:, _pair_perm()] * scale
    wkc = w_in[:, offs[1]:offs[2]]
    wks = w_in[:, offs[2]:offs[3]]
    wkw = w_in[:, offs[3]:offs[4]]
    wg = w_in[:, offs[4]:offs[5]]
    wg = jnp.pad(wg, ((0, 0), (0, LANES - wg.shape[1])))
    half = KV_WIDTH // 2
    w = jnp.concatenate([wq, _rot_cols(wq), wkc, wks, _rot_cols(wks[:, :half]), wkw, _rot_cols(wkw[:, :half]), wg],
                        axis=1)
    return w.astype(BF16)


def _rope_tables(pos):
    half = HEAD_DIM // 2
    inv = ROPE_THETA ** (-jnp.arange(half, dtype=F32) / half)
    ang = pos.astype(F32)[:, None] * inv[None, :]
    cos = jnp.tile(jnp.cos(ang), (1, 2 * NSA_HEADS))
    sin = jnp.tile(jnp.sin(ang), (1, 2 * NSA_HEADS))
    return cos, sin


def _attn_proj(x2, g, w_att, cos, sin, seq_len=None):
    n = x2.shape[0]
    tm = _tile(n, TM_PROJ)
    n_tab = cos.shape[0] // tm
    row = lambda i: (i, 0)
    tab = lambda i: (i % n_tab, 0)
    const = lambda i: (0, 0)
    row_major = lambda width, dt: (jax.ShapeDtypeStruct((n, width), dt), pl.BlockSpec((tm, width), row))
    outs = [row_major(NSA_WIDTH, BF16), row_major(NSA_WIDTH, BF16), row_major(KV_WIDTH, F32), row_major(KV_WIDTH, F32),
            row_major(KV_WIDTH, F32), row_major(KV_WIDTH, BF16), row_major(KV_WIDTH, BF16), row_major(LANES, F32)]
    if seq_len is not None:
        assert seq_len % tm == 0 and tm % LANES == 0
        per_seq = seq_len // tm
        pos_minor = (jax.ShapeDtypeStruct((n // seq_len, KV_WIDTH, seq_len), F32),
                     pl.BlockSpec((1, KV_WIDTH, tm), lambda i: (i // per_seq, 0, i % per_seq)))
        outs[3] = outs[4] = pos_minor
        outs.append(pos_minor)
    return _pallas(
        functools.partial(_attn_proj_kernel, kv_t=seq_len is not None),
        grid=(n // tm,),
        in_specs=[pl.BlockSpec((tm, D_MODEL), row), pl.BlockSpec((1, D_MODEL), const),
                  pl.BlockSpec((D_MODEL, _A_END), const),
                  pl.BlockSpec((tm, NSA_WIDTH), tab), pl.BlockSpec((tm, NSA_WIDTH), tab)],
        out_specs=[spec for _, spec in outs],
        out_shape=[shape for shape, _ in outs],
        compiler_params=_cparams(("parallel",)),
    )(x2, g, w_att, cos, sin)


def _norm_proj_kernel(x_ref, g_ref, w_ref, o_ref):
    h = _rms(x_ref[...], g_ref[...]).astype(BF16)
    o_ref[...] = _dot(h, w_ref[...])


def _norm_proj(x2, g, w):
    n = x2.shape[0]
    tm = _tile(n, TM_PROJ)
    width = w.shape[1]
    return _pallas(
        _norm_proj_kernel,
        grid=(n // tm,),
        in_specs=[pl.BlockSpec((tm, D_MODEL), lambda i: (i, 0)), pl.BlockSpec((1, D_MODEL), lambda i: (0, 0)),
                  pl.BlockSpec((D_MODEL, width), lambda i: (0, 0))],
        out_specs=pl.BlockSpec((tm, width), lambda i: (i, 0)),
        out_shape=jax.ShapeDtypeStruct((n, width), F32),
        compiler_params=_cparams(("parallel",)),
    )(x2, g, w)


def _compress_weights(cmp_pe, cmp_w1, cmp_b1, cmp_w2, cmp_b2):
    w1r = cmp_w1.reshape(2, 2, CMP_STRIDE, HEAD_DIM, CMP_HIDDEN)
    eye = jnp.eye(NSA_KV_HEADS, dtype=F32)
    w1bd = jnp.einsum('casdh,kj->cskdajh', w1r, eye).reshape(2, CMP_STRIDE * NSA_KV_HEADS * HEAD_DIM,
                                                             2 * NSA_KV_HEADS * CMP_HIDDEN)
    w2bd = jnp.einsum('chd,kj->ckhjd', cmp_w2, eye).reshape(2, NSA_KV_HEADS * CMP_HIDDEN, NSA_KV_HEADS * HEAD_DIM)
    pe8 = jnp.broadcast_to(cmp_pe.reshape(2, 1, CMP_BLOCK * HEAD_DIM), (2, 8, CMP_BLOCK * HEAD_DIM))
    b1t = jnp.tile(cmp_b1[:, None, :], (1, 1, NSA_KV_HEADS))
    b2t = jnp.tile(cmp_b2[:, None, :], (1, 1, NSA_KV_HEADS))
    c0 = _pallas(
        _compress_bias_kernel,
        out_shape=jax.ShapeDtypeStruct((2, 1, NSA_KV_HEADS * CMP_HIDDEN), F32),
    )(pe8, cmp_w1, b1t)
    return w1bd.astype(BF16), w2bd.astype(BF16), c0, b2t


def _compress_bias_kernel(pe8_ref, w1_ref, b1_ref, c0_ref):
    for c in range(2):
        c0 = _dot(pe8_ref[c], w1_ref[c], precision=HIGHEST)[0:1, :]
        c0_ref[c] = jnp.concatenate([c0] * NSA_KV_HEADS, axis=1) + b1_ref[c]


def _compress_rows(rows_refs, n_ch, a_ref, w1bd_ref, w2bd_ref, c0_ref, b2_ref, out_ref):
    for c in range(2):
        for s in range(CMP_STRIDE):
            a_ref[:, s * LANES:(s + 1) * LANES] = rows_refs[c][pl.ds(s, n_ch, stride=CMP_STRIDE), :].astype(BF16)
        _compress_mlp(c, a_ref, n_ch, w1bd_ref, w2bd_ref, c0_ref, b2_ref, out_ref)


def _compress_mlp(c, a_ref, n_ch, w1bd_ref, w2bd_ref, c0_ref, b2_ref, out_ref):
    hcat = _dot(a_ref[...], w1bd_ref[c])
    h_b = pltpu.roll(hcat[:, 2 * LANES:], n_ch - 1, 0)
    hid = jax.nn.gelu(hcat[:, :2 * LANES] + h_b + c0_ref[c])
    out_ref[:, c * LANES:(c + 1) * LANES] = _dot(hid.astype(BF16), w2bd_ref[c]) + b2_ref[c]


def _compress_prompt_kernel(k_ref, v_ref, w1bd_ref, w2bd_ref, c0_ref, b2_ref, out_ref, a_ref, *, n_ch):
    _compress_rows((k_ref, v_ref), n_ch, a_ref, w1bd_ref, w2bd_ref, c0_ref, b2_ref, out_ref)


def _cw_specs():
    z3 = lambda *a: (0, 0, 0)
    return [pl.BlockSpec((2, CMP_STRIDE * LANES, 4 * LANES), z3), pl.BlockSpec((2, 2 * LANES, LANES), z3),
            pl.BlockSpec((2, 1, 2 * LANES), z3), pl.BlockSpec((2, 1, LANES), z3)]


def _compress_prompt(kvc, b_sz, s_len, cw):
    n_ch = s_len // CMP_STRIDE
    return _pallas(
        functools.partial(_compress_prompt_kernel, n_ch=n_ch),
        grid=(b_sz,),
        in_specs=[pl.BlockSpec((s_len, LANES), lambda b: (b, 0)), pl.BlockSpec((s_len, LANES), lambda b: (b, 1))]
        + _cw_specs(),
        out_specs=pl.BlockSpec((n_ch, KV_WIDTH), lambda b: (b, 0)),
        out_shape=jax.ShapeDtypeStruct((b_sz * n_ch, KV_WIDTH), F32),
        scratch_shapes=[pltpu.VMEM((n_ch, CMP_STRIDE * LANES), BF16)],
        compiler_params=_cparams(("parallel",)),
    )(kvc, kvc, *cw)


def _cover_t(n_cmp_pad, n_cmp, n_sel, n_sel_pad):
    i = np.arange(n_cmp_pad)[None, :]
    j = np.arange(n_sel_pad)[:, None]
    start = i * CMP_STRIDE
    m = (start < (j + 1) * SEL_BLOCK) & (start + CMP_BLOCK > j * SEL_BLOCK) & (i < n_cmp) & (j < n_sel)
    return jnp.asarray(m.astype(np.float32))


def _select_t(imp_t, tp, n_sel):
    j = lax.broadcasted_iota(jnp.int32, imp_t.shape, 0)
    cur = tp // SEL_BLOCK
    valid = j * SEL_BLOCK <= tp
    forced = (j == 0) | (j == cur) | (j == cur - 1)
    score = jnp.where(valid, jnp.where(forced, BIG, imp_t), NEG)
    cnt = jnp.zeros(imp_t.shape, F32)
    for jp in range(n_sel):
        row = score[jp:jp + 1, :]
        cnt = cnt + jnp.where(j > jp, jnp.where(row >= score, 1.0, 0.0), jnp.where(row > score, 1.0, 0.0))
    return jnp.where((cnt < min(N_SEL, n_sel)) & valid, 1.0, 0.0)


def _online_step(s, ok, v, m, l, acc, v_t=False):
    if ok is None:
        m_new = jnp.maximum(m, jnp.max(s, axis=-1, keepdims=True))
        e = jnp.exp2(s - m_new)
    else:
        m_new = jnp.maximum(m, jnp.max(jnp.where(ok, s, NEG), axis=-1, keepdims=True))
        e = jnp.where(ok, jnp.exp2(s - m_new), 0.0)
    alpha = jnp.exp2(m - m_new)
    l = alpha * l + jnp.sum(e, axis=-1, keepdims=True)
    acc = alpha * acc + (_dot_nt if v_t else _dot)(e.astype(BF16), v)
    return m_new, l, acc


WIN_TILES = ((0, WINDOW + TQ),)
CHAIN_ROWS = 512


def _attend(state, s, v_t, mask):
    m_old, l_old, acc_old = state
    rows, tk = s.shape
    if mask is not None:
        s = jnp.where(mask[None], s.reshape(rows // TQ, TQ, tk), NEG).reshape(rows, tk)
    m_new = jnp.maximum(m_old, jnp.broadcast_to(jnp.max(s, axis=-1, keepdims=True), (rows, LANES)))
    e = jnp.concatenate([jnp.exp2(s[:, c:c + LANES] - m_new) for c in range(0, tk, LANES)], axis=1)
    ones = jnp.ones((tk, LANES), BF16)
    pv = _dot(e.astype(BF16), jnp.concatenate([v_t, ones], axis=1))
    alpha = jnp.exp2(m_old - m_new)
    return m_new, alpha * l_old + pv[:, LANES:], alpha * acc_old + pv[:, :LANES]


def _nsa_prompt_kernel(qu_ref, qr_ref, ckv_ref, ks_ref, vs_ref, kw_ref, vw_ref, gate_ref, covt_ref, eb_ref, ge_ref,
                       o_ref, qu_s, qa_s, m_s, l_s, acc_s, owin_s, *, n_sel):
    qi = pl.program_id(1)
    t0 = qi * TQ
    r = NSA_HEADS * TQ
    lane = lax.broadcasted_iota(jnp.int32, (TQ, LANES), 1)
    own = [lane < HEAD_DIM, lane >= HEAD_DIM]
    rpos = lax.broadcasted_iota(jnp.int32, (TQ, 1), 0)
    for g in range(NSA_GROUP):
        qug = qu_ref[:, g * LANES:(g + 1) * LANES]
        qrg = qr_ref[:, g * LANES:(g + 1) * LANES]
        for kvh in range(NSA_KV_HEADS):
            rows = pl.ds((g * NSA_KV_HEADS + kvh) * TQ, TQ)
            qu_s[rows, :] = jnp.where(own[kvh], qug, jnp.zeros_like(qug))
            qa_s[rows, :LANES] = jnp.where(own[kvh], qrg, jnp.zeros_like(qrg))

    for c0 in range(0, r, CHAIN_ROWS):
        cs = pl.ds(c0, CHAIN_ROWS)
        state = (jnp.full((CHAIN_ROWS, LANES), NEG, F32), jnp.zeros((CHAIN_ROWS, LANES), F32),
                 jnp.zeros((CHAIN_ROWS, LANES), F32))
        for off, tk in WIN_TILES:
            rows = pl.ds(pl.multiple_of(t0 + off, TQ), tk)
            kpos = t0 + off - WINDOW + lax.broadcasted_iota(jnp.int32, (1, tk), 1)
            mask = (kpos >= 0) & (kpos <= t0 + rpos) & (kpos > t0 + rpos - WINDOW)
            state = _attend(state, _dot_nt(qa_s[cs, :LANES], kw_ref[rows, :]), vw_ref[rows, :], mask)
        owin_s[cs, :] = state[2] / jnp.maximum(state[1], 1e-30)

    ckv = ckv_ref[...]
    n_cmp_pad = ckv.shape[0]
    c_ok = (lax.broadcasted_iota(jnp.int32, (1, n_cmp_pad), 1) * CMP_STRIDE + (CMP_BLOCK - 1)) <= (t0 + rpos)
    s = _dot_nt(qu_s[...], ckv[:, :LANES].astype(BF16)).reshape(NSA_HEADS, TQ, n_cmp_pad)
    p = _masked_softmax(s, c_ok[None])
    o_cmp = _dot(p.reshape(r, n_cmp_pad).astype(BF16), ckv[:, LANES:].astype(BF16))
    p4 = p.reshape(NSA_GROUP, NSA_KV_HEADS * TQ, n_cmp_pad)
    psum = (p4[0] + p4[1]) + (p4[2] + p4[3])
    imp_t = _dot_nt(covt_ref[...], psum, precision=HIGHEST)
    tp = t0 + (lax.broadcasted_iota(jnp.int32, (n_sel, NSA_KV_HEADS * TQ), 1) % TQ)
    sel_t = _select_t(imp_t[:n_sel], tp, n_sel)
    unsel_t = jnp.concatenate([1.0 - sel_t, jnp.zeros((LANES - n_sel, NSA_KV_HEADS * TQ), F32)], axis=0)
    unsel = unsel_t.T.astype(BF16)
    for g in range(NSA_GROUP):
        qa_s[pl.ds(g * NSA_KV_HEADS * TQ, NSA_KV_HEADS * TQ), LANES:] = unsel

    fresh = (jnp.full((r, LANES), NEG, F32), jnp.zeros((r, LANES), F32), jnp.zeros((r, LANES), F32))

    del fresh
    m_s[...] = jnp.full(m_s.shape, NEG, F32)
    l_s[...] = jnp.zeros(l_s.shape, F32)
    acc_s[...] = jnp.zeros(acc_s.shape, F32)

    def slc_tile(kt, causal):
        rows = pl.ds(pl.multiple_of(kt * TK_SLC, TK_SLC), TK_SLC)
        k_aug = jnp.concatenate([ks_ref[rows, :], eb_ref[rows, :]], axis=1)
        v_t = vs_ref[rows, :]
        mask = None
        if causal:
            mask = (kt * TK_SLC + lax.broadcasted_iota(jnp.int32, (1, TK_SLC), 1)) <= (t0 + rpos)
        chain = CHAIN_ROWS if causal else r
        for c0 in range(0, r, chain):
            cs = pl.ds(c0, chain)
            state = _attend((m_s[cs, :], l_s[cs, :], acc_s[cs, :]), _dot_nt(qa_s[cs, :], k_aug), v_t, mask)
            m_s[cs, :], l_s[cs, :], acc_s[cs, :] = state

    n_full = t0 // TK_SLC

    def slc_body(kt, c):
        slc_tile(kt, False)
        return c

    lax.fori_loop(0, n_full, slc_body, 0)
    slc_tile(n_full, True)
    o_slc = acc_s[...] / jnp.maximum(l_s[...], 1e-30)

    gt = gate_ref[...]
    g_hi = gt.astype(BF16)
    g_lo = (gt - g_hi.astype(F32)).astype(BF16)
    gexp = _dot(jnp.concatenate([g_hi, g_lo], axis=1), ge_ref[...])
    o_win = owin_s[...]
    for g in range(NSA_GROUP):
        lo = slice((g * NSA_KV_HEADS) * TQ, (g * NSA_KV_HEADS + 1) * TQ)
        hi = slice((g * NSA_KV_HEADS + 1) * TQ, (g * NSA_KV_HEADS + 2) * TQ)
        out = None
        for i, o_i in enumerate((o_cmp, o_slc, o_win)):
            col = (i * NSA_GROUP + g) * LANES
            term = gexp[:, col:col + LANES] * jnp.where(own[0], o_i[lo], o_i[hi])
            out = term if out is None else out + term
        o_ref[:, g * LANES:(g + 1) * LANES] = out.astype(BF16)


def _gate_expand():
    ge = np.zeros((LANES, 3, NSA_GROUP, NSA_KV_HEADS, HEAD_DIM), np.float32)
    for kvh in range(NSA_KV_HEADS):
        for g in range(NSA_GROUP):
            for i in range(3):
                ge[3 * (kvh * NSA_GROUP + g) + i, i, g, kvh, :] = 1.0
    ge = ge.reshape(LANES, 3 * NSA_WIDTH)
    return jnp.asarray(np.concatenate([ge, ge], axis=0)).astype(BF16)


def _block_neg(n_keys):
    k = np.arange(n_keys)[:, None]
    j = np.arange(LANES)[None, :]
    return jnp.asarray(np.where(k // SEL_BLOCK == j, NEG, 0.0).astype(np.float32)).astype(BF16)


def _nsa_prompt(qu, qr, ckv, kvs_b, kvw_b, gates, b_sz, s_len):
    n_cmp_pad = s_len // CMP_STRIDE
    n_sel = s_len // SEL_BLOCK
    assert n_sel <= LANES and n_cmp_pad % LANES == 0 and s_len % TK_SLC == 0
    covt = _cover_t(n_cmp_pad, n_cmp_pad - 1, n_sel, LANES)
    eb = _block_neg(s_len)
    kvw_pad = jnp.pad(kvw_b.reshape(b_sz, s_len, KV_WIDTH), ((0, 0), (WINDOW, 0), (0, 0)))
    kvw_pad = kvw_pad.reshape(b_sz * (s_len + WINDOW), KV_WIDTH)
    nq = s_len // TQ
    r = NSA_HEADS * TQ
    tile = lambda b, q: (b * nq + q, 0)
    const = lambda b, q: (0, 0)
    return _pallas(
        functools.partial(_nsa_prompt_kernel, n_sel=n_sel),
        grid=(b_sz, nq),
        in_specs=[pl.BlockSpec((TQ, NSA_WIDTH), tile), pl.BlockSpec((TQ, NSA_WIDTH), tile),
                  pl.BlockSpec((n_cmp_pad, KV_WIDTH), lambda b, q: (b, 0)),
                  pl.BlockSpec((s_len, LANES), lambda b, q: (b, 0)), pl.BlockSpec((s_len, LANES), lambda b, q: (b, 1)),
                  pl.BlockSpec((s_len + WINDOW, LANES), lambda b, q: (b, 0)),
                  pl.BlockSpec((s_len + WINDOW, LANES), lambda b, q: (b, 1)),
                  pl.BlockSpec((TQ, LANES), tile),
                  pl.BlockSpec((LANES, n_cmp_pad), const), pl.BlockSpec((s_len, LANES), const),
                  pl.BlockSpec((2 * LANES, 3 * NSA_WIDTH), const)],
        out_specs=pl.BlockSpec((TQ, NSA_WIDTH), tile),
        out_shape=jax.ShapeDtypeStruct((b_sz * s_len, NSA_WIDTH), BF16),
        scratch_shapes=[pltpu.VMEM((r, LANES), BF16), pltpu.VMEM((r, 2 * LANES), BF16),
                        pltpu.VMEM((r, LANES), F32), pltpu.VMEM((r, LANES), F32), pltpu.VMEM((r, LANES), F32),
                        pltpu.VMEM((r, LANES), F32)],
        compiler_params=_cparams(("parallel", "arbitrary")),
    )(qu, qr, ckv, kvs_b, kvs_b, kvw_pad, kvw_pad, gates, covt, eb, _gate_expand())


def _block_expand(n_sel_pad, n_keys):
    j = np.arange(n_sel_pad)[:, None]
    k = np.arange(n_keys)[None, :]
    return jnp.asarray((k // SEL_BLOCK == j).astype(np.float32)).astype(BF16)


def _hgrn_consts(c):
    n_lvl = int(math.log2(c))
    hc = HGRN_HEADS * c
    t = np.arange(hc) % c
    head = np.arange(hc) // c
    same_head = head[:, None] == head[None, :]
    u = t[None, :]
    mall = np.zeros(((1 + n_lvl) * hc, hc), np.float32)
    masks = np.zeros((n_lvl, hc, hc), np.float32)
    mall[:hc] = same_head & (u <= t[:, None])
    for li in range(n_lvl):
        n = c >> (li + 1)
        blk = t // (2 * n)
        up_start = blk * 2 * n + n
        upper = t >= up_start
        m_up = upper[:, None] & (u >= up_start[:, None]) & (u <= t[:, None])
        m_lo = (~upper)[:, None] & (u > t[:, None]) & (u < up_start[:, None])
        mall[(1 + li) * hc:(2 + li) * hc] = same_head & (m_up | m_lo)
        masks[li] = same_head & upper[:, None] & (~upper)[None, :] & (blk[:, None] == blk[None, :])
    return jnp.asarray(mall).astype(BF16), jnp.asarray(masks)


def _split2(x):
    a = x.astype(BF16)
    return a, (x - a.astype(F32)).astype(BF16)


def _lower_bound(lbl):
    e = jnp.exp(lbl - jnp.max(lbl, axis=0, keepdims=True))
    return e[0:1] / jnp.sum(e, axis=0, keepdims=True)


def _hgrn_prompt_kernel(hq_ref, hf_ref, hi_ref, hg_ref, lbl_ref, gn_ref, mall_ref, mask_ref, o_ref, s_ref, st_ref,
                        *, c, n_lvl, ts, n_seq):
    si = pl.program_id(1)

    @pl.when(si == 0)
    def _():
        st_ref[...] = jnp.zeros_like(st_ref)

    hc = HGRN_HEADS * c
    lb_all = _lower_bound(lbl_ref[...])
    lb = jnp.concatenate([jnp.broadcast_to(lb_all[:, h * LANES:(h + 1) * LANES], (c, LANES))
                          for h in range(HGRN_HEADS)], axis=0)
    gn = gn_ref[...]
    rowi = lax.broadcasted_iota(jnp.int32, (hc, 1), 0)

    def chunk(ci, carry):
        for sq in range(n_seq):
            seq_chunk(sq, pl.multiple_of(ci * c, c))
        return carry

    def seq_chunk(sq, r0):
        def stacked(ref):
            return jnp.concatenate([ref[sq, pl.ds(r0, c), h * LANES:(h + 1) * LANES] for h in range(HGRN_HEADS)],
                                   axis=0)

        q = jax.nn.silu(stacked(hq_ref)) * HGRN_DK ** -0.5
        f = lb + (1.0 - lb) * jax.nn.sigmoid(stacked(hf_ref))
        k = 1.0 - f
        v = stacked(hi_ref)
        a, b = _split2(jnp.log(f))
        sums = _dot(mall_ref[...], jnp.concatenate([a, b], axis=1))
        sums = sums[:, :LANES] + sums[:, LANES:]
        cum = sums[:hc]
        att = jnp.zeros((hc, hc), F32)
        for li in range(n_lvl):
            n = c >> (li + 1)
            e = jnp.exp(sums[(1 + li) * hc:(2 + li) * hc])
            zz = (jnp.where((rowi & n) != 0, q, k) * e).astype(BF16)
            att = att + _dot_nt(zz, zz) * mask_ref[li]
        o = _dot(att.astype(BF16), v.astype(BF16)) + jnp.sum(q * k, axis=-1, keepdims=True) * v
        qe = (q * jnp.exp(cum)).astype(BF16)
        o_heads = []
        for h in range(HGRN_HEADS):
            rows = slice(h * c, (h + 1) * c)
            st = st_ref[sq, h]
            o_heads.append(o[rows] + _dot_nt(qe[rows], st.astype(BF16)))
            last = cum[(h + 1) * c - 1:(h + 1) * c, :]
            kd = (k[rows] * jnp.exp(last - cum[rows])).astype(BF16)
            st_ref[sq, h] = st * jnp.exp(last) + _dot(v[rows].T.astype(BF16), kd)
        o = jnp.concatenate(o_heads, axis=0)
        y = o * lax.rsqrt(jnp.mean(o * o, axis=-1, keepdims=True) + NORM_EPS) * gn * jax.nn.silu(stacked(hg_ref))
        for h in range(HGRN_HEADS):
            o_ref[sq, pl.ds(r0, c), h * LANES:(h + 1) * LANES] = y[h * c:(h + 1) * c].astype(BF16)

    lax.fori_loop(0, ts // c, chunk, 0, unroll=HG_UNROLL)

    @pl.when(si == pl.num_programs(1) - 1)
    def _():
        for sq in range(n_seq):
            for h in range(HGRN_HEADS):
                s_ref[sq, h] = st_ref[sq, h].T


def _hgrn_prompt(zh, lb_logits, gnorm, b_sz, s_len):
    c = HG_CHUNK
    ts = min(HG_TS, s_len)
    ns = s_len // ts
    n_lvl = int(math.log2(c))
    mall, masks = _hgrn_consts(c)
    n_seq = HG_SEQS if b_sz % HG_SEQS == 0 else 1
    z3 = zh.reshape(b_sz, s_len, zh.shape[1])
    col = lambda k: (lambda b, s: (b, s, k))
    o_b, s_fin = _pallas(
        functools.partial(_hgrn_prompt_kernel, c=c, n_lvl=n_lvl, ts=ts, n_seq=n_seq),
        grid=(b_sz // n_seq, ns),
        in_specs=[pl.BlockSpec((n_seq, ts, HGRN_KW), col(0)), pl.BlockSpec((n_seq, ts, HGRN_KW), col(1)),
                  pl.BlockSpec((n_seq, ts, HGRN_VW), col(2)), pl.BlockSpec((n_seq, ts, HGRN_VW), col(3)),
                  pl.BlockSpec((lb_logits.shape[0], HGRN_KW), lambda b, s: (0, 0)),
                  pl.BlockSpec((1, LANES), lambda b, s: (0, 0)),
                  pl.BlockSpec(mall.shape, lambda b, s: (0, 0)),
                  pl.BlockSpec(masks.shape, lambda b, s: (0, 0, 0))],
        out_specs=[pl.BlockSpec((n_seq, ts, HGRN_VW), lambda b, s: (b, s, 0)),
                   pl.BlockSpec((n_seq, HGRN_HEADS, HGRN_DK, HGRN_DV), lambda b, s: (b, 0, 0, 0))],
        out_shape=[jax.ShapeDtypeStruct((b_sz, s_len, HGRN_VW), BF16),
                   jax.ShapeDtypeStruct((b_sz, HGRN_HEADS, HGRN_DK, HGRN_DV), F32)],
        scratch_shapes=[pltpu.VMEM((n_seq, HGRN_HEADS, HGRN_DV, HGRN_DK), F32)],
        compiler_params=_cparams(("parallel", "arbitrary")),
    )(z3, z3, z3, z3, lb_logits, gnorm, mall, masks)
    return o_b.reshape(b_sz * s_len, HGRN_VW), s_fin


_R_EXP0 = 8


def _post_kernel(x_ref, oa_ref, ob_ref, ma_ref, mb_ref, wpa_ref, wpb_ref, wo_ref, g2_ref, wr_ref, br_ref,
                 x1_ref, h2_ref, lg_ref):
    pa = _dot(oa_ref[...], wpa_ref[...])
    pb = _dot(ob_ref[...], wpb_ref[...])
    merged = jax.nn.sigmoid(ma_ref[...]) * pa + jax.nn.sigmoid(mb_ref[...]) * pb
    x1 = x_ref[...] + _dot(merged.astype(BF16), wo_ref[...])
    h2 = _rms(x1, g2_ref[...])
    x1_ref[...] = x1
    h2_ref[...] = h2
    h_hi = h2.astype(BF16)
    h_lo = (h2 - h_hi.astype(F32)).astype(BF16)
    lg = _dot(jnp.concatenate([h_hi, h_hi, h_lo], axis=1), wr_ref[...]) + br_ref[...]
    lg_ref[...] = lg.T


def _router_weights(w_rg, b_rg, w_re, b_re):
    wr = jnp.zeros((D_MODEL, LANES), F32)
    wr = wr.at[:, :N_GROUPS].set(w_rg)
    wr = wr.at[:, _R_EXP0:_R_EXP0 + N_EXPERTS].set(jnp.transpose(w_re, (1, 0, 2)).reshape(D_MODEL, N_EXPERTS))
    w_hi = wr.astype(BF16)
    w_lo = (wr - w_hi.astype(F32)).astype(BF16)
    br = jnp.zeros((1, LANES), F32)
    br = br.at[0, :N_GROUPS].set(b_rg)
    br = br.at[0, _R_EXP0:_R_EXP0 + N_EXPERTS].set(b_re.reshape(N_EXPERTS))
    return jnp.concatenate([w_hi, w_lo, w_hi], axis=0), br


def _post_mixer(x2, o_a, o_b, zm, wpa, wpb, wo, g2, wrt, br):
    n = x2.shape[0]
    tm = _tile(n, TM_POST)
    row = lambda i: (i, 0)
    const = lambda i: (0, 0)
    return _pallas(
        _post_kernel,
        grid=(n // tm,),
        in_specs=[pl.BlockSpec((tm, D_MODEL), row), pl.BlockSpec((tm, NSA_WIDTH), row),
                  pl.BlockSpec((tm, HGRN_VW), row),
                  pl.BlockSpec((tm, D_MODEL), lambda i: (i, 0)), pl.BlockSpec((tm, D_MODEL), lambda i: (i, 1)),
                  pl.BlockSpec((NSA_WIDTH, D_MODEL), const), pl.BlockSpec((HGRN_VW, D_MODEL), const),
                  pl.BlockSpec((D_MODEL, D_MODEL), const), pl.BlockSpec((1, D_MODEL), const),
                  pl.BlockSpec((3 * D_MODEL, LANES), const), pl.BlockSpec((1, LANES), const)],
        out_specs=[pl.BlockSpec((tm, D_MODEL), row), pl.BlockSpec((tm, D_MODEL), row),
                   pl.BlockSpec((LANES, tm), lambda i: (0, i))],
        out_shape=[jax.ShapeDtypeStruct((n, D_MODEL), F32), jax.ShapeDtypeStruct((n, D_MODEL), F32),
                   jax.ShapeDtypeStruct((LANES, n), F32)],
        compiler_params=_cparams(("parallel",)),
    )(x2, o_a, o_b, zm, zm, wpa, wpb, wo, g2, wrt, br)


def _lowest_argmax(vals, top):
    idx = jnp.full(top.shape, len(vals) - 1, jnp.int32)
    for i in range(len(vals) - 2, -1, -1):
        idx = jnp.where(vals[i] == top, i, idx)
    return idx


def _route_kernel(lg_ref, u_ref, eid_ref, gate_ref, rank_ref, cnt_ref, carry_ref):
    @pl.when(pl.program_id(0) == 0)
    def _():
        carry_ref[...] = jnp.zeros_like(carry_ref)

    lg = lg_ref[...]
    tr = lg.shape[1]
    grp = [lg[g:g + 1] for g in range(N_GROUPS)]
    mx = functools.reduce(jnp.maximum, grp)
    ex = [jnp.exp(v - mx) for v in grp]
    den = functools.reduce(lambda a, b: a + b, ex)
    pr = [e / den for e in ex]
    pg = functools.reduce(jnp.maximum, pr)
    gtop = _lowest_argmax(pr, pg)
    le = []
    for j in range(EXPERTS_PER_GROUP):
        v = lg[_R_EXP0 + j:_R_EXP0 + j + 1]
        for g in range(1, N_GROUPS):
            r = _R_EXP0 + g * EXPERTS_PER_GROUP + j
            v = jnp.where(gtop == g, lg[r:r + 1], v)
        le.append(v)
    m1 = functools.reduce(jnp.maximum, le)
    i1 = _lowest_argmax(le, m1)
    le2 = [jnp.where(i1 == j, -jnp.inf, le[j]) for j in range(EXPERTS_PER_GROUP)]
    m2 = functools.reduce(jnp.maximum, le2)
    i2 = _lowest_argmax(le2, m2)
    e2 = jnp.exp(m2 - m1)
    den2 = 1.0 + e2
    gate1 = (1.0 / den2) * pg
    gate2 = (e2 / den2) * pg
    eid1 = gtop * EXPERTS_PER_GROUP + i1
    eid2 = gtop * EXPERTS_PER_GROUP + i2

    eio = lax.broadcasted_iota(jnp.int32, (N_EXPERTS, tr), 0)
    oh1 = jnp.where(eio == eid1, 1.0, 0.0)
    oh2 = jnp.where(eio == eid2, 1.0, 0.0)
    both = (oh1 + oh2).astype(BF16)
    carry = carry_ref[...]
    before = _dot(both, u_ref[...]) + jnp.concatenate([carry] * (tr // LANES), axis=1)
    rank1 = jnp.sum(oh1 * before, axis=0, keepdims=True).astype(jnp.int32)
    rank2 = jnp.sum(oh2 * before, axis=0, keepdims=True).astype(jnp.int32)
    carry = carry + _dot(both, jnp.ones((tr, LANES), BF16))
    carry_ref[...] = carry
    cnt_ref[...] = carry
    zi = jnp.zeros((6, tr), jnp.int32)
    eid_ref[...] = jnp.concatenate([eid1, eid2, zi], axis=0)
    rank_ref[...] = jnp.concatenate([rank1, rank2, zi], axis=0)
    gate_ref[...] = jnp.concatenate([gate1, gate2, jnp.zeros((6, tr), F32)], axis=0)


def _route(lg):
    n = lg.shape[1]
    tr = _tile(n, TR_ROUTE, LANES)
    u = jnp.asarray(np.triu(np.ones((tr, tr), np.float32), 1)).astype(BF16)
    col = lambda i: (0, i)
    return _pallas(
        _route_kernel,
        grid=(n // tr,),
        in_specs=[pl.BlockSpec((LANES, tr), col), pl.BlockSpec((tr, tr), lambda i: (0, 0))],
        out_specs=[pl.BlockSpec((8, tr), col), pl.BlockSpec((8, tr), col), pl.BlockSpec((8, tr), col),
                   pl.BlockSpec((N_EXPERTS, LANES), lambda i: (0, 0))],
        out_shape=[jax.ShapeDtypeStruct((8, n), jnp.int32), jax.ShapeDtypeStruct((8, n), F32),
                   jax.ShapeDtypeStruct((8, n), jnp.int32), jax.ShapeDtypeStruct((N_EXPERTS, LANES), F32)],
        scratch_shapes=[pltpu.VMEM((N_EXPERTS, LANES), F32)],
        compiler_params=_cparams(("arbitrary",)),
    )(lg, u)


def _dest_kernel(ps_ref, eid_ref, rank_ref, dest_ref):
    eid = eid_ref[...]
    dest = rank_ref[...]
    for e in range(N_EXPERTS):
        dest = dest + jnp.where(eid == e, ps_ref[e], 0)
    dest_ref[...] = dest


def _dest_rows(pad_start, eid, rank):
    n = eid.shape[1]
    tr = _tile(n, 4096, LANES)
    col = lambda i: (0, i)
    return _pallas(
        _dest_kernel,
        grid=(n // tr,),
        in_specs=[pl.BlockSpec(memory_space=pltpu.SMEM), pl.BlockSpec((8, tr), col), pl.BlockSpec((8, tr), col)],
        out_specs=pl.BlockSpec((8, tr), col),
        out_shape=jax.ShapeDtypeStruct((8, n), jnp.int32),
        compiler_params=_cparams(("parallel",)),
    )(pad_start, eid, rank)


DMA_UNROLL = 8


def _issue_rows(t_n, row_copy):
    for t in range(t_n):
        for k in range(TOP_K_EXPERTS):
            row_copy(t, k).start(priority=k % 2)


def _wait_rows(t_n, row_copy):
    def wait(t, c):
        for k in range(TOP_K_EXPERTS):
            row_copy(t, k).wait()
        return c

    lax.fori_loop(0, t_n, wait, 0, unroll=DMA_UNROLL)


def _row_loops(t_n, row_copy):
    _issue_rows(t_n, row_copy)
    _wait_rows(t_n, row_copy)


def _dispatch_kernel(dest_ref, h_ref, xb_in_ref, xb_ref, sem):
    del xb_in_ref

    def row_copy(t, k):
        return pltpu.make_async_copy(h_ref.at[pl.ds(t, 1), :], xb_ref.at[pl.ds(dest_ref[k, t], 1), :], sem)

    _row_loops(h_ref.shape[0], row_copy)


def _smem_cols(t):
    return pl.BlockSpec((8, t), lambda i: (0, i), memory_space=pltpu.SMEM)


def _dispatch(dest, h2, rows_total):
    n = h2.shape[0]
    t = _tile(n, T_DISP, LANES)
    xb0 = jnp.zeros((rows_total, D_MODEL), F32)
    return _pallas(
        _dispatch_kernel,
        grid=(n // t,),
        in_specs=[_smem_cols(t), pl.BlockSpec((t, D_MODEL), lambda i: (i, 0)), pl.BlockSpec(memory_space=pl.ANY)],
        out_specs=pl.BlockSpec(memory_space=pl.ANY),
        out_shape=jax.ShapeDtypeStruct((rows_total, D_MODEL), F32),
        scratch_shapes=[pltpu.SemaphoreType.DMA(())],
        input_output_aliases={2: 0},
        compiler_params=_cparams(("arbitrary",)),
    )(dest, h2, xb0)


def _expert_kernel(blk_e_ref, x_ref, wg_ref, wu_ref, wd_ref, y_ref):
    del blk_e_ref
    x = x_ref[...].astype(BF16)
    hid = jax.nn.silu(_dot(x, wg_ref[0])) * _dot(x, wu_ref[0])
    y_ref[...] = _dot(hid.astype(BF16), wd_ref[0])


def _experts(blk_e, xb, wg, wu, wd):
    rows_total = xb.shape[0]
    wsel = lambda i, e: (e[i], 0, 0)
    grid_spec = pltpu.PrefetchScalarGridSpec(
        num_scalar_prefetch=1,
        grid=(rows_total // MOE_BM,),
        in_specs=[pl.BlockSpec((MOE_BM, D_MODEL), lambda i, e: (i, 0)),
                  pl.BlockSpec((1, D_MODEL, D_EXPERT), wsel), pl.BlockSpec((1, D_MODEL, D_EXPERT), wsel),
                  pl.BlockSpec((1, D_EXPERT, D_MODEL), wsel)],
        out_specs=pl.BlockSpec((MOE_BM, D_MODEL), lambda i, e: (i, 0)),
    )
    return _pallas(
        _expert_kernel,
        grid_spec=grid_spec,
        out_shape=jax.ShapeDtypeStruct((rows_total, D_MODEL), F32),
        compiler_params=_cparams(("arbitrary",)),
    )(blk_e, xb, wg, wu, wd)


def _combine_kernel(dest_ref, next_ref, x1_ref, gate_ref, fn_ref, yb_ref, y_ref, buf_ref, sem):
    i = pl.program_id(0)
    slot = i % 2
    t_n = x1_ref.shape[0]

    def row_copy(idx_ref, sl):
        def copy(t, k):
            return pltpu.make_async_copy(yb_ref.at[pl.ds(idx_ref[k, t], 1), :], buf_ref.at[sl, k, pl.ds(t, 1), :],
                                         sem.at[sl])
        return copy

    @pl.when(i == 0)
    def _():
        _issue_rows(t_n, row_copy(dest_ref, 0))

    @pl.when(i + 1 < pl.num_programs(0))
    def _():
        _issue_rows(t_n, row_copy(next_ref, 1 - slot))

    _wait_rows(t_n, row_copy(dest_ref, slot))
    g = gate_ref[...]
    out = x1_ref[...] + g[:, 0:1] * buf_ref[slot, 0] + g[:, 1:2] * buf_ref[slot, 1]
    y_ref[...] = _rms(out, fn_ref[...])


def _combine(dest, x1, gate_t, fnorm, yb):
    n = x1.shape[0]
    t = _tile(n, T_DISP, LANES)
    return _pallas(
        _combine_kernel,
        grid=(n // t,),
        in_specs=[_smem_cols(t),
                  pl.BlockSpec((8, t), lambda i: (0, jnp.minimum(i + 1, n // t - 1)), memory_space=pltpu.SMEM),
                  pl.BlockSpec((t, D_MODEL), lambda i: (i, 0)), pl.BlockSpec((t, 8), lambda i: (i, 0)),
                  pl.BlockSpec((1, D_MODEL), lambda i: (0, 0)), pl.BlockSpec(memory_space=pl.ANY)],
        out_specs=pl.BlockSpec((t, D_MODEL), lambda i: (i, 0)),
        out_shape=jax.ShapeDtypeStruct((n, D_MODEL), F32),
        scratch_shapes=[pltpu.VMEM((2, TOP_K_EXPERTS, t, D_MODEL), F32), pltpu.SemaphoreType.DMA((2,))],
        compiler_params=_cparams(("arbitrary",)),
    )(dest, dest, x1, gate_t, fnorm, yb)


def _moe_and_final(x1, h2, lg, wg, wu, wd, fnorm):
    n = x1.shape[0]
    eid, gate, rank, cnt = _route(lg)
    counts = cnt[:, 0].astype(jnp.int32)
    padded = (counts + MOE_BM - 1) // MOE_BM * MOE_BM
    pad_end = jnp.cumsum(padded)
    pad_start = (pad_end - padded).astype(jnp.int32)
    n_blocks = -(-(n * TOP_K_EXPERTS + N_EXPERTS * (MOE_BM - 1)) // MOE_BM)
    blk_start = jnp.arange(n_blocks, dtype=jnp.int32) * MOE_BM
    blk_e = jnp.minimum(jnp.sum(pad_end[None, :] <= blk_start[:, None], axis=1), N_EXPERTS - 1).astype(jnp.int32)
    dest = _dest_rows(pad_start, eid, rank)
    xb = _dispatch(dest, h2, n_blocks * MOE_BM)
    yb = _experts(blk_e, xb, wg, wu, wd)
    return _combine(dest, x1, gate.T, fnorm, yb)


def _page_stream(pt_ref, pool_ref, buf_ref, sem, n_pages):
    b = pl.program_id(0)
    slot = b % 2

    def page_copy(bb, sl, p):
        col = pl.ds(pl.multiple_of(p * PAGE_SIZE, PAGE_SIZE), PAGE_SIZE)
        return pltpu.make_async_copy(pool_ref.at[pt_ref[bb, p]], buf_ref.at[sl, :, col], sem.at[sl])

    def issue(bb, sl):
        def body(p, c):
            page_copy(bb, sl, p).start()
            return c
        lax.fori_loop(0, n_pages, body, 0, unroll=DMA_UNROLL)

    def wait():
        def body(p, c):
            page_copy(b, slot, p).wait()
            return c
        lax.fori_loop(0, n_pages, body, 0, unroll=DMA_UNROLL)

    @pl.when(b == 0)
    def _():
        issue(0, 0)

    @pl.when(b + 1 < pl.num_programs(0))
    def _():
        issue(b + 1, 1 - slot)

    return slot, wait


TR_BLOCK = 256


def _chunk_perm():
    n_loc = TR_BLOCK // CMP_STRIDE
    r = np.arange(TR_BLOCK)
    perm = np.zeros((TR_BLOCK, TR_BLOCK), np.float32)
    perm[r, (r % n_loc) * CMP_STRIDE + r // n_loc] = 1.0
    return jnp.asarray(perm).astype(BF16)


def _compress_sample_kernel(pt_ref, pool_ref, perm_ref, w1bd_ref, w2bd_ref, c0_ref, b2_ref, out_ref,
                            buf_ref, ak_ref, av_ref, sem, *, n_pages, n_ch):
    slot, wait = _page_stream(pt_ref, pool_ref, buf_ref, sem, n_pages)
    wait()
    n_loc = TR_BLOCK // CMP_STRIDE

    def block(j, carry):
        xt = buf_ref[slot, :, pl.ds(pl.multiple_of(j * TR_BLOCK, TR_BLOCK), TR_BLOCK)].astype(BF16)
        rows = _dot_nt(perm_ref[...], xt)
        dst = pl.ds(pl.multiple_of(j * n_loc, n_loc), n_loc)
        for s in range(CMP_STRIDE):
            piece = rows[s * n_loc:(s + 1) * n_loc]
            ak_ref[dst, s * LANES:(s + 1) * LANES] = piece[:, :LANES].astype(BF16)
            av_ref[dst, s * LANES:(s + 1) * LANES] = piece[:, LANES:].astype(BF16)
        return carry

    lax.fori_loop(0, n_pages * PAGE_SIZE // TR_BLOCK, block, 0, unroll=4)
    for c, a_ref in ((0, ak_ref), (1, av_ref)):
        _compress_mlp(c, a_ref, n_ch, w1bd_ref, w2bd_ref, c0_ref, b2_ref, out_ref)


def _compress_sample(page_table, pool, cw):
    b_sz, n_pages = page_table.shape
    past = n_pages * PAGE_SIZE
    n_ch = past // CMP_STRIDE
    z3 = lambda b, pt: (0, 0, 0)
    specs = [pl.BlockSpec((2, CMP_STRIDE * LANES, 4 * LANES), z3), pl.BlockSpec((2, 2 * LANES, LANES), z3),
             pl.BlockSpec((2, 1, 2 * LANES), z3), pl.BlockSpec((2, 1, LANES), z3)]
    grid_spec = pltpu.PrefetchScalarGridSpec(
        num_scalar_prefetch=1,
        grid=(b_sz,),
        in_specs=[pl.BlockSpec(memory_space=pl.ANY), pl.BlockSpec((TR_BLOCK, TR_BLOCK), lambda b, pt: (0, 0))] + specs,
        out_specs=pl.BlockSpec((n_ch, KV_WIDTH), lambda b, pt: (b, 0)),
        scratch_shapes=[pltpu.VMEM((2, KV_WIDTH, past), F32), pltpu.VMEM((n_ch, CMP_STRIDE * LANES), BF16),
                        pltpu.VMEM((n_ch, CMP_STRIDE * LANES), BF16), pltpu.SemaphoreType.DMA((2,))],
    )
    assert past % TR_BLOCK == 0
    return _pallas(
        functools.partial(_compress_sample_kernel, n_pages=n_pages, n_ch=n_ch),
        grid_spec=grid_spec,
        out_shape=jax.ShapeDtypeStruct((b_sz * n_ch, KV_WIDTH), F32),
        compiler_params=_cparams(("arbitrary",)),
    )(page_table, pool, _chunk_perm(), *cw)


def _cmp_sample_kernel(qu_ref, ckv_ref, covt_ref, ocmp_ref, selt_ref, ps_s, *, past, t_len, n_sel, n_grp, n_cmp_pad):
    r = qu_ref.shape[1]
    rows_kt = NSA_KV_HEADS * t_len
    tpos = past + (lax.broadcasted_iota(jnp.int32, (r, 1), 0) % t_len)
    c_end = lax.broadcasted_iota(jnp.int32, (1, n_cmp_pad), 1) * CMP_STRIDE + (CMP_BLOCK - 1)

    def one_seq(sq, carry):
        ckv = ckv_ref[pl.ds(pl.multiple_of(sq * n_cmp_pad, n_cmp_pad), n_cmp_pad), :]
        p = _masked_softmax(_dot_nt(qu_ref[sq], ckv[:, :LANES].astype(BF16)), c_end <= tpos)
        ocmp_ref[sq] = _dot(p.astype(BF16), ckv[:, LANES:].astype(BF16))
        psum = []
        for kvh in range(NSA_KV_HEADS):
            acc = None
            for g in range(NSA_GROUP):
                blk = g * NSA_KV_HEADS + kvh
                pg = p[blk * t_len:(blk + 1) * t_len]
                acc = pg if acc is None else acc + pg
            psum.append(acc)
        ps_s[pl.ds(pl.multiple_of(sq * rows_kt, rows_kt), rows_kt), :] = jnp.concatenate(psum, axis=0)
        return carry

    lax.fori_loop(0, n_grp, one_seq, 0)
    imp_t = _dot_nt(covt_ref[...], ps_s[...], precision=HIGHEST)
    tp = past + (lax.broadcasted_iota(jnp.int32, imp_t.shape, 1) % t_len)
    selt_ref[0] = _select_t(imp_t, tp, n_sel)


def _cmp_sample(qu32, ckv, b_sz, past, t_len, n_sel, n_sel_pad):
    n_cmp_pad = past // CMP_STRIDE
    covt = _cover_t(n_cmp_pad, n_cmp_pad - 1, n_sel, n_sel_pad)
    r = qu32.shape[1]
    rows_kt = NSA_KV_HEADS * t_len
    n_grp = math.gcd(b_sz, LANES // rows_kt)
    assert rows_kt % 8 == 0
    ocmp, sel_t = _pallas(
        functools.partial(_cmp_sample_kernel, past=past, t_len=t_len, n_sel=n_sel, n_grp=n_grp, n_cmp_pad=n_cmp_pad),
        grid=(b_sz // n_grp,),
        in_specs=[pl.BlockSpec((n_grp, r, LANES), lambda b: (b, 0, 0)),
                  pl.BlockSpec((n_grp * n_cmp_pad, KV_WIDTH), lambda b: (b, 0)),
                  pl.BlockSpec((n_sel_pad, n_cmp_pad), lambda b: (0, 0))],
        out_specs=[pl.BlockSpec((n_grp, r, LANES), lambda b: (b, 0, 0)),
                   pl.BlockSpec((1, n_sel_pad, n_grp * rows_kt), lambda b: (b, 0, 0))],
        out_shape=[jax.ShapeDtypeStruct((b_sz, r, LANES), F32),
                   jax.ShapeDtypeStruct((b_sz // n_grp, n_sel_pad, n_grp * rows_kt), F32)],
        scratch_shapes=[pltpu.VMEM((n_grp * rows_kt, n_cmp_pad), F32)],
        compiler_params=_cparams(("parallel",)),
    )(qu32, ckv, covt)
    sel = sel_t.reshape(b_sz // n_grp, n_sel_pad, n_grp, rows_kt).transpose(0, 2, 3, 1)
    return ocmp, sel.reshape(b_sz, rows_kt, n_sel_pad)


TK_SAMPLE = 1024


def _slc_win_sample_kernel(pt_ref, qr_ref, qa_ref, sel_ref, e_ref, pool_ref, new_s_ref, win_ref, new_w_ref, ocmp_ref,
                           gate_ref, o_ref, buf_ref, sem, *, n_pages, past, t_len, win_buf):
    slot, wait_pages = _page_stream(pt_ref, pool_ref, buf_ref, sem, n_pages)
    qr = qr_ref[0]
    sel = sel_ref[0]
    r = qr.shape[0]
    tpos = past + (lax.broadcasted_iota(jnp.int32, (r, 1), 0) % t_len)
    init = (jnp.full((r, 1), NEG, F32), jnp.zeros((r, 1), F32), jnp.zeros((r, LANES), F32))
    new_i = lax.broadcasted_iota(jnp.int32, (1, LANES), 1)
    new_ok = ((past + new_i) <= tpos) & (new_i < t_len)

    wt = win_ref[0]
    wpos = past - win_buf + lax.broadcasted_iota(jnp.int32, (1, win_buf), 1)
    ok = (wpos >= 0) & (wpos <= tpos) & (wpos > tpos - WINDOW)
    carry = _online_step(_dot(qr, wt[:LANES].astype(BF16)), ok, wt[LANES:].astype(BF16), *init, v_t=True)
    rows = new_w_ref[0]
    ok = new_ok & ((past + new_i) > tpos - WINDOW)
    m, l, acc = _online_step(_dot_nt(qr, rows[:, :LANES].astype(BF16)), ok, rows[:, LANES:].astype(BF16), *carry)
    o_win = acc / jnp.maximum(l, 1e-30)

    wait_pages()

    qa = qa_ref[0]

    def slc_body(kt, carry):
        col = pl.ds(pl.multiple_of(kt * TK_SAMPLE, TK_SAMPLE), TK_SAMPLE)
        k_aug = jnp.concatenate([buf_ref[slot, :LANES, col].astype(BF16), e_ref[:, col]], axis=0)
        return _online_step(_dot(qa, k_aug), None, buf_ref[slot, LANES:, col].astype(BF16), *carry, v_t=True)

    carry = lax.fori_loop(0, past // TK_SAMPLE, slc_body, init, unroll=2)
    rows = new_s_ref[0]
    n_past_blk = past // SEL_BLOCK
    ok = new_ok & (sel[:, n_past_blk:n_past_blk + 1] > 0.5)
    m, l, acc = _online_step(_dot_nt(qr, rows[:, :LANES].astype(BF16)), ok, rows[:, LANES:].astype(BF16), *carry)
    o_slc = acc / jnp.maximum(l, 1e-30)

    g = gate_ref[0]
    o = g[:, 0:1] * ocmp_ref[0] + g[:, 1:2] * o_slc + g[:, 2:3] * o_win
    row = lax.broadcasted_iota(jnp.int32, (r, LANES), 0)
    lane = lax.broadcasted_iota(jnp.int32, (r, LANES), 1)
    o_ref[0] = jnp.where((((row // t_len) % 2) == 0) == (lane < HEAD_DIM), o, 0.0)


def _slc_win_sample(page_table, qr32, sel32, pool, new_s, win_t, new_w, ocmp32, gate32, past, t_len, n_sel_pad):
    b_sz, n_pages = page_table.shape
    win_buf = win_t.shape[2]
    r = qr32.shape[1]
    n_past_blk = past // SEL_BLOCK
    assert n_past_blk <= LANES
    unsel = jnp.pad(1.0 - sel32[:, :, :n_past_blk].astype(F32), ((0, 0), (0, 0), (0, LANES - n_past_blk)))
    qa32 = jnp.concatenate([qr32, unsel.astype(BF16)], axis=2)
    e_mat = _block_neg(past).T
    b3 = lambda b, pt: (b, 0, 0)
    grid_spec = pltpu.PrefetchScalarGridSpec(
        num_scalar_prefetch=1,
        grid=(b_sz,),
        in_specs=[pl.BlockSpec((1, r, LANES), b3), pl.BlockSpec((1, r, 2 * LANES), b3),
                  pl.BlockSpec((1, r, n_sel_pad), b3),
                  pl.BlockSpec((LANES, past), lambda b, pt: (0, 0)),
                  pl.BlockSpec(memory_space=pl.ANY),
                  pl.BlockSpec((1, LANES, KV_WIDTH), b3),
                  pl.BlockSpec((1, KV_WIDTH, win_buf), b3),
                  pl.BlockSpec((1, LANES, KV_WIDTH), b3),
                  pl.BlockSpec((1, r, LANES), b3), pl.BlockSpec((1, r, 3), b3)],
        out_specs=pl.BlockSpec((1, r, LANES), b3),
        scratch_shapes=[pltpu.VMEM((2, KV_WIDTH, past), F32), pltpu.SemaphoreType.DMA((2,))],
    )
    return _pallas(
        functools.partial(_slc_win_sample_kernel, n_pages=n_pages, past=past, t_len=t_len, win_buf=win_buf),
        grid_spec=grid_spec,
        out_shape=jax.ShapeDtypeStruct((b_sz, r, LANES), F32),
        compiler_params=_cparams(("arbitrary",)),
    )(page_table, qr32, qa32, sel32, e_mat, pool, new_s, win_t, new_w, ocmp32, gate32)


def _rows32(q, b_sz, t_len):
    q5 = q.reshape(b_sz, t_len, NSA_GROUP, NSA_KV_HEADS, HEAD_DIM).transpose(0, 2, 3, 1, 4)
    eye = jnp.eye(NSA_KV_HEADS, dtype=q.dtype)
    return jnp.einsum('bgktd,kj->bgktjd', q5, eye).reshape(b_sz, NSA_HEADS * t_len, LANES)


def _from_rows32(o32, b_sz, t_len):
    o6 = o32.reshape(b_sz, NSA_GROUP, NSA_KV_HEADS, t_len, NSA_KV_HEADS, HEAD_DIM)
    o5 = jnp.stack([o6[:, :, k, :, k] for k in range(NSA_KV_HEADS)], axis=2)
    return o5.transpose(0, 3, 1, 2, 4).reshape(b_sz * t_len, NSA_WIDTH)


def _nsa_sample(qu, qr, kvs_new, kvw_new, gates, cache_cmp, cache_slc, cache_win, page_table, cw, t_len):
    b_sz, n_pages = page_table.shape
    past = n_pages * PAGE_SIZE
    total = past + t_len
    assert t_len < CMP_STRIDE and t_len <= SEL_BLOCK and past % TK_SAMPLE == 0
    n_sel = -(-total // SEL_BLOCK)
    n_sel_pad = -(-n_sel // 8) * 8
    pool_c = cache_cmp.reshape(-1, PAGE_SIZE, KV_WIDTH).transpose(0, 2, 1)
    pool_s = cache_slc.reshape(-1, PAGE_SIZE, KV_WIDTH).transpose(0, 2, 1)
    win_t = cache_win.reshape(b_sz, cache_win.shape[1], KV_WIDTH).transpose(0, 2, 1)
    ckv = _compress_sample(page_table, pool_c, cw)
    ocmp32, sel = _cmp_sample(_rows32(qu, b_sz, t_len), ckv, b_sz, past, t_len, n_sel, n_sel_pad)
    sel = sel.reshape(b_sz, 1, NSA_KV_HEADS * t_len, n_sel_pad)
    sel32 = jnp.broadcast_to(sel, (b_sz, NSA_GROUP, NSA_KV_HEADS * t_len, n_sel_pad))
    sel32 = sel32.reshape(b_sz, NSA_HEADS * t_len, n_sel_pad).astype(BF16)
    pad_rows = lambda a: jnp.pad(a.reshape(b_sz, t_len, KV_WIDTH), ((0, 0), (0, LANES - t_len), (0, 0)))
    g4 = gates[:, :3 * NSA_HEADS].reshape(b_sz, t_len, NSA_KV_HEADS, NSA_GROUP, 3)
    gate32 = g4.transpose(0, 3, 2, 1, 4).reshape(b_sz, NSA_HEADS * t_len, 3)
    o32 = _slc_win_sample(page_table, _rows32(qr, b_sz, t_len), sel32, pool_s, pad_rows(kvs_new), win_t,
                          pad_rows(kvw_new), ocmp32, gate32, past, t_len, n_sel_pad)
    return _from_rows32(o32, b_sz, t_len).astype(BF16)


def _hgrn_sample_kernel(hq_ref, hf_ref, vt_ref, hgt_ref, s0_ref, lbl_ref, gnt_ref, ot_ref, s_ref, *, t_len):
    lb = _lower_bound(lbl_ref[...])
    q_all = jax.nn.silu(hq_ref[0]) * HGRN_DK ** -0.5
    f_all = lb + (1.0 - lb) * jax.nn.sigmoid(hf_ref[0])
    for h in range(HGRN_HEADS):
        hs = slice(h * LANES, (h + 1) * LANES)
        q = q_all[:, hs]
        f = f_all[:, hs]
        k = 1.0 - f
        vt = vt_ref[0, h]
        st = s0_ref[0, h].T
        cols = []
        for t in range(t_len):
            st = st * f[t:t + 1] + vt[:, t:t + 1] * k[t:t + 1]
            cols.append(jnp.sum(st * q[t:t + 1], axis=1, keepdims=True))
        ot = jnp.concatenate(cols, axis=1)
        y = ot * lax.rsqrt(jnp.mean(ot * ot, axis=0, keepdims=True) + NORM_EPS) * gnt_ref[...]
        ot_ref[0, h] = y * jax.nn.silu(hgt_ref[0, h])
        s_ref[0, h] = st.T


def _hgrn_sample(zh, s0, lb_logits, gnorm, b_sz, t_len):
    z3 = zh.reshape(b_sz, t_len, 4 * HGRN_KW)
    to_t = lambda a: a.reshape(b_sz, t_len, HGRN_HEADS, HGRN_DV).transpose(0, 2, 3, 1)
    vt = to_t(z3[:, :, 2 * HGRN_KW:3 * HGRN_KW])
    hgt = to_t(z3[:, :, 3 * HGRN_KW:])
    col = lambda k: (lambda b: (b, 0, k))
    b4 = lambda b: (b, 0, 0, 0)
    ot, s_new = _pallas(
        functools.partial(_hgrn_sample_kernel, t_len=t_len),
        grid=(b_sz,),
        in_specs=[pl.BlockSpec((1, t_len, HGRN_KW), col(0)), pl.BlockSpec((1, t_len, HGRN_KW), col(1)),
                  pl.BlockSpec((1, HGRN_HEADS, HGRN_DV, t_len), b4), pl.BlockSpec((1, HGRN_HEADS, HGRN_DV, t_len), b4),
                  pl.BlockSpec((1, HGRN_HEADS, HGRN_DK, HGRN_DV), b4),
                  pl.BlockSpec((lb_logits.shape[0], HGRN_KW), lambda b: (0, 0)),
                  pl.BlockSpec((HGRN_DV, 1), lambda b: (0, 0))],
        out_specs=[pl.BlockSpec((1, HGRN_HEADS, HGRN_DV, t_len), b4),
                   pl.BlockSpec((1, HGRN_HEADS, HGRN_DK, HGRN_DV), b4)],
        out_shape=[jax.ShapeDtypeStruct((b_sz, HGRN_HEADS, HGRN_DV, t_len), F32),
                   jax.ShapeDtypeStruct((b_sz, HGRN_HEADS, HGRN_DK, HGRN_DV), F32)],
        compiler_params=_cparams(("parallel",)),
    )(z3, z3, vt, hgt, s0, lb_logits, gnorm.reshape(HGRN_DV, 1))
    o_b = ot.transpose(0, 3, 1, 2).reshape(b_sz * t_len, HGRN_VW)
    return o_b.astype(BF16), s_new


def kernel(x_prompt, x_sample, cache_cmp_kv, cache_slc_kv, cache_win_kv, state_hgrn, page_table, norm1, w_in, cmp_pe,
           cmp_w1, cmp_b1, cmp_w2, cmp_b2, hgrn_lb_logits, hgrn_gnorm, w_proj_a, w_proj_b, w_out, norm2,
           w_router_group, b_router_group, w_router_expert, b_router_expert, w_exp_gate, w_exp_up, w_exp_down,
           final_norm):
    assert w_in.shape[0] == 1, "one layer"
    b_sz, s_len, _ = x_prompt.shape
    d_sz, t_len, _ = x_sample.shape
    past = page_table.shape[1] * PAGE_SIZE
    n_p = b_sz * s_len
    n_s = d_sz * t_len
    kv_row = (2, NSA_KV_HEADS, HEAD_DIM)

    offs = np.cumsum((0,) + IN_SPLITS)
    w_att = _attn_weight(w_in[0])
    w_h = w_in[0][:, offs[5]:offs[9]].astype(BF16)
    w_m = w_in[0][:, offs[9]:offs[11]].astype(BF16)
    g1 = norm1[0][None]
    cw = _compress_weights(cmp_pe[0], cmp_w1[0], cmp_b1[0], cmp_w2[0], cmp_b2[0])
    perm = _pair_perm()
    wpa = w_proj_a[0][perm].astype(BF16)
    wpb = w_proj_b[0].astype(BF16)
    wo = w_out[0].astype(BF16)
    wrt, br = _router_weights(w_router_group[0], b_router_group[0], w_router_expert[0], b_router_expert[0])
    gn = hgrn_gnorm[0][None]

    xp = x_prompt.reshape(n_p, D_MODEL)
    cos, sin = _rope_tables(jnp.arange(s_len))
    qu, qr, kvc_p, kvs_t, kvw_t, kvs_b, kvw_b, gates, kvc_t = _attn_proj(xp, g1, w_att, cos, sin, seq_len=s_len)
    zh = _norm_proj(xp, g1, w_h)
    zm = _norm_proj(xp, g1, w_m)
    ckv = _compress_prompt(kvc_p, b_sz, s_len, cw)
    o_a = _nsa_prompt(qu, qr, ckv, kvs_b, kvw_b, gates, b_sz, s_len)
    o_b, hg_p = _hgrn_prompt(zh, hgrn_lb_logits, gn, b_sz, s_len)
    x1_p, h2_p, lg_p = _post_mixer(xp, o_a, o_b, zm, wpa, wpb, wo, norm2[0][None], wrt, br)

    xs = x_sample.reshape(n_s, D_MODEL)
    cos, sin = _rope_tables(past + jnp.tile(jnp.arange(t_len), d_sz))
    qu, qr, kvc_s, kvs_s, kvw_s, _, _, gates = _attn_proj(xs, g1, w_att, cos, sin)
    zh = _norm_proj(xs, g1, w_h)
    zm = _norm_proj(xs, g1, w_m)
    o_a = _nsa_sample(qu, qr, kvs_s, kvw_s, gates, cache_cmp_kv[0], cache_slc_kv[0], cache_win_kv[0], page_table, cw,
                      t_len)
    o_b, hg_s = _hgrn_sample(zh, state_hgrn[0], hgrn_lb_logits, hgrn_gnorm[0], d_sz, t_len)
    x1_s, h2_s, lg_s = _post_mixer(xs, o_a, o_b, zm, wpa, wpb, wo, norm2[0][None], wrt, br)

    w_exp = (w_exp_gate[0].astype(BF16), w_exp_up[0].astype(BF16), w_exp_down[0].astype(BF16))
    y_p = _moe_and_final(x1_p, h2_p, lg_p, *w_exp, final_norm[None]).reshape(b_sz, s_len, D_MODEL)
    y_s = _moe_and_final(x1_s, h2_s, lg_s, *w_exp, final_norm[None]).reshape(d_sz, t_len, D_MODEL)

    def rows6(a_t):
        return a_t.reshape((a_t.shape[0],) + kv_row + (a_t.shape[2],)).transpose(0, 4, 1, 2, 3)[None]

    win_p = kvw_t[:, :, s_len - min(WINDOW, s_len):]
    win_rows = jnp.concatenate([cache_win_kv[0], kvw_s.reshape((d_sz, t_len) + kv_row)], axis=1)
    win_s = win_rows[:, win_rows.shape[1] - min(WINDOW, past + t_len):]
    return (y_p, y_s,
            rows6(kvc_t), kvc_s.reshape((1, d_sz, t_len) + kv_row),
            rows6(kvs_t), kvs_s.reshape((1, d_sz, t_len) + kv_row),
            rows6(win_p), win_s[None], hg_p[None], hg_s[None])
```
